```python
import math
import jax, jax.numpy as jnp
from jax import lax
import numpy as np

D_MODEL = 1024
BATCH = 32
SEQ = 2048
DEPTH = 1

H_A = 4
DK_A = 128
DV_A = 128
W_A = H_A * DV_A
CONV_W = 4
CHUNK = 64
H_B = 4
R_KV = 128
DV_B = 128
W_B = H_B * DV_B
H_IDX = 8
D_IDX = 64
TOPK_MAX = 256
Q_BLOCK = 128
N_BUCKETS = 32
MAX_EXACT = 16
MAX_DIST = 128
EPS = 1e-6
SPLITS = (3 * W_A, W_A, H_A, H_A, H_B * R_KV, R_KV, W_B, H_IDX * D_IDX, D_IDX, H_IDX)
SPLIT_OFFSETS = (3072 // 2, 2048, 2052, 2056, 2568, 2696, 3208, 3720, 3784)
D_IN = 3792

kernel_name = 'hybrid_gdn_dsa_parallel_heads'


def rmsnorm(x, g):
    xf = x.astype(jnp.float32)
    xf = xf * lax.rsqrt(jnp.mean(xf * xf, axis=-1, keepdims=True) + EPS)
    return xf.astype(x.dtype) * g


def l2norm(x):
    return x * lax.rsqrt(jnp.sum(x * x, axis=-1, keepdims=True) + EPS)


def causal_conv(u, w):
    L = u.shape[1]
    up = jnp.pad(u, ((0, 0), (CONV_W - 1, 0), (0, 0)))
    out = up[:, 0:L] * w[0]
    for j in range(1, CONV_W):
        out = out + up[:, j:j + L] * w[j]
    return out


def t5_bucket(dist):
    n = jnp.maximum(dist, 0)
    nf = jnp.maximum(n, 1).astype(jnp.float32)
    large = MAX_EXACT + (jnp.log(nf / MAX_EXACT) / math.log(MAX_DIST / MAX_EXACT)
                         * (N_BUCKETS - MAX_EXACT)).astype(jnp.int32)
    large = jnp.minimum(large, N_BUCKETS - 1)
    return jnp.where(n < MAX_EXACT, n, large)


def gated_deltanet(qkv, z, b, a, conv_w, a_log, dt_bias, g_norm):
    Bsz, L, _ = qkv.shape
    n_chunks = L // CHUNK
    qkv = jax.nn.silu(causal_conv(qkv, conv_w))
    q, k, v = jnp.split(qkv, 3, axis=-1)

    def heads(t, d):
        t = t.reshape(Bsz, L, H_A, d).transpose(0, 2, 1, 3).astype(jnp.float32)
        return t.reshape(Bsz, H_A, n_chunks, CHUNK, d)

    q = l2norm(heads(q, DK_A)) * (DK_A ** -0.5)
    k = l2norm(heads(k, DK_A))
    v = heads(v, DV_A)
    beta = jax.nn.sigmoid(b.astype(jnp.float32)).transpose(0, 2, 1).reshape(Bsz, H_A, n_chunks, CHUNK)
    g = -jnp.exp(a_log.astype(jnp.float32)) * jax.nn.softplus(a.astype(jnp.float32) + dt_bias.astype(jnp.float32))
    g = g.transpose(0, 2, 1).reshape(Bsz, H_A, n_chunks, CHUNK)
    gc = jnp.cumsum(g, axis=-1)
    pos = jnp.arange(CHUNK)
    causal = pos[:, None] >= pos[None, :]
    strict = pos[:, None] > pos[None, :]
    decay = jnp.exp(jnp.where(causal, gc[..., :, None] - gc[..., None, :], -jnp.inf))
    kb = k * beta[..., None]
    vb = v * beta[..., None]
    lmat = jnp.where(strict, jnp.einsum('bhnik,bhnjk->bhnij', kb, k) * decay, 0.0)
    eye = jnp.eye(CHUNK, dtype=jnp.float32)
    tmat = lax.linalg.triangular_solve(lmat + eye, jnp.broadcast_to(eye, lmat.shape),
                                       left_side=True, lower=True, unit_diagonal=True)
    u = jnp.einsum('bhnij,bhnjd->bhnid', tmat, vb)
    w = jnp.einsum('bhnij,bhnjd->bhnid', tmat, kb * jnp.exp(gc)[..., None])
    attn = jnp.einsum('bhnik,bhnjk->bhnij', q, k) * decay
    qg = q * jnp.exp(gc)[..., None]
    kg = k * jnp.exp(gc[..., -1:] - gc)[..., None]
    glast = jnp.exp(gc[..., -1])
    xs = tuple(jnp.moveaxis(t, 2, 0) for t in (qg, kg, u, w, attn, glast))

    def step(S, inp):
        qg_c, kg_c, u_c, w_c, attn_c, gl_c = inp
        v_new = u_c - jnp.einsum('bhck,bhkv->bhcv', w_c, S)
        o = jnp.einsum('bhck,bhkv->bhcv', qg_c, S) + jnp.einsum('bhij,bhjv->bhiv', attn_c, v_new)
        S = S * gl_c[..., None, None] + jnp.einsum('bhck,bhcv->bhkv', kg_c, v_new)
        return S, o

    S0 = jnp.zeros((Bsz, H_A, DK_A, DV_A), jnp.float32)
    _, o = lax.scan(step, S0, xs)
    o = jnp.moveaxis(o, 0, 2).reshape(Bsz, H_A, L, DV_A).transpose(0, 2, 1, 3)
    o = rmsnorm(o, g_norm.astype(jnp.float32))
    o = o.reshape(Bsz, L, W_A) * jax.nn.silu(z.astype(jnp.float32))
    return o.astype(qkv.dtype)


def dsa_sparse_attention(q_b, ckv, iq, ik, iw, g_kv, w_uv, rel_bias, z):
    Bsz, L, _ = q_b.shape
    n_blocks = L // Q_BLOCK
    k_top = min(TOPK_MAX, L // 4)
    ckv = rmsnorm(ckv, g_kv)
    q_b = q_b.reshape(Bsz, L, H_B, R_KV)
    iq = iq.reshape(Bsz, L, H_IDX, D_IDX)
    iw = iw * (H_IDX ** -0.5 * D_IDX ** -0.5)
    key_pos = jnp.arange(L, dtype=jnp.int32)

    def to_blocks(t):
        return jnp.moveaxis(t.reshape((Bsz, n_blocks, Q_BLOCK) + t.shape[2:]), 1, 0)

    def block(args):
        qb, iqb, iwb, t0 = args
        t = t0 + jnp.arange(Q_BLOCK, dtype=jnp.int32)
        rel = jnp.einsum('bqhd,bsd->bqhs', iqb, ik)
        score = jnp.einsum('bqh,bqhs->bqs', iwb.astype(jnp.float32), jax.nn.relu(rel).astype(jnp.float32))
        score = jnp.where(key_pos[None, None, :] <= t[None, :, None], score, -jnp.inf)
        _, sel = lax.top_k(score, k_top)
        kv_sel = jax.vmap(lambda kv, i: kv[i])(ckv, sel)
        dist = t[None, :, None] - sel
        bias = jnp.moveaxis(rel_bias[t5_bucket(dist)], 3, 2)
        logits = (jnp.einsum('bqhr,bqkr->bqhk', qb, kv_sel).astype(jnp.float32) * (R_KV ** -0.5)
                  + bias.astype(jnp.float32))
        logits = jnp.where((dist >= 0)[:, :, None, :], logits, -jnp.inf)
        p = jax.nn.softmax(logits, axis=-1).astype(kv_sel.dtype)
        return jnp.einsum('bqhk,bqkr->bqhr', p, kv_sel)

    t0s = jnp.arange(n_blocks, dtype=jnp.int32) * Q_BLOCK
    o = lax.map(block, (to_blocks(q_b), to_blocks(iq), to_blocks(iw), t0s))
    o = jnp.moveaxis(o, 0, 1).reshape(Bsz, L, H_B, R_KV)
    o = jnp.einsum('blhr,hrd->blhd', o, w_uv).reshape(Bsz, L, W_B)
    return o * jax.nn.silu(z)


def setup_inputs(seed: int = 0) -> dict:
    key = jax.random.key(seed)
    ks = jax.random.split(key, 16)
    nrm = jax.random.normal
    x = nrm(ks[0], (BATCH, SEQ, D_MODEL), jnp.float32)
    c = nrm(ks[1], (BATCH, D_MODEL), jnp.float32)
    w_ada = nrm(ks[2], (DEPTH, D_MODEL, 3 * D_MODEL), jnp.float32) * (0.5 * D_MODEL ** -0.5)
    b_ada = 0.01 * nrm(ks[3], (DEPTH, 3 * D_MODEL), jnp.float32)
    g_pre = 1.0 + 0.05 * nrm(ks[4], (DEPTH, D_MODEL), jnp.float32)
    w_in = nrm(ks[5], (DEPTH, D_MODEL, D_IN), jnp.float32) * (D_MODEL ** -0.5)
    conv_w = nrm(ks[6], (DEPTH, CONV_W, 3 * W_A), jnp.float32) * (CONV_W ** -0.5)
    a_log = jnp.log(jax.random.uniform(ks[7], (DEPTH, H_A), jnp.float32, 1.0, 16.0))
    dt = jnp.exp(jax.random.uniform(ks[8], (DEPTH, H_A), jnp.float32, math.log(1e-3), math.log(1e-1)))
    dt_bias = dt + jnp.log(-jnp.expm1(-dt))
    g_gdn = 1.0 + 0.05 * nrm(ks[9], (DEPTH, DV_A), jnp.float32)
    g_kv = 1.0 + 0.05 * nrm(ks[10], (DEPTH, R_KV), jnp.float32)
    w_uv = nrm(ks[11], (DEPTH, H_B, R_KV, DV_B), jnp.float32) * (R_KV ** -0.5)
    rel_bias = 0.5 * nrm(ks[12], (N_BUCKETS, H_B), jnp.float32)
    w_out = nrm(ks[13], (DEPTH, W_A + W_B, D_MODEL), jnp.float32) * ((W_A + W_B) ** -0.5)
    g_post = 1.0 + 0.05 * nrm(ks[14], (DEPTH, D_MODEL), jnp.float32)
    return {'x': x, 'c': c, 'w_ada': w_ada, 'b_ada': b_ada, 'g_pre': g_pre, 'w_in': w_in,
            'conv_w': conv_w, 'a_log': a_log, 'dt_bias': dt_bias, 'g_gdn': g_gdn, 'g_kv': g_kv,
            'w_uv': w_uv, 'rel_bias': rel_bias, 'w_out': w_out, 'g_post': g_post}


def reference(x, c, w_ada, b_ada, g_pre, w_in, conv_w, a_log, dt_bias, g_gdn, g_kv, w_uv, rel_bias, w_out, g_post):
    for layer in range(DEPTH):
        mod = jax.nn.silu(c) @ w_ada[layer] + b_ada[layer]
        shift, scale, gate = jnp.split(mod, 3, axis=-1)
        h = rmsnorm(x, g_pre[layer]) * (1.0 + scale[:, None, :]) + shift[:, None, :]
        proj = h @ w_in[layer]
        (qkv_a, z_a, b_a, a_a, q_b, ckv, z_b, iq, ik, iw) = jnp.split(proj, list(SPLIT_OFFSETS), axis=-1)
        o_a = gated_deltanet(qkv_a, z_a, b_a, a_a, conv_w[layer], a_log[layer], dt_bias[layer], g_gdn[layer])
        o_b = dsa_sparse_attention(q_b, ckv, iq, ik, iw, g_kv[layer], w_uv[layer], rel_bias, z_b)
        mix = jnp.concatenate([o_a, o_b], axis=-1) @ w_out[layer]
        x = x + gate[:, None, :] * rmsnorm(mix, g_post[layer])
    return x
```

```python
import functools
import math

import jax
import jax.numpy as jnp
from jax import lax
from jax.experimental import pallas as pl
from jax.experimental.pallas import tpu as pltpu

F32 = jnp.float32
BF16 = jnp.bfloat16
I32 = jnp.int32
HIGHEST = lax.Precision.HIGHEST

H_A, DK_A, DV_A = 4, 128, 128
W_A = H_A * DV_A
CONV_W = 4
CHUNK = 64
H_B, R_KV, DV_B = 4, 128, 128
W_B = H_B * DV_B
H_IDX, D_IDX = 8, 64
TOPK_MAX = 256
N_BUCKETS, MAX_EXACT, MAX_DIST = 32, 16, 128
EPS = 1e-6

LANES = 128
TQ = 128
VMEM_LIMIT = 52 * 1024 * 1024

C_QKV = (0, 3 * W_A)
C_ZA = (C_QKV[1], C_QKV[1] + W_A)
C_BA = (C_ZA[1], C_ZA[1] + LANES)
C_QB = (C_BA[1], C_BA[1] + H_B * R_KV)
C_CKV = (C_QB[1], C_QB[1] + R_KV)
C_ZB = (C_CKV[1], C_CKV[1] + W_B)
C_IQ = (C_ZB[1], C_ZB[1] + H_IDX * D_IDX)
C_IKW = (C_IQ[1], C_IQ[1] + LANES)
D_IN_PAD = C_IKW[1]
IW_LANE = D_IDX

NEG_INF_KEY = -2139095041


def _sigmoid(x):
    return 1.0 / (1.0 + jnp.exp(-x))


def _silu(x):
    return x * _sigmoid(x)


def _softplus(x):
    return jnp.maximum(x, 0.0) + jnp.log1p(jnp.exp(-jnp.abs(x)))


def _dot(a, b, precision=None):
    return jnp.dot(a, b, precision=precision, preferred_element_type=F32)


def _dot_nt(a, b, precision=None):
    return lax.dot_general(a, b, (((1,), (1,)), ((), ())), precision=precision,
                           preferred_element_type=F32)


def _dot_tn(a, b, precision=None):
    return lax.dot_general(a, b, (((0,), (0,)), ((), ())), precision=precision,
                           preferred_element_type=F32)


def _mod_kernel(c_ref, w_ref, b_ref, o_ref):
    c = c_ref[...]
    o_ref[...] = _dot(_silu(c), w_ref[...], HIGHEST) + b_ref[...]


def _modulation(c, w_ada, b_ada):
    bsz, d = c.shape
    n = w_ada.shape[1]
    tn = 512
    return pl.pallas_call(
        _mod_kernel,
        grid=(n // tn,),
        in_specs=[pl.BlockSpec((bsz, d), lambda j: (0, 0)),
                  pl.BlockSpec((d, tn), lambda j: (0, j)),
                  pl.BlockSpec((1, tn), lambda j: (0, j))],
        out_specs=pl.BlockSpec((bsz, tn), lambda j: (0, j)),
        out_shape=jax.ShapeDtypeStruct((bsz, n), F32),
        compiler_params=pltpu.CompilerParams(dimension_semantics=("arbitrary",),
                                             vmem_limit_bytes=VMEM_LIMIT),
        name="mod",
    )(c, w_ada, b_ada.reshape(1, n))


def _proj_kernel(x_ref, mod_ref, g_ref, w_ref,
                 qkv_ref, za_ref, ba_ref, qb_ref, ckv_ref, zb_ref, iq_ref, ikw_ref):
    x = x_ref[0]
    ms = jnp.mean(x * x, axis=-1, keepdims=True)
    xn = x * lax.rsqrt(ms + EPS)
    shift = mod_ref[0, 0:1, :]
    scale = mod_ref[0, 1:2, :]
    h = (xn * g_ref[...]) * (1.0 + scale) + shift
    hb = h.astype(BF16)

    def mm(cols):
        return _dot(hb, w_ref[:, cols[0]:cols[1]])

    qkv_ref[0] = mm(C_QKV)
    za_ref[0] = mm(C_ZA)
    ba_ref[0] = mm(C_BA)
    qb_ref[0] = mm(C_QB).astype(BF16)
    ckv_ref[0] = mm(C_CKV)
    zb_ref[0] = mm(C_ZB)
    iq_ref[0] = mm(C_IQ).astype(BF16)
    ikw_ref[0] = mm(C_IKW)


def _projection(x, mod3, g_pre, w_pad, tm):
    bsz, seq, d = x.shape
    widths = [(C_QKV, F32), (C_ZA, F32), (C_BA, F32), (C_QB, BF16),
              (C_CKV, F32), (C_ZB, F32), (C_IQ, BF16), (C_IKW, F32)]
    out_shape = [jax.ShapeDtypeStruct((bsz, seq, c[1] - c[0]), dt) for c, dt in widths]
    out_specs = [pl.BlockSpec((1, tm, c[1] - c[0]), lambda b, i: (b, i, 0)) for c, _ in widths]
    return pl.pallas_call(
        _proj_kernel,
        grid=(bsz, seq // tm),
        in_specs=[pl.BlockSpec((1, tm, d), lambda b, i: (b, i, 0)),
                  pl.BlockSpec((1, 3, d), lambda b, i: (b, 0, 0)),
                  pl.BlockSpec((1, d), lambda b, i: (0, 0)),
                  pl.BlockSpec((d, D_IN_PAD), lambda b, i: (0, 0))],
        out_specs=out_specs,
        out_shape=out_shape,
        compiler_params=pltpu.CompilerParams(dimension_semantics=("arbitrary", "arbitrary"),
                                             vmem_limit_bytes=VMEM_LIMIT),
        name="proj",
    )(x, mod3, g_pre.reshape(1, d), w_pad)


def _tri_inverse(lmat, level_masks, eye, precision):
    t = eye - jnp.where(level_masks[0], lmat, 0.0)
    for m in level_masks[1:]:
        c = jnp.where(m, lmat, 0.0)
        t = t - _dot(t, _dot(c, t, precision), precision)
    return t


def _gdn_kernel(qkv_ref, za_ref, ba_ref, cw_ref, alog_ref, dtb_ref, gn_ref, o_ref,
                ext_ref, s_ref, gc_ref, beta_ref, gct_ref, *, tl, precision):
    n_ch = tl // CHUNK
    li = pl.program_id(1)

    @pl.when(li == 0)
    def _():
        ext_ref[0:8, :] = jnp.zeros((8, 3 * W_A), F32)
        s_ref[...] = jnp.zeros_like(s_ref)

    @pl.when(li > 0)
    def _():
        ext_ref[0:8, :] = ext_ref[tl:tl + 8, :]

    ext_ref[8:8 + tl, :] = qkv_ref[0]

    ba = ba_ref[0]
    beta_ref[...] = _sigmoid(ba)
    g = -jnp.exp(alog_ref[...]) * _softplus(ba + dtb_ref[...])
    r = lax.broadcasted_iota(I32, (tl, tl), 0)
    cc = lax.broadcasted_iota(I32, (tl, tl), 1)
    shift = CHUNK.bit_length() - 1
    tri = jnp.where((cc <= r) & ((r >> shift) == (cc >> shift)), 1.0, 0.0)
    gc = _dot(tri, g, HIGHEST)
    gc_ref[...] = gc
    gct = gc.T
    for c in range(n_ch):
        gct_ref[c] = gct[:, c * CHUNK:(c + 1) * CHUNK]

    row = lax.broadcasted_iota(I32, (CHUNK, CHUNK), 0)
    col = lax.broadcasted_iota(I32, (CHUNK, CHUNK), 1)
    causal = row >= col
    strict = row > col
    eye = jnp.where(row == col, 1.0, 0.0)
    level_masks = []
    s = 1
    while s < CHUNK:
        ls = s.bit_length() - 1
        level_masks.append(((row >> (ls + 1)) == (col >> (ls + 1)))
                           & (((row >> ls) & 1) == 1) & (((col >> ls) & 1) == 0))
        s *= 2

    def chunk_body(c, carry):
        base = pl.multiple_of(c * CHUNK, CHUNK)
        gc_c = gc_ref[pl.ds(base, CHUNK), :]
        beta_c = beta_ref[pl.ds(base, CHUNK), :]
        gct_c = gct_ref[c]
        for h in range(H_A):
            def conv_silu(sec):
                c0 = sec * W_A + h * DK_A
                win = ext_ref[pl.ds(base, CHUNK + 8), c0:c0 + DK_A]
                acc = None
                for j in range(CONV_W):
                    sh = CONV_W - 1 - j
                    u = win if sh == 0 else pltpu.roll(win, sh, axis=0)
                    term = u[8:8 + CHUNK] * cw_ref[j:j + 1, c0:c0 + DK_A]
                    acc = term if acc is None else acc + term
                return _silu(acc)

            q = conv_silu(0)
            k = conv_silu(1)
            v = conv_silu(2)
            q = q * lax.rsqrt(jnp.sum(q * q, axis=-1, keepdims=True) + EPS) * (DK_A ** -0.5)
            k = k * lax.rsqrt(jnp.sum(k * k, axis=-1, keepdims=True) + EPS)
            beta = beta_c[:, h:h + 1]
            gcol = gc_c[:, H_A + h:H_A + h + 1]
            grow = gct_c[H_A + h:H_A + h + 1, :]
            glast = gc_c[CHUNK - 1:CHUNK, H_A + h:H_A + h + 1]
            decay = jnp.exp(jnp.where(causal, gcol - grow, -jnp.inf))
            ecol = jnp.exp(gcol)
            kb = k * beta
            vb = v * beta
            a2 = _dot_nt(jnp.concatenate([kb, q], axis=0), k, precision)
            lmat = jnp.where(strict, a2[:CHUNK] * decay, 0.0)
            attn = a2[CHUNK:] * decay
            tmat = _tri_inverse(lmat, level_masks, eye, precision)
            uw = _dot(tmat, jnp.concatenate([vb, kb * ecol], axis=1), precision)
            u = uw[:, :DV_A]
            w = uw[:, DV_A:]
            qg = q * ecol
            kg = k * jnp.exp(glast - gcol)
            st = s_ref[h]
            ws = _dot(jnp.concatenate([w, qg], axis=0), st, precision)
            v_new = u - ws[:CHUNK]
            o = ws[CHUNK:] + _dot(attn, v_new, precision)
            s_ref[h] = st * jnp.exp(glast) + _dot_tn(kg, v_new, precision)
            on = o * lax.rsqrt(jnp.mean(o * o, axis=-1, keepdims=True) + EPS) * gn_ref[...]
            z = za_ref[0, pl.ds(base, CHUNK), h * DV_A:(h + 1) * DV_A]
            o_ref[0, pl.ds(base, CHUNK), h * DV_A:(h + 1) * DV_A] = on * _silu(z)
        return carry

    lax.fori_loop(0, n_ch, chunk_body, 0)


def _gated_deltanet(qkv, za, ba, conv_w, alog_vec, dtb_vec, g_norm, tl, precision):
    bsz, seq, _ = qkv.shape
    kern = functools.partial(_gdn_kernel, tl=tl, precision=precision)
    return pl.pallas_call(
        kern,
        grid=(bsz, seq // tl),
        in_specs=[pl.BlockSpec((1, tl, 3 * W_A), lambda b, i: (b, i, 0)),
                  pl.BlockSpec((1, tl, W_A), lambda b, i: (b, i, 0)),
                  pl.BlockSpec((1, tl, LANES), lambda b, i: (b, i, 0)),
                  pl.BlockSpec((CONV_W, 3 * W_A), lambda b, i: (0, 0)),
                  pl.BlockSpec((1, LANES), lambda b, i: (0, 0)),
                  pl.BlockSpec((1, LANES), lambda b, i: (0, 0)),
                  pl.BlockSpec((1, DV_A), lambda b, i: (0, 0))],
        out_specs=pl.BlockSpec((1, tl, W_A), lambda b, i: (b, i, 0)),
        out_shape=jax.ShapeDtypeStruct((bsz, seq, W_A), F32),
        scratch_shapes=[pltpu.VMEM((tl + 8, 3 * W_A), F32),
                        pltpu.VMEM((H_A, DK_A, DV_A), F32),
                        pltpu.VMEM((tl, LANES), F32),
                        pltpu.VMEM((tl, LANES), F32),
                        pltpu.VMEM((tl // CHUNK, LANES, CHUNK), F32)],
        compiler_params=pltpu.CompilerParams(dimension_semantics=("arbitrary", "arbitrary"),
                                             vmem_limit_bytes=VMEM_LIMIT),
        name="gdn",
    )(qkv, za, ba, conv_w, alog_vec, dtb_vec, g_norm.reshape(1, DV_A))


def _bias_kernel(rb_ref, o_ref):
    d = pl.program_id(0)
    qi = lax.broadcasted_iota(I32, (TQ, TQ), 0)
    kj = lax.broadcasted_iota(I32, (TQ, TQ), 1)
    dist = d * TQ + qi - kj
    n = jnp.maximum(dist, 0)
    nf = jnp.maximum(n, 1).astype(F32)
    large = MAX_EXACT + (jnp.log(nf / MAX_EXACT) / math.log(MAX_DIST / MAX_EXACT)
                         * (N_BUCKETS - MAX_EXACT)).astype(I32)
    large = jnp.minimum(large, N_BUCKETS - 1)
    bucket = jnp.where(n < MAX_EXACT, n, large)
    for h in range(H_B):
        acc = jnp.zeros((TQ, TQ), F32)
        for kb in range(N_BUCKETS):
            acc = jnp.where(bucket == kb, rb_ref[kb, h], acc)
        o_ref[0, h * TQ:(h + 1) * TQ, :] = acc


def _bias_tiles(rel_bias, n_diag):
    return pl.pallas_call(
        _bias_kernel,
        grid=(n_diag,),
        in_specs=[pl.BlockSpec(memory_space=pltpu.SMEM)],
        out_specs=pl.BlockSpec((1, H_B * TQ, TQ), lambda d: (d, 0, 0)),
        out_shape=jax.ShapeDtypeStruct((n_diag, H_B * TQ, TQ), F32),
        compiler_params=pltpu.CompilerParams(dimension_semantics=("arbitrary",)),
        name="bias",
    )(rel_bias)


def _dsa_kernel(iq_ref, ikwq_ref, qb_ref, zb_ref, ikw_ref, ckv_ref, gkv_ref, wuv_ref, bias_ref,
                o_ref,
                kvn_ref, iklo_ref, ikhi_ref, iwe_ref, iwo_ref, key_ref, am_ref, lg_ref,
                acc_ref, lsum_ref, *, seq, k_top):
    qi = pl.program_id(1)
    n_kc = qi + 1
    hq = H_B * TQ

    @pl.when(qi == 0)
    def _():
        rows = 256
        for r0 in range(0, seq, rows):
            ckv = ckv_ref[0, r0:r0 + rows, :]
            ms = jnp.mean(ckv * ckv, axis=-1, keepdims=True)
            kvn_ref[r0:r0 + rows, :] = ((ckv * lax.rsqrt(ms + EPS)) * gkv_ref[...]).astype(BF16)
            ikw = ikw_ref[0, r0:r0 + rows, :]
            lane = lax.broadcasted_iota(I32, ikw.shape, 1)
            lo = jnp.where(lane < D_IDX, ikw, 0.0)
            iklo_ref[r0:r0 + rows, :] = lo.astype(BF16)
            ikhi_ref[r0:r0 + rows, :] = pltpu.roll(lo, D_IDX, axis=1).astype(BF16)

    iq = iq_ref[0]
    iq4 = jnp.concatenate([iq[:, p * LANES:(p + 1) * LANES] for p in range(H_IDX // 2)], axis=0)
    iw = ikwq_ref[0] * (H_IDX ** -0.5 * D_IDX ** -0.5)
    for p in range(H_IDX // 2):
        le = IW_LANE + 2 * p
        iwe_ref[p * TQ:(p + 1) * TQ, :] = jnp.broadcast_to(iw[:, le:le + 1], (TQ, LANES))
        iwo_ref[p * TQ:(p + 1) * TQ, :] = jnp.broadcast_to(iw[:, le + 1:le + 2], (TQ, LANES))

    row_t = qi * TQ + lax.broadcasted_iota(I32, (TQ, LANES), 0)
    lane_i = lax.broadcasted_iota(I32, (TQ, LANES), 1)

    def causal_mask(c):
        return (c * TQ + lane_i) <= row_t

    def score_chunk(c, carry):
        kb = pl.multiple_of(c * TQ, TQ)
        re = _dot_nt(iq4, iklo_ref[pl.ds(kb, TQ), :])
        ro = _dot_nt(iq4, ikhi_ref[pl.ds(kb, TQ), :])
        t = jnp.maximum(re, 0.0) * iwe_ref[...] + jnp.maximum(ro, 0.0) * iwo_ref[...]
        s = (t[0:TQ] + t[TQ:2 * TQ]) + (t[2 * TQ:3 * TQ] + t[3 * TQ:4 * TQ])
        s = jnp.where(s == 0.0, 0.0, s)
        s = jnp.where(causal_mask(c), s, -jnp.inf)
        bits = pltpu.bitcast(s, I32)
        key_ref[c] = jnp.where(bits < 0, bits ^ 0x7FFFFFFF, bits)
        return carry

    lax.fori_loop(0, n_kc, score_chunk, 0)

    def count_ge(cand):
        def body(c, acc):
            return acc + jnp.where(key_ref[c] >= cand, 1.0, 0.0)
        acc = lax.fori_loop(0, n_kc, body, jnp.zeros((TQ, LANES), F32))
        return jnp.broadcast_to(jnp.sum(acc, axis=1, keepdims=True), (TQ, LANES))

    sign = jnp.int32(-2 ** 31)
    kf = float(k_top)

    def bit_body(i, tu):
        bit = lax.shift_left(jnp.int32(1), 31 - i)
        cand_u = tu | bit
        cnt = count_ge(cand_u ^ sign)
        return jnp.where(cnt >= kf, cand_u, tu)

    tu = lax.fori_loop(0, 32, bit_body, jnp.zeros((TQ, LANES), I32))
    thr = tu ^ sign

    def count_both(c, carry):
        cg, ce = carry
        key = key_ref[c]
        return (cg + jnp.where(key > thr, 1.0, 0.0), ce + jnp.where(key == thr, 1.0, 0.0))

    cg, ce = lax.fori_loop(0, n_kc, count_both,
                           (jnp.zeros((TQ, LANES), F32), jnp.zeros((TQ, LANES), F32)))
    cnt_gt = jnp.broadcast_to(jnp.sum(cg, axis=1, keepdims=True), (TQ, LANES))
    cnt_eq = jnp.broadcast_to(jnp.sum(ce, axis=1, keepdims=True), (TQ, LANES))
    need = kf - cnt_gt
    simple = (cnt_eq <= need) | ((thr == NEG_INF_KEY) & (row_t < k_top))
    all_simple = jnp.min(jnp.where(simple, 1.0, 0.0)) > 0.5

    @pl.when(all_simple)
    def _():
        def body(c, carry):
            sel = (key_ref[c] >= thr) & causal_mask(c)
            am_ref[c] = jnp.where(sel, 0.0, -jnp.inf)
            return carry
        lax.fori_loop(0, n_kc, body, 0)

    @pl.when(jnp.logical_not(all_simple))
    def _():
        ii = lax.broadcasted_iota(I32, (TQ, TQ), 0)
        jj = lax.broadcasted_iota(I32, (TQ, TQ), 1)
        upper = jnp.where(ii <= jj, 1.0, 0.0).astype(BF16)

        def body(c, seen):
            key = key_ref[c]
            eq = key == thr
            eqf = jnp.where(eq, 1.0, 0.0)
            rank = seen + _dot(eqf.astype(BF16), upper)
            sel = (key > thr) | (eq & (rank <= need))
            am_ref[c] = jnp.where(sel & causal_mask(c), 0.0, -jnp.inf)
            return seen + jnp.broadcast_to(jnp.sum(eqf, axis=1, keepdims=True), (TQ, LANES))
        lax.fori_loop(0, n_kc, body, jnp.zeros((TQ, LANES), F32))

    qb = qb_ref[0]
    q4 = jnp.concatenate([qb[:, h * R_KV:(h + 1) * R_KV] for h in range(H_B)], axis=0)
    scale = R_KV ** -0.5

    def pass1(c, m):
        kb = pl.multiple_of(c * TQ, TQ)
        kv = kvn_ref[pl.ds(kb, TQ), :]
        lg = _dot_nt(q4, kv) * scale + bias_ref[qi - c]
        am = am_ref[c]
        lg = lg + jnp.concatenate([am] * H_B, axis=0)
        lg_ref[c] = lg
        return jnp.maximum(m, lg)

    m = lax.fori_loop(0, n_kc, pass1, jnp.full((hq, LANES), -jnp.inf, F32))
    mrow = jnp.broadcast_to(jnp.max(m, axis=1, keepdims=True), (hq, LANES))
    acc_ref[...] = jnp.zeros_like(acc_ref)
    lsum_ref[...] = jnp.zeros_like(lsum_ref)

    def pass2(c, carry):
        kb = pl.multiple_of(c * TQ, TQ)
        p = jnp.exp(lg_ref[c] - mrow)
        lsum_ref[...] += p
        acc_ref[...] += _dot(p.astype(BF16), kvn_ref[pl.ds(kb, TQ), :])
        return carry

    lax.fori_loop(0, n_kc, pass2, 0)
    o4 = acc_ref[...] / jnp.sum(lsum_ref[...], axis=1, keepdims=True)
    for h in range(H_B):
        oh = o4[h * TQ:(h + 1) * TQ].astype(BF16)
        y = _dot(oh, wuv_ref[h])
        z = zb_ref[0, :, h * DV_B:(h + 1) * DV_B]
        o_ref[0, :, h * DV_B:(h + 1) * DV_B] = y * _silu(z)


def _dsa_attention(iq, ikw, qb, zb, ckv, g_kv, w_uv_bf16, bias_tiles, k_top):
    bsz, seq, _ = iq.shape
    n_q = seq // TQ
    hq = H_B * TQ
    kern = functools.partial(_dsa_kernel, seq=seq, k_top=k_top)
    return pl.pallas_call(
        kern,
        grid=(bsz, n_q),
        in_specs=[pl.BlockSpec((1, TQ, H_IDX * D_IDX), lambda b, i: (b, i, 0)),
                  pl.BlockSpec((1, TQ, LANES), lambda b, i: (b, i, 0)),
                  pl.BlockSpec((1, TQ, H_B * R_KV), lambda b, i: (b, i, 0)),
                  pl.BlockSpec((1, TQ, W_B), lambda b, i: (b, i, 0)),
                  pl.BlockSpec((1, seq, LANES), lambda b, i: (b, 0, 0)),
                  pl.BlockSpec((1, seq, R_KV), lambda b, i: (b, 0, 0)),
                  pl.BlockSpec((1, R_KV), lambda b, i: (0, 0)),
                  pl.BlockSpec((H_B, R_KV, DV_B), lambda b, i: (0, 0, 0)),
                  pl.BlockSpec((n_q, hq, TQ), lambda b, i: (0, 0, 0))],
        out_specs=pl.BlockSpec((1, TQ, W_B), lambda b, i: (b, i, 0)),
        out_shape=jax.ShapeDtypeStruct((bsz, seq, W_B), F32),
        scratch_shapes=[pltpu.VMEM((seq, R_KV), BF16),
                        pltpu.VMEM((seq, LANES), BF16),
                        pltpu.VMEM((seq, LANES), BF16),
                        pltpu.VMEM((hq, LANES), F32),
                        pltpu.VMEM((hq, LANES), F32),
                        pltpu.VMEM((n_q, TQ, TQ), I32),
                        pltpu.VMEM((n_q, TQ, TQ), F32),
                        pltpu.VMEM((n_q, hq, TQ), F32),
                        pltpu.VMEM((hq, R_KV), F32),
                        pltpu.VMEM((hq, LANES), F32)],
        compiler_params=pltpu.CompilerParams(dimension_semantics=("arbitrary", "arbitrary"),
                                             vmem_limit_bytes=VMEM_LIMIT),
        name="dsa",
    )(iq, ikw, qb, zb, ikw, ckv, g_kv.reshape(1, R_KV), w_uv_bf16, bias_tiles)


def _out_kernel(x_ref, oa_ref, ob_ref, mod_ref, g_ref, w_ref, o_ref):
    mix_in = jnp.concatenate([oa_ref[0], ob_ref[0]], axis=-1).astype(BF16)
    mix = _dot(mix_in, w_ref[...])
    ms = jnp.mean(mix * mix, axis=-1, keepdims=True)
    normed = (mix * lax.rsqrt(ms + EPS)) * g_ref[...]
    o_ref[0] = x_ref[0] + mod_ref[0, 2:3, :] * normed


def _output(x, o_a, o_b, mod3, g_post, w_out_bf16, tm):
    bsz, seq, d = x.shape
    return pl.pallas_call(
        _out_kernel,
        grid=(bsz, seq // tm),
        in_specs=[pl.BlockSpec((1, tm, d), lambda b, i: (b, i, 0)),
                  pl.BlockSpec((1, tm, W_A), lambda b, i: (b, i, 0)),
                  pl.BlockSpec((1, tm, W_B), lambda b, i: (b, i, 0)),
                  pl.BlockSpec((1, 3, d), lambda b, i: (b, 0, 0)),
                  pl.BlockSpec((1, d), lambda b, i: (0, 0)),
                  pl.BlockSpec((W_A + W_B, d), lambda b, i: (0, 0))],
        out_specs=pl.BlockSpec((1, tm, d), lambda b, i: (b, i, 0)),
        out_shape=jax.ShapeDtypeStruct((bsz, seq, d), F32),
        compiler_params=pltpu.CompilerParams(dimension_semantics=("arbitrary", "arbitrary"),
                                             vmem_limit_bytes=VMEM_LIMIT),
        name="outproj",
    )(x, o_a, o_b, mod3, g_post.reshape(1, d), w_out_bf16)


def _pad_lanes(w):
    return jnp.pad(w, ((0, 0), (0, LANES - w.shape[1])))


def _pad_input_projection(w_in):
    o_ba = 3 * W_A + W_A
    o_qb = o_ba + 2 * H_A
    o_ik = o_qb + H_B * R_KV + R_KV + W_B + H_IDX * D_IDX
    return jnp.concatenate([w_in[:, :o_ba], _pad_lanes(w_in[:, o_ba:o_qb]),
                            w_in[:, o_qb:o_ik], _pad_lanes(w_in[:, o_ik:])], axis=1)


def _head_lanes(v):
    return jnp.zeros((1, LANES), F32).at[0, H_A:2 * H_A].set(v.astype(F32))


def kernel(x, c, w_ada, b_ada, g_pre, w_in, conv_w, a_log, dt_bias, g_gdn, g_kv, w_uv, rel_bias, w_out, g_post):
    bsz, seq, d = x.shape
    depth = w_ada.shape[0]
    assert seq % TQ == 0 and seq % CHUNK == 0
    k_top = min(TOPK_MAX, seq // 4)
    tm = min(512, seq)
    tl = min(512, seq)
    bias_tiles = _bias_tiles(rel_bias, seq // TQ)
    for layer in range(depth):
        mod3 = _modulation(c, w_ada[layer], b_ada[layer]).reshape(bsz, 3, d)
        w_pad = _pad_input_projection(w_in[layer]).astype(BF16)
        qkv, za, ba, qb, ckv, zb, iq, ikw = _projection(x, mod3, g_pre[layer], w_pad, tm)
        o_a = _gated_deltanet(qkv, za, ba, conv_w[layer], _head_lanes(a_log[layer]),
                              _head_lanes(dt_bias[layer]), g_gdn[layer], tl, HIGHEST)
        o_b = _dsa_attention(iq, ikw, qb, zb, ckv, g_kv[layer], w_uv[layer].astype(BF16),
                             bias_tiles, k_top)
        x = _output(x, o_a, o_b, mod3, g_post[layer], w_out[layer].astype(BF16), tm)
    return x
```

```python
import functools
import math

import jax
import jax.numpy as jnp
from jax import lax
from jax.experimental import pallas as pl
from jax.experimental.pallas import tpu as pltpu

F32 = jnp.float32
BF16 = jnp.bfloat16
I32 = jnp.int32
HIGHEST = lax.Precision.HIGHEST

H_A, DK_A, DV_A = 4, 128, 128
W_A = H_A * DV_A
CONV_W = 4
CHUNK = 64
H_B, R_KV, DV_B = 4, 128, 128
W_B = H_B * DV_B
H_IDX, D_IDX = 8, 64
TOPK_MAX = 256
N_BUCKETS, MAX_EXACT, MAX_DIST = 32, 16, 128
EPS = 1e-6

LANES = 128
TQ = 128
VMEM_LIMIT = 52 * 1024 * 1024

C_QKV = (0, 3 * W_A)
C_ZA = (C_QKV[1], C_QKV[1] + W_A)
C_BA = (C_ZA[1], C_ZA[1] + LANES)
C_QB = (C_BA[1], C_BA[1] + H_B * R_KV)
C_CKV = (C_QB[1], C_QB[1] + R_KV)
C_ZB = (C_CKV[1], C_CKV[1] + W_B)
C_IQ = (C_ZB[1], C_ZB[1] + H_IDX * D_IDX)
C_IKW = (C_IQ[1], C_IQ[1] + LANES)
D_IN_PAD = C_IKW[1]
IW_LANE = D_IDX

NEG_INF_KEY = -2139095041


def _sigmoid(x):
    return 1.0 / (1.0 + jnp.exp(-x))


def _silu(x):
    return x * _sigmoid(x)


def _softplus(x):
    return jnp.maximum(x, 0.0) + jnp.log1p(jnp.exp(-jnp.abs(x)))


def _dot(a, b, precision=None):
    return jnp.dot(a, b, precision=precision, preferred_element_type=F32)


def _dot_nt(a, b, precision=None):
    return lax.dot_general(a, b, (((1,), (1,)), ((), ())), precision=precision,
                           preferred_element_type=F32)


def _dot_tn(a, b, precision=None):
    return lax.dot_general(a, b, (((0,), (0,)), ((), ())), precision=precision,
                           preferred_element_type=F32)


def _bdot(a, b, precision=None):
    return _dot(a.astype(BF16), b.astype(BF16))


def _bdot_nt(a, b, precision=None):
    return _dot_nt(a.astype(BF16), b.astype(BF16))


def _bdot_tn(a, b, precision=None):
    return _dot_tn(a.astype(BF16), b.astype(BF16))


def _mod_kernel(c_ref, w_ref, b_ref, o_ref):
    c = c_ref[...]
    o_ref[...] = _dot(_silu(c), w_ref[...], HIGHEST) + b_ref[...]


def _modulation(c, w_ada, b_ada):
    bsz, d = c.shape
    n = w_ada.shape[1]
    tn = 512
    return pl.pallas_call(
        _mod_kernel,
        grid=(n // tn,),
        in_specs=[pl.BlockSpec((bsz, d), lambda j: (0, 0)),
                  pl.BlockSpec((d, tn), lambda j: (0, j)),
                  pl.BlockSpec((1, tn), lambda j: (0, j))],
        out_specs=pl.BlockSpec((bsz, tn), lambda j: (0, j)),
        out_shape=jax.ShapeDtypeStruct((bsz, n), F32),
        compiler_params=pltpu.CompilerParams(dimension_semantics=("arbitrary",),
                                             vmem_limit_bytes=VMEM_LIMIT),
        name="mod",
    )(c, w_ada, b_ada.reshape(1, n))


def _proj_kernel(x_ref, mod_ref, g_ref, w_ref,
                 qkv_ref, za_ref, ba_ref, qb_ref, ckv_ref, zb_ref, iq_ref, ikw_ref):
    x = x_ref[0]
    ms = jnp.mean(x * x, axis=-1, keepdims=True)
    xn = x * lax.rsqrt(ms + EPS)
    shift = mod_ref[0, 0:1, :]
    scale = mod_ref[0, 1:2, :]
    h = (xn * g_ref[...]) * (1.0 + scale) + shift
    hb = h.astype(BF16)

    def mm(cols):
        return _dot(hb, w_ref[:, cols[0]:cols[1]])

    qkv_ref[0] = mm(C_QKV)
    za_ref[0] = mm(C_ZA)
    ba_ref[0] = mm(C_BA)
    qb_ref[0] = mm(C_QB).astype(BF16)
    ckv_ref[0] = mm(C_CKV)
    zb_ref[0] = mm(C_ZB)
    iq_ref[0] = mm(C_IQ).astype(BF16)
    ikw_ref[0] = mm(C_IKW)


def _projection(x, mod3, g_pre, w_pad, tm):
    bsz, seq, d = x.shape
    widths = [(C_QKV, F32), (C_ZA, F32), (C_BA, F32), (C_QB, BF16),
              (C_CKV, F32), (C_ZB, F32), (C_IQ, BF16), (C_IKW, F32)]
    out_shape = [jax.ShapeDtypeStruct((bsz, seq, c[1] - c[0]), dt) for c, dt in widths]
    out_specs = [pl.BlockSpec((1, tm, c[1] - c[0]), lambda b, i: (b, i, 0)) for c, _ in widths]
    return pl.pallas_call(
        _proj_kernel,
        grid=(bsz, seq // tm),
        in_specs=[pl.BlockSpec((1, tm, d), lambda b, i: (b, i, 0)),
                  pl.BlockSpec((1, 3, d), lambda b, i: (b, 0, 0)),
                  pl.BlockSpec((1, d), lambda b, i: (0, 0)),
                  pl.BlockSpec((d, D_IN_PAD), lambda b, i: (0, 0))],
        out_specs=out_specs,
        out_shape=out_shape,
        compiler_params=pltpu.CompilerParams(dimension_semantics=("arbitrary", "arbitrary"),
                                             vmem_limit_bytes=VMEM_LIMIT),
        name="proj",
    )(x, mod3, g_pre.reshape(1, d), w_pad)


def _tri_inverse_many(lmats, level_masks, eye):
    ts = [eye - jnp.where(level_masks[0], lm, 0.0) for lm in lmats]
    for m in level_masks[1:]:
        xs = [_bdot(jnp.where(m, lm, 0.0), t) for lm, t in zip(lmats, ts)]
        ts = [t - _bdot(t, x) for t, x in zip(ts, xs)]
    return ts


def _gdn_kernel(qkv_ref, za_ref, ba_ref, cw_ref, alog_ref, dtb_ref, gn_ref, o_ref,
                ext_ref, s_ref, gc_ref, beta_ref, gct_ref, u_ref, wq_ref, kg_ref, at_ref,
                *, rb, tl, group_a):
    n_ch = tl // CHUNK
    li = pl.program_id(1)

    @pl.when(li == 0)
    def _():
        ext_ref[:, 0:8, :] = jnp.zeros((rb, 8, 3 * W_A), F32)
        s_ref[...] = jnp.zeros_like(s_ref)

    @pl.when(li > 0)
    def _():
        ext_ref[:, 0:8, :] = ext_ref[:, tl:tl + 8, :]

    ext_ref[:, 8:8 + tl, :] = qkv_ref[...]

    r_i = lax.broadcasted_iota(I32, (tl, tl), 0)
    c_i = lax.broadcasted_iota(I32, (tl, tl), 1)
    shift = CHUNK.bit_length() - 1
    tri = jnp.where((c_i <= r_i) & ((r_i >> shift) == (c_i >> shift)), 1.0, 0.0)
    for r in range(rb):
        ba = ba_ref[r]
        beta_ref[r] = _sigmoid(ba)
        g = -jnp.exp(alog_ref[...]) * _softplus(ba + dtb_ref[...])
        gc = _dot(tri, g, HIGHEST)
        gc_ref[r] = gc
        gct = gc.T
        for c in range(n_ch):
            gct_ref[r * n_ch + c] = gct[:, c * CHUNK:(c + 1) * CHUNK]

    row = lax.broadcasted_iota(I32, (CHUNK, CHUNK), 0)
    col = lax.broadcasted_iota(I32, (CHUNK, CHUNK), 1)
    causal = row >= col
    strict = row > col
    eye = jnp.where(row == col, 1.0, 0.0)
    level_masks = []
    s = 1
    while s < CHUNK:
        ls = s.bit_length() - 1
        level_masks.append(((row >> (ls + 1)) == (col >> (ls + 1)))
                           & (((row >> ls) & 1) == 1) & (((col >> ls) & 1) == 0))
        s *= 2

    def phase_a(ig, carry):
        probs = []
        for j in range(group_a):
            it = ig * group_a + j
            r = it // n_ch
            c = it - r * n_ch
            base = pl.multiple_of(c * CHUNK, CHUNK)
            gc_c = gc_ref[r, pl.ds(base, CHUNK), :]
            beta_c = beta_ref[r, pl.ds(base, CHUNK), :]
            gct_c = gct_ref[it]
            for h in range(H_A):
                def conv_silu(sec):
                    c0 = sec * W_A + h * DK_A
                    win = ext_ref[r, pl.ds(base, CHUNK + 8), c0:c0 + DK_A]
                    acc = None
                    for t in range(CONV_W):
                        sh = CONV_W - 1 - t
                        u = win if sh == 0 else pltpu.roll(win, sh, axis=0)
                        term = u[8:8 + CHUNK] * cw_ref[t:t + 1, c0:c0 + DK_A]
                        acc = term if acc is None else acc + term
                    return _silu(acc)

                q = conv_silu(0)
                k = conv_silu(1)
                v = conv_silu(2)
                q = q * lax.rsqrt(jnp.sum(q * q, axis=-1, keepdims=True) + EPS) * (DK_A ** -0.5)
                k = k * lax.rsqrt(jnp.sum(k * k, axis=-1, keepdims=True) + EPS)
                beta = beta_c[:, h:h + 1]
                gcol = gc_c[:, H_A + h:H_A + h + 1]
                grow = gct_c[H_A + h:H_A + h + 1, :]
                glast = gc_c[CHUNK - 1:CHUNK, H_A + h:H_A + h + 1]
                decay = jnp.exp(jnp.where(causal, gcol - grow, -jnp.inf))
                ecol = jnp.exp(gcol)
                kb = k * beta
                hc = slice(h * DV_A, (h + 1) * DV_A)
                wq_ref[it, h, CHUNK:2 * CHUNK, :] = (q * ecol).astype(BF16)
                kg_ref[r, pl.ds(base, CHUNK), hc] = (k * jnp.exp(glast - gcol)).astype(BF16)
                a2 = _bdot_nt(jnp.concatenate([kb, q], axis=0), k)
                at_ref[it, h] = (a2[CHUNK:] * decay).astype(BF16)
                probs.append(dict(
                    r=r, it=it, h=h, base=base, hc=hc,
                    lmat=jnp.where(strict, a2[:CHUNK] * decay, 0.0),
                    rhs=jnp.concatenate([v * beta, kb * ecol], axis=1).astype(BF16)))
        tmats = _tri_inverse_many([p["lmat"] for p in probs], level_masks, eye)
        for p, tmat in zip(probs, tmats):
            uw = _dot(tmat.astype(BF16), p["rhs"])
            u_ref[p["r"], pl.ds(p["base"], CHUNK), p["hc"]] = uw[:, :DV_A]
            wq_ref[p["it"], p["h"], 0:CHUNK, :] = uw[:, DV_A:].astype(BF16)
        return carry

    lax.fori_loop(0, rb * n_ch // group_a, phase_a, 0)

    def phase_b(c, carry):
        base = pl.multiple_of(c * CHUNK, CHUNK)
        chains = [(r, h) for r in range(rb) for h in range(H_A)]
        hcs = [slice(h * DV_A, (h + 1) * DV_A) for _, h in chains]
        sts = [s_ref[r, h] for r, h in chains]
        wss = [_dot(wq_ref[r * n_ch + c, h], st.astype(BF16)) for (r, h), st in zip(chains, sts)]
        vnbs = [(u_ref[r, pl.ds(base, CHUNK), hc] - ws[:CHUNK]).astype(BF16)
                for (r, h), hc, ws in zip(chains, hcs, wss)]
        upds = [_dot_tn(kg_ref[r, pl.ds(base, CHUNK), hc], vnb)
                for (r, h), hc, vnb in zip(chains, hcs, vnbs)]
        for (r, h), st, upd in zip(chains, sts, upds):
            glast = gc_ref[r, pl.ds(base + CHUNK - 1, 1), :][:, H_A + h:H_A + h + 1]
            s_ref[r, h] = st * jnp.exp(glast) + upd
        for (r, h), hc, ws, vnb in zip(chains, hcs, wss, vnbs):
            o = ws[CHUNK:] + _dot(at_ref[r * n_ch + c, h], vnb)
            on = o * lax.rsqrt(jnp.mean(o * o, axis=-1, keepdims=True) + EPS) * gn_ref[...]
            z = za_ref[r, pl.ds(base, CHUNK), hc]
            o_ref[r, pl.ds(base, CHUNK), hc] = on * _silu(z)
        return carry

    lax.fori_loop(0, n_ch, phase_b, 0)


def _gated_deltanet(qkv, za, ba, conv_w, alog_vec, dtb_vec, g_norm, rb, tl, group_a):
    bsz, seq, _ = qkv.shape
    n_ch = tl // CHUNK
    kern = functools.partial(_gdn_kernel, rb=rb, tl=tl, group_a=group_a)
    return pl.pallas_call(
        kern,
        grid=(bsz // rb, seq // tl),
        in_specs=[pl.BlockSpec((rb, tl, 3 * W_A), lambda b, i: (b, i, 0)),
                  pl.BlockSpec((rb, tl, W_A), lambda b, i: (b, i, 0)),
                  pl.BlockSpec((rb, tl, LANES), lambda b, i: (b, i, 0)),
                  pl.BlockSpec((CONV_W, 3 * W_A), lambda b, i: (0, 0)),
                  pl.BlockSpec((1, LANES), lambda b, i: (0, 0)),
                  pl.BlockSpec((1, LANES), lambda b, i: (0, 0)),
                  pl.BlockSpec((1, DV_A), lambda b, i: (0, 0))],
        out_specs=pl.BlockSpec((rb, tl, W_A), lambda b, i: (b, i, 0)),
        out_shape=jax.ShapeDtypeStruct((bsz, seq, W_A), F32),
        scratch_shapes=[pltpu.VMEM((rb, tl + 8, 3 * W_A), F32),
                        pltpu.VMEM((rb, H_A, DK_A, DV_A), F32),
                        pltpu.VMEM((rb, tl, LANES), F32),
                        pltpu.VMEM((rb, tl, LANES), F32),
                        pltpu.VMEM((rb * n_ch, LANES, CHUNK), F32),
                        pltpu.VMEM((rb, tl, W_A), F32),
                        pltpu.VMEM((rb * n_ch, H_A, 2 * CHUNK, DK_A), BF16),
                        pltpu.VMEM((rb, tl, W_A), BF16),
                        pltpu.VMEM((rb * n_ch, H_A, CHUNK, CHUNK), BF16)],
        compiler_params=pltpu.CompilerParams(dimension_semantics=("arbitrary", "arbitrary"),
                                             vmem_limit_bytes=VMEM_LIMIT),
        name="gdn",
    )(qkv, za, ba, conv_w, alog_vec, dtb_vec, g_norm.reshape(1, DV_A))


def _bias_kernel(rb_ref, o_ref):
    d = pl.program_id(0)
    qi = lax.broadcasted_iota(I32, (TQ, TQ), 0)
    kj = lax.broadcasted_iota(I32, (TQ, TQ), 1)
    dist = d * TQ + qi - kj
    n = jnp.maximum(dist, 0)
    nf = jnp.maximum(n, 1).astype(F32)
    large = MAX_EXACT + (jnp.log(nf / MAX_EXACT) / math.log(MAX_DIST / MAX_EXACT)
                         * (N_BUCKETS - MAX_EXACT)).astype(I32)
    large = jnp.minimum(large, N_BUCKETS - 1)
    bucket = jnp.where(n < MAX_EXACT, n, large)
    for h in range(H_B):
        acc = jnp.zeros((TQ, TQ), F32)
        for kb in range(N_BUCKETS):
            acc = jnp.where(bucket == kb, rb_ref[kb, h], acc)
        o_ref[0, h * TQ:(h + 1) * TQ, :] = acc


def _bias_tiles(rel_bias, n_diag):
    return pl.pallas_call(
        _bias_kernel,
        grid=(n_diag,),
        in_specs=[pl.BlockSpec(memory_space=pltpu.SMEM)],
        out_specs=pl.BlockSpec((1, H_B * TQ, TQ), lambda d: (d, 0, 0)),
        out_shape=jax.ShapeDtypeStruct((n_diag, H_B * TQ, TQ), F32),
        compiler_params=pltpu.CompilerParams(dimension_semantics=("arbitrary",)),
        name="bias",
    )(rel_bias)


def _dsa_kernel(iq_ref, ikwq_ref, qb_ref, zb_ref, ikw_ref, ckv_ref, gkv_ref, wuv_ref, bias_ref,
                o_ref,
                kvn_ref, iklo_ref, ikhi_ref, iwe_ref, iwo_ref, key_ref, am_ref, lg_ref,
                acc_ref, lsum_ref, *, seq, k_top):
    qi = pl.program_id(1)
    n_kc = qi + 1
    hq = H_B * TQ

    @pl.when(qi == 0)
    def _():
        rows = 256
        for r0 in range(0, seq, rows):
            ckv = ckv_ref[0, r0:r0 + rows, :]
            ms = jnp.mean(ckv * ckv, axis=-1, keepdims=True)
            kvn_ref[r0:r0 + rows, :] = ((ckv * lax.rsqrt(ms + EPS)) * gkv_ref[...]).astype(BF16)
            ikw = ikw_ref[0, r0:r0 + rows, :]
            lane = lax.broadcasted_iota(I32, ikw.shape, 1)
            lo = jnp.where(lane < D_IDX, ikw, 0.0)
            iklo_ref[r0:r0 + rows, :] = lo.astype(BF16)
            ikhi_ref[r0:r0 + rows, :] = pltpu.roll(lo, D_IDX, axis=1).astype(BF16)

    iq = iq_ref[0]
    iq4 = jnp.concatenate([iq[:, p * LANES:(p + 1) * LANES] for p in range(H_IDX // 2)], axis=0)
    iw = ikwq_ref[0] * (H_IDX ** -0.5 * D_IDX ** -0.5)
    for p in range(H_IDX // 2):
        le = IW_LANE + 2 * p
        iwe_ref[p * TQ:(p + 1) * TQ, :] = jnp.broadcast_to(iw[:, le:le + 1], (TQ, LANES))
        iwo_ref[p * TQ:(p + 1) * TQ, :] = jnp.broadcast_to(iw[:, le + 1:le + 2], (TQ, LANES))

    row_t = qi * TQ + lax.broadcasted_iota(I32, (TQ, LANES), 0)
    lane_i = lax.broadcasted_iota(I32, (TQ, LANES), 1)

    def causal_mask(c):
        return (c * TQ + lane_i) <= row_t

    def score_chunk(c, carry):
        kb = pl.multiple_of(c * TQ, TQ)
        re = _dot_nt(iq4, iklo_ref[pl.ds(kb, TQ), :])
        ro = _dot_nt(iq4, ikhi_ref[pl.ds(kb, TQ), :])
        t = jnp.maximum(re, 0.0) * iwe_ref[...] + jnp.maximum(ro, 0.0) * iwo_ref[...]
        s = (t[0:TQ] + t[TQ:2 * TQ]) + (t[2 * TQ:3 * TQ] + t[3 * TQ:4 * TQ])
        s = jnp.where(s == 0.0, 0.0, s)
        s = jnp.where(causal_mask(c), s, -jnp.inf)
        bits = pltpu.bitcast(s, I32)
        key_ref[c] = jnp.where(bits < 0, bits ^ 0x7FFFFFFF, bits)
        return carry

    lax.fori_loop(0, n_kc, score_chunk, 0)

    def count_ge(cand):
        def body(c, acc):
            return acc + jnp.where(key_ref[c] >= cand, 1.0, 0.0)
        acc = lax.fori_loop(0, n_kc, body, jnp.zeros((TQ, LANES), F32))
        return jnp.broadcast_to(jnp.sum(acc, axis=1, keepdims=True), (TQ, LANES))

    sign = jnp.int32(-2 ** 31)
    kf = float(k_top)

    def bit_body(i, tu):
        bit = lax.shift_left(jnp.int32(1), 31 - i)
        cand_u = tu | bit
        cnt = count_ge(cand_u ^ sign)
        return jnp.where(cnt >= kf, cand_u, tu)

    tu = lax.fori_loop(0, 32, bit_body, jnp.zeros((TQ, LANES), I32))
    thr = tu ^ sign

    def count_both(c, carry):
        cg, ce = carry
        key = key_ref[c]
        return (cg + jnp.where(key > thr, 1.0, 0.0), ce + jnp.where(key == thr, 1.0, 0.0))

    cg, ce = lax.fori_loop(0, n_kc, count_both,
                           (jnp.zeros((TQ, LANES), F32), jnp.zeros((TQ, LANES), F32)))
    cnt_gt = jnp.broadcast_to(jnp.sum(cg, axis=1, keepdims=True), (TQ, LANES))
    cnt_eq = jnp.broadcast_to(jnp.sum(ce, axis=1, keepdims=True), (TQ, LANES))
    need = kf - cnt_gt
    simple = (cnt_eq <= need) | ((thr == NEG_INF_KEY) & (row_t < k_top))
    all_simple = jnp.min(jnp.where(simple, 1.0, 0.0)) > 0.5

    @pl.when(all_simple)
    def _():
        def body(c, carry):
            sel = (key_ref[c] >= thr) & causal_mask(c)
            am_ref[c] = jnp.where(sel, 0.0, -jnp.inf)
            return carry
        lax.fori_loop(0, n_kc, body, 0)

    @pl.when(jnp.logical_not(all_simple))
    def _():
        ii = lax.broadcasted_iota(I32, (TQ, TQ), 0)
        jj = lax.broadcasted_iota(I32, (TQ, TQ), 1)
        upper = jnp.where(ii <= jj, 1.0, 0.0).astype(BF16)

        def body(c, seen):
            key = key_ref[c]
            eq = key == thr
            eqf = jnp.where(eq, 1.0, 0.0)
            rank = seen + _dot(eqf.astype(BF16), upper)
            sel = (key > thr) | (eq & (rank <= need))
            am_ref[c] = jnp.where(sel & causal_mask(c), 0.0, -jnp.inf)
            return seen + jnp.broadcast_to(jnp.sum(eqf, axis=1, keepdims=True), (TQ, LANES))
        lax.fori_loop(0, n_kc, body, jnp.zeros((TQ, LANES), F32))

    qb = qb_ref[0]
    q4 = jnp.concatenate([qb[:, h * R_KV:(h + 1) * R_KV] for h in range(H_B)], axis=0)
    scale = R_KV ** -0.5

    def pass1(c, m):
        kb = pl.multiple_of(c * TQ, TQ)
        kv = kvn_ref[pl.ds(kb, TQ), :]
        lg = _dot_nt(q4, kv) * scale + bias_ref[qi - c]
        am = am_ref[c]
        lg = lg + jnp.concatenate([am] * H_B, axis=0)
        lg_ref[c] = lg
        return jnp.maximum(m, lg)

    m = lax.fori_loop(0, n_kc, pass1, jnp.full((hq, LANES), -jnp.inf, F32))
    mrow = jnp.broadcast_to(jnp.max(m, axis=1, keepdims=True), (hq, LANES))
    acc_ref[...] = jnp.zeros_like(acc_ref)
    lsum_ref[...] = jnp.zeros_like(lsum_ref)

    def pass2(c, carry):
        kb = pl.multiple_of(c * TQ, TQ)
        p = jnp.exp(lg_ref[c] - mrow)
        lsum_ref[...] += p
        acc_ref[...] += _dot(p.astype(BF16), kvn_ref[pl.ds(kb, TQ), :])
        return carry

    lax.fori_loop(0, n_kc, pass2, 0)
    o4 = acc_ref[...] / jnp.sum(lsum_ref[...], axis=1, keepdims=True)
    for h in range(H_B):
        oh = o4[h * TQ:(h + 1) * TQ].astype(BF16)
        y = _dot(oh, wuv_ref[h])
        z = zb_ref[0, :, h * DV_B:(h + 1) * DV_B]
        o_ref[0, :, h * DV_B:(h + 1) * DV_B] = y * _silu(z)


def _dsa_attention(iq, ikw, qb, zb, ckv, g_kv, w_uv_bf16, bias_tiles, k_top):
    bsz, seq, _ = iq.shape
    n_q = seq // TQ
    hq = H_B * TQ
    kern = functools.partial(_dsa_kernel, seq=seq, k_top=k_top)
    return pl.pallas_call(
        kern,
        grid=(bsz, n_q),
        in_specs=[pl.BlockSpec((1, TQ, H_IDX * D_IDX), lambda b, i: (b, i, 0)),
                  pl.BlockSpec((1, TQ, LANES), lambda b, i: (b, i, 0)),
                  pl.BlockSpec((1, TQ, H_B * R_KV), lambda b, i: (b, i, 0)),
                  pl.BlockSpec((1, TQ, W_B), lambda b, i: (b, i, 0)),
                  pl.BlockSpec((1, seq, LANES), lambda b, i: (b, 0, 0)),
                  pl.BlockSpec((1, seq, R_KV), lambda b, i: (b, 0, 0)),
                  pl.BlockSpec((1, R_KV), lambda b, i: (0, 0)),
                  pl.BlockSpec((H_B, R_KV, DV_B), lambda b, i: (0, 0, 0)),
                  pl.BlockSpec((n_q, hq, TQ), lambda b, i: (0, 0, 0))],
        out_specs=pl.BlockSpec((1, TQ, W_B), lambda b, i: (b, i, 0)),
        out_shape=jax.ShapeDtypeStruct((bsz, seq, W_B), F32),
        scratch_shapes=[pltpu.VMEM((seq, R_KV), BF16),
                        pltpu.VMEM((seq, LANES), BF16),
                        pltpu.VMEM((seq, LANES), BF16),
                        pltpu.VMEM((hq, LANES), F32),
                        pltpu.VMEM((hq, LANES), F32),
                        pltpu.VMEM((n_q, TQ, TQ), I32),
                        pltpu.VMEM((n_q, TQ, TQ), F32),
                        pltpu.VMEM((n_q, hq, TQ), F32),
                        pltpu.VMEM((hq, R_KV), F32),
                        pltpu.VMEM((hq, LANES), F32)],
        compiler_params=pltpu.CompilerParams(dimension_semantics=("arbitrary", "arbitrary"),
                                             vmem_limit_bytes=VMEM_LIMIT),
        name="dsa",
    )(iq, ikw, qb, zb, ikw, ckv, g_kv.reshape(1, R_KV), w_uv_bf16, bias_tiles)


def _out_kernel(x_ref, oa_ref, ob_ref, mod_ref, g_ref, w_ref, o_ref):
    mix_in = jnp.concatenate([oa_ref[0], ob_ref[0]], axis=-1).astype(BF16)
    mix = _dot(mix_in, w_ref[...])
    ms = jnp.mean(mix * mix, axis=-1, keepdims=True)
    normed = (mix * lax.rsqrt(ms + EPS)) * g_ref[...]
    o_ref[0] = x_ref[0] + mod_ref[0, 2:3, :] * normed


def _output(x, o_a, o_b, mod3, g_post, w_out_bf16, tm):
    bsz, seq, d = x.shape
    return pl.pallas_call(
        _out_kernel,
        grid=(bsz, seq // tm),
        in_specs=[pl.BlockSpec((1, tm, d), lambda b, i: (b, i, 0)),
                  pl.BlockSpec((1, tm, W_A), lambda b, i: (b, i, 0)),
                  pl.BlockSpec((1, tm, W_B), lambda b, i: (b, i, 0)),
                  pl.BlockSpec((1, 3, d), lambda b, i: (b, 0, 0)),
                  pl.BlockSpec((1, d), lambda b, i: (0, 0)),
                  pl.BlockSpec((W_A + W_B, d), lambda b, i: (0, 0))],
        out_specs=pl.BlockSpec((1, tm, d), lambda b, i: (b, i, 0)),
        out_shape=jax.ShapeDtypeStruct((bsz, seq, d), F32),
        compiler_params=pltpu.CompilerParams(dimension_semantics=("arbitrary", "arbitrary"),
                                             vmem_limit_bytes=VMEM_LIMIT),
        name="outproj",
    )(x, o_a, o_b, mod3, g_post.reshape(1, d), w_out_bf16)


def _pad_lanes(w):
    return jnp.pad(w, ((0, 0), (0, LANES - w.shape[1])))


def _pad_input_projection(w_in):
    o_ba = 3 * W_A + W_A
    o_qb = o_ba + 2 * H_A
    o_ik = o_qb + H_B * R_KV + R_KV + W_B + H_IDX * D_IDX
    return jnp.concatenate([w_in[:, :o_ba], _pad_lanes(w_in[:, o_ba:o_qb]),
                            w_in[:, o_qb:o_ik], _pad_lanes(w_in[:, o_ik:])], axis=1)


def _head_lanes(v):
    return jnp.zeros((1, LANES), F32).at[0, H_A:2 * H_A].set(v.astype(F32))


def kernel(x, c, w_ada, b_ada, g_pre, w_in, conv_w, a_log, dt_bias, g_gdn, g_kv, w_uv, rel_bias, w_out, g_post):
    bsz, seq, d = x.shape
    depth = w_ada.shape[0]
    assert seq % TQ == 0 and seq % CHUNK == 0
    k_top = min(TOPK_MAX, seq // 4)
    tm = min(512, seq)
    tl = min(256, seq)
    rb = 4 if bsz % 4 == 0 else (2 if bsz % 2 == 0 else 1)
    bias_tiles = _bias_tiles(rel_bias, seq // TQ)
    for layer in range(depth):
        mod3 = _modulation(c, w_ada[layer], b_ada[layer]).reshape(bsz, 3, d)
        w_pad = _pad_input_projection(w_in[layer]).astype(BF16)
        qkv, za, ba, qb, ckv, zb, iq, ikw = _projection(x, mod3, g_pre[layer], w_pad, tm)
        o_a = _gated_deltanet(qkv, za, ba, conv_w[layer], _head_lanes(a_log[layer]),
                              _head_lanes(dt_bias[layer]), g_gdn[layer], rb, tl, 4)
        o_b = _dsa_attention(iq, ikw, qb, zb, ckv, g_kv[layer], w_uv[layer].astype(BF16),
                             bias_tiles, k_top)
        x = _output(x, o_a, o_b, mod3, g_post[layer], w_out[layer].astype(BF16), tm)
    return x
```

```python
import functools
import math

import jax
import jax.numpy as jnp
from jax import lax
from jax.experimental import pallas as pl
from jax.experimental.pallas import tpu as pltpu

F32 = jnp.float32
BF16 = jnp.bfloat16
I32 = jnp.int32
HIGHEST = lax.Precision.HIGHEST

H_A, DK_A, DV_A = 4, 128, 128
W_A = H_A * DV_A
CONV_W = 4
CHUNK = 64
H_B, R_KV, DV_B = 4, 128, 128
W_B = H_B * DV_B
H_IDX, D_IDX = 8, 64
TOPK_MAX = 256
N_BUCKETS, MAX_EXACT, MAX_DIST = 32, 16, 128
EPS = 1e-6

LANES = 128
TQ = 128
VMEM_LIMIT = 52 * 1024 * 1024

C_QKV = (0, 3 * W_A)
C_ZA = (C_QKV[1], C_QKV[1] + W_A)
C_BA = (C_ZA[1], C_ZA[1] + LANES)
C_QB = (C_BA[1], C_BA[1] + H_B * R_KV)
C_CKV = (C_QB[1], C_QB[1] + R_KV)
C_ZB = (C_CKV[1], C_CKV[1] + W_B)
C_IQ = (C_ZB[1], C_ZB[1] + H_IDX * D_IDX)
C_IKW = (C_IQ[1], C_IQ[1] + LANES)
D_IN_PAD = C_IKW[1]
IW_LANE = D_IDX

NEG_INF_KEY = -2139095041


def _sigmoid(x):
    return 1.0 / (1.0 + jnp.exp(-x))


def _silu(x):
    return x * _sigmoid(x)


def _softplus(x):
    return jnp.maximum(x, 0.0) + jnp.log1p(jnp.exp(-jnp.abs(x)))


def _dot(a, b, precision=None):
    return jnp.dot(a, b, precision=precision, preferred_element_type=F32)


def _dot_nt(a, b, precision=None):
    return lax.dot_general(a, b, (((1,), (1,)), ((), ())), precision=precision,
                           preferred_element_type=F32)


def _dot_tn(a, b, precision=None):
    return lax.dot_general(a, b, (((0,), (0,)), ((), ())), precision=precision,
                           preferred_element_type=F32)


def _bdot(a, b, precision=None):
    return _dot(a.astype(BF16), b.astype(BF16))


def _bdot_nt(a, b, precision=None):
    return _dot_nt(a.astype(BF16), b.astype(BF16))


def _bdot_tn(a, b, precision=None):
    return _dot_tn(a.astype(BF16), b.astype(BF16))


def _mod_kernel(c_ref, w_ref, b_ref, o_ref):
    c = c_ref[...]
    o_ref[...] = _dot(_silu(c), w_ref[...], HIGHEST) + b_ref[...]


def _modulation(c, w_ada, b_ada):
    bsz, d = c.shape
    n = w_ada.shape[1]
    tn = 512
    return pl.pallas_call(
        _mod_kernel,
        grid=(n // tn,),
        in_specs=[pl.BlockSpec((bsz, d), lambda j: (0, 0)),
                  pl.BlockSpec((d, tn), lambda j: (0, j)),
                  pl.BlockSpec((1, tn), lambda j: (0, j))],
        out_specs=pl.BlockSpec((bsz, tn), lambda j: (0, j)),
        out_shape=jax.ShapeDtypeStruct((bsz, n), F32),
        compiler_params=pltpu.CompilerParams(dimension_semantics=("arbitrary",),
                                             vmem_limit_bytes=VMEM_LIMIT),
        name="mod",
    )(c, w_ada, b_ada.reshape(1, n))


def _proj_kernel(x_ref, mod_ref, g_ref, w_ref,
                 qkv_ref, za_ref, ba_ref, qb_ref, ckv_ref, zb_ref, iq_ref, ikw_ref):
    x = x_ref[0]
    ms = jnp.mean(x * x, axis=-1, keepdims=True)
    xn = x * lax.rsqrt(ms + EPS)
    shift = mod_ref[0, 0:1, :]
    scale = mod_ref[0, 1:2, :]
    h = (xn * g_ref[...]) * (1.0 + scale) + shift
    hb = h.astype(BF16)

    def mm(cols):
        return _dot(hb, w_ref[:, cols[0]:cols[1]])

    qkv_ref[0] = mm(C_QKV)
    za_ref[0] = mm(C_ZA)
    ba_ref[0] = mm(C_BA)
    qb_ref[0] = mm(C_QB).astype(BF16)
    ckv_ref[0] = mm(C_CKV)
    zb_ref[0] = mm(C_ZB)
    iq_ref[0] = mm(C_IQ).astype(BF16)
    ikw_ref[0] = mm(C_IKW)


def _projection(x, mod3, g_pre, w_pad, tm):
    bsz, seq, d = x.shape
    widths = [(C_QKV, F32), (C_ZA, F32), (C_BA, F32), (C_QB, BF16),
              (C_CKV, F32), (C_ZB, F32), (C_IQ, BF16), (C_IKW, F32)]
    out_shape = [jax.ShapeDtypeStruct((bsz, seq, c[1] - c[0]), dt) for c, dt in widths]
    out_specs = [pl.BlockSpec((1, tm, c[1] - c[0]), lambda b, i: (b, i, 0)) for c, _ in widths]
    return pl.pallas_call(
        _proj_kernel,
        grid=(bsz, seq // tm),
        in_specs=[pl.BlockSpec((1, tm, d), lambda b, i: (b, i, 0)),
                  pl.BlockSpec((1, 3, d), lambda b, i: (b, 0, 0)),
                  pl.BlockSpec((1, d), lambda b, i: (0, 0)),
                  pl.BlockSpec((d, D_IN_PAD), lambda b, i: (0, 0))],
        out_specs=out_specs,
        out_shape=out_shape,
        compiler_params=pltpu.CompilerParams(dimension_semantics=("arbitrary", "arbitrary"),
                                             vmem_limit_bytes=VMEM_LIMIT),
        name="proj",
    )(x, mod3, g_pre.reshape(1, d), w_pad)


def _tri_inverse_many(lmats, level_masks, eye):
    ts = [eye - jnp.where(level_masks[0], lm, 0.0) for lm in lmats]
    for m in level_masks[1:]:
        xs = [_bdot(jnp.where(m, lm, 0.0), t) for lm, t in zip(lmats, ts)]
        ts = [t - _bdot(t, x) for t, x in zip(ts, xs)]
    return ts


def _gdn_kernel(qkv_ref, za_ref, ba_ref, cw_ref, alog_ref, dtb_ref, gn_ref, o_ref,
                ext_ref, s_ref, gc_ref, beta_ref, gct_ref, u_ref, wq_ref, kg_ref, at_ref,
                *, rb, tl, group_a):
    n_ch = tl // CHUNK
    li = pl.program_id(1)

    @pl.when(li == 0)
    def _():
        ext_ref[:, 0:8, :] = jnp.zeros((rb, 8, 3 * W_A), F32)
        s_ref[...] = jnp.zeros_like(s_ref)

    @pl.when(li > 0)
    def _():
        ext_ref[:, 0:8, :] = ext_ref[:, tl:tl + 8, :]

    ext_ref[:, 8:8 + tl, :] = qkv_ref[...]

    r_i = lax.broadcasted_iota(I32, (tl, tl), 0)
    c_i = lax.broadcasted_iota(I32, (tl, tl), 1)
    shift = CHUNK.bit_length() - 1
    tri = jnp.where((c_i <= r_i) & ((r_i >> shift) == (c_i >> shift)), 1.0, 0.0)
    for r in range(rb):
        ba = ba_ref[r]
        beta_ref[r] = _sigmoid(ba)
        g = -jnp.exp(alog_ref[...]) * _softplus(ba + dtb_ref[...])
        gc = _dot(tri, g, HIGHEST)
        gc_ref[r] = gc
        gct = gc.T
        for c in range(n_ch):
            gct_ref[r * n_ch + c] = gct[:, c * CHUNK:(c + 1) * CHUNK]

    row = lax.broadcasted_iota(I32, (CHUNK, CHUNK), 0)
    col = lax.broadcasted_iota(I32, (CHUNK, CHUNK), 1)
    causal = row >= col
    strict = row > col
    eye = jnp.where(row == col, 1.0, 0.0)
    level_masks = []
    s = 1
    while s < CHUNK:
        ls = s.bit_length() - 1
        level_masks.append(((row >> (ls + 1)) == (col >> (ls + 1)))
                           & (((row >> ls) & 1) == 1) & (((col >> ls) & 1) == 0))
        s *= 2

    def phase_a(ig, carry):
        probs = []
        for j in range(group_a):
            it = ig * group_a + j
            r = it // n_ch
            c = it - r * n_ch
            base = pl.multiple_of(c * CHUNK, CHUNK)
            gc_c = gc_ref[r, pl.ds(base, CHUNK), :]
            beta_c = beta_ref[r, pl.ds(base, CHUNK), :]
            gct_c = gct_ref[it]
            for h in range(H_A):
                def conv_silu(sec):
                    c0 = sec * W_A + h * DK_A
                    win = ext_ref[r, pl.ds(base, CHUNK + 8), c0:c0 + DK_A]
                    acc = None
                    for t in range(CONV_W):
                        sh = CONV_W - 1 - t
                        u = win if sh == 0 else pltpu.roll(win, sh, axis=0)
                        term = u[8:8 + CHUNK] * cw_ref[t:t + 1, c0:c0 + DK_A]
                        acc = term if acc is None else acc + term
                    return _silu(acc)

                q = conv_silu(0)
                k = conv_silu(1)
                v = conv_silu(2)
                q = q * lax.rsqrt(jnp.sum(q * q, axis=-1, keepdims=True) + EPS) * (DK_A ** -0.5)
                k = k * lax.rsqrt(jnp.sum(k * k, axis=-1, keepdims=True) + EPS)
                beta = beta_c[:, h:h + 1]
                gcol = gc_c[:, H_A + h:H_A + h + 1]
                grow = gct_c[H_A + h:H_A + h + 1, :]
                glast = gc_c[CHUNK - 1:CHUNK, H_A + h:H_A + h + 1]
                decay = jnp.exp(jnp.where(causal, gcol - grow, -jnp.inf))
                ecol = jnp.exp(gcol)
                kb = k * beta
                hc = slice(h * DV_A, (h + 1) * DV_A)
                wq_ref[it, h, CHUNK:2 * CHUNK, :] = (q * ecol).astype(BF16)
                kg_ref[r, pl.ds(base, CHUNK), hc] = (k * jnp.exp(glast - gcol)).astype(BF16)
                a2 = _bdot_nt(jnp.concatenate([kb, q], axis=0), k)
                at_ref[it, h] = (a2[CHUNK:] * decay).astype(BF16)
                probs.append(dict(
                    r=r, it=it, h=h, base=base, hc=hc,
                    lmat=jnp.where(strict, a2[:CHUNK] * decay, 0.0),
                    rhs=jnp.concatenate([v * beta, kb * ecol], axis=1).astype(BF16)))
        tmats = _tri_inverse_many([p["lmat"] for p in probs], level_masks, eye)
        for p, tmat in zip(probs, tmats):
            uw = _dot(tmat.astype(BF16), p["rhs"])
            u_ref[p["r"], pl.ds(p["base"], CHUNK), p["hc"]] = uw[:, :DV_A]
            wq_ref[p["it"], p["h"], 0:CHUNK, :] = uw[:, DV_A:].astype(BF16)
        return carry

    lax.fori_loop(0, rb * n_ch // group_a, phase_a, 0)

    def phase_b(c, carry):
        base = pl.multiple_of(c * CHUNK, CHUNK)
        chains = [(r, h) for r in range(rb) for h in range(H_A)]
        hcs = [slice(h * DV_A, (h + 1) * DV_A) for _, h in chains]
        sts = [s_ref[r, h] for r, h in chains]
        wss = [_dot(wq_ref[r * n_ch + c, h], st.astype(BF16)) for (r, h), st in zip(chains, sts)]
        vnbs = [(u_ref[r, pl.ds(base, CHUNK), hc] - ws[:CHUNK]).astype(BF16)
                for (r, h), hc, ws in zip(chains, hcs, wss)]
        upds = [_dot_tn(kg_ref[r, pl.ds(base, CHUNK), hc], vnb)
                for (r, h), hc, vnb in zip(chains, hcs, vnbs)]
        for (r, h), st, upd in zip(chains, sts, upds):
            glast = gc_ref[r, pl.ds(base + CHUNK - 1, 1), :][:, H_A + h:H_A + h + 1]
            s_ref[r, h] = st * jnp.exp(glast) + upd
        for (r, h), hc, ws, vnb in zip(chains, hcs, wss, vnbs):
            o = ws[CHUNK:] + _dot(at_ref[r * n_ch + c, h], vnb)
            on = o * lax.rsqrt(jnp.mean(o * o, axis=-1, keepdims=True) + EPS) * gn_ref[...]
            z = za_ref[r, pl.ds(base, CHUNK), hc]
            o_ref[r, pl.ds(base, CHUNK), hc] = on * _silu(z)
        return carry

    lax.fori_loop(0, n_ch, phase_b, 0)


def _gated_deltanet(qkv, za, ba, conv_w, alog_vec, dtb_vec, g_norm, rb, tl, group_a):
    bsz, seq, _ = qkv.shape
    n_ch = tl // CHUNK
    kern = functools.partial(_gdn_kernel, rb=rb, tl=tl, group_a=group_a)
    return pl.pallas_call(
        kern,
        grid=(bsz // rb, seq // tl),
        in_specs=[pl.BlockSpec((rb, tl, 3 * W_A), lambda b, i: (b, i, 0)),
                  pl.BlockSpec((rb, tl, W_A), lambda b, i: (b, i, 0)),
                  pl.BlockSpec((rb, tl, LANES), lambda b, i: (b, i, 0)),
                  pl.BlockSpec((CONV_W, 3 * W_A), lambda b, i: (0, 0)),
                  pl.BlockSpec((1, LANES), lambda b, i: (0, 0)),
                  pl.BlockSpec((1, LANES), lambda b, i: (0, 0)),
                  pl.BlockSpec((1, DV_A), lambda b, i: (0, 0))],
        out_specs=pl.BlockSpec((rb, tl, W_A), lambda b, i: (b, i, 0)),
        out_shape=jax.ShapeDtypeStruct((bsz, seq, W_A), F32),
        scratch_shapes=[pltpu.VMEM((rb, tl + 8, 3 * W_A), F32),
                        pltpu.VMEM((rb, H_A, DK_A, DV_A), F32),
                        pltpu.VMEM((rb, tl, LANES), F32),
                        pltpu.VMEM((rb, tl, LANES), F32),
                        pltpu.VMEM((rb * n_ch, LANES, CHUNK), F32),
                        pltpu.VMEM((rb, tl, W_A), F32),
                        pltpu.VMEM((rb * n_ch, H_A, 2 * CHUNK, DK_A), BF16),
                        pltpu.VMEM((rb, tl, W_A), BF16),
                        pltpu.VMEM((rb * n_ch, H_A, CHUNK, CHUNK), BF16)],
        compiler_params=pltpu.CompilerParams(dimension_semantics=("arbitrary", "arbitrary"),
                                             vmem_limit_bytes=VMEM_LIMIT),
        name="gdn",
    )(qkv, za, ba, conv_w, alog_vec, dtb_vec, g_norm.reshape(1, DV_A))


def _bias_kernel(rb_ref, o_ref):
    d = pl.program_id(0)
    kj = lax.broadcasted_iota(I32, (TQ, TQ), 0)
    qi = lax.broadcasted_iota(I32, (TQ, TQ), 1)
    dist = d * TQ + qi - kj
    n = jnp.maximum(dist, 0)
    nf = jnp.maximum(n, 1).astype(F32)
    large = MAX_EXACT + (jnp.log(nf / MAX_EXACT) / math.log(MAX_DIST / MAX_EXACT)
                         * (N_BUCKETS - MAX_EXACT)).astype(I32)
    large = jnp.minimum(large, N_BUCKETS - 1)
    bucket = jnp.where(n < MAX_EXACT, n, large)
    for h in range(H_B):
        acc = jnp.zeros((TQ, TQ), F32)
        for kb in range(N_BUCKETS):
            acc = jnp.where(bucket == kb, rb_ref[kb, h], acc)
        o_ref[0, :, h * TQ:(h + 1) * TQ] = acc


def _bias_tiles(rel_bias, n_diag):
    return pl.pallas_call(
        _bias_kernel,
        grid=(n_diag,),
        in_specs=[pl.BlockSpec(memory_space=pltpu.SMEM)],
        out_specs=pl.BlockSpec((1, TQ, H_B * TQ), lambda d: (d, 0, 0)),
        out_shape=jax.ShapeDtypeStruct((n_diag, TQ, H_B * TQ), F32),
        compiler_params=pltpu.CompilerParams(dimension_semantics=("arbitrary",)),
        name="bias",
    )(rel_bias)


def _fold_keys(x, op):
    x = op(x.reshape(4, TQ // 32, 8, x.shape[-1]), axis=1)
    return op(x, axis=0)


def _dsa_kernel(iq_ref, ikwq_ref, qb_ref, zb_ref, ikw_ref, ckv_ref, gkv_ref, wuv_ref, bias_ref,
                o_ref,
                kvn_ref, kvt_ref, iklo_ref, ikhi_ref, key_ref, am_ref, lg_ref, acc_ref,
                *, rb, seq, k_top):
    qi = pl.program_id(1)
    n_kc = qi + 1
    hq = H_B * TQ
    rows = range(rb)

    @pl.when(qi == 0)
    def _():
        for r in rows:
            for c in range(seq // TQ):
                sl = slice(c * TQ, (c + 1) * TQ)
                ckv = ckv_ref[r, sl, :]
                ms = jnp.mean(ckv * ckv, axis=-1, keepdims=True)
                kvn = (ckv * lax.rsqrt(ms + EPS)) * gkv_ref[...]
                kvn_ref[r, sl, :] = kvn.astype(BF16)
                kvt_ref[r, c] = kvn.T.astype(BF16)
                ikw = ikw_ref[r, sl, :]
                lane = lax.broadcasted_iota(I32, ikw.shape, 1)
                lo = jnp.where(lane < D_IDX, ikw, 0.0)
                iklo_ref[r, sl, :] = lo.astype(BF16)
                ikhi_ref[r, sl, :] = pltpu.roll(lo, D_IDX, axis=1).astype(BF16)

    iq4s, iwts = [], []
    for r in rows:
        iq = iq_ref[r]
        iq4s.append(jnp.concatenate(
            [iq[:, p * LANES:(p + 1) * LANES] for p in range(H_IDX // 2)], axis=0))
        iwts.append((ikwq_ref[r] * (H_IDX ** -0.5 * D_IDX ** -0.5)).T)

    key_j = lax.broadcasted_iota(I32, (TQ, TQ), 0)
    qry_t = qi * TQ + lax.broadcasted_iota(I32, (TQ, TQ), 1)

    def causal_mask(c):
        return (c * TQ + key_j) <= qry_t

    def score_chunk(c, carry):
        kb = pl.multiple_of(c * TQ, TQ)
        res = [_dot_nt(iklo_ref[r, pl.ds(kb, TQ), :], iq4s[r]) for r in rows]
        ros = [_dot_nt(ikhi_ref[r, pl.ds(kb, TQ), :], iq4s[r]) for r in rows]
        cm = causal_mask(c)
        for r in rows:
            s = None
            for p in range(H_IDX // 2):
                ps = slice(p * TQ, (p + 1) * TQ)
                we = iwts[r][IW_LANE + 2 * p:IW_LANE + 2 * p + 1, :]
                wo = iwts[r][IW_LANE + 2 * p + 1:IW_LANE + 2 * p + 2, :]
                t = jnp.maximum(res[r][:, ps], 0.0) * we + jnp.maximum(ros[r][:, ps], 0.0) * wo
                s = t if s is None else s + t
            s = jnp.where(s == 0.0, 0.0, s)
            s = jnp.where(cm, s, -jnp.inf)
            bits = pltpu.bitcast(s, I32)
            key_ref[r, c] = jnp.where(bits < 0, bits ^ 0x7FFFFFFF, bits)
        return carry

    lax.fori_loop(0, n_kc, score_chunk, 0)

    def allsum(x):
        return jnp.broadcast_to(jnp.sum(x, axis=0, keepdims=True), x.shape)

    def count_ge(cands):
        def body(c, accs):
            return tuple(acc + _fold_keys(jnp.where(key_ref[r, c] >= cand[0:1, :], 1.0, 0.0),
                                          jnp.sum)
                         for r, (acc, cand) in enumerate(zip(accs, cands)))
        accs = lax.fori_loop(0, n_kc, body, tuple(jnp.zeros((8, TQ), F32) for _ in rows))
        return [allsum(a) for a in accs]

    sign = jnp.int32(-2 ** 31)
    kf = float(k_top)

    def bit_body(i, tus):
        bit = lax.shift_left(jnp.int32(1), 31 - i)
        cands = [tu | bit for tu in tus]
        cnts = count_ge([cu ^ sign for cu in cands])
        return tuple(jnp.where(cnt >= kf, cu, tu) for cnt, cu, tu in zip(cnts, cands, tus))

    tus = lax.fori_loop(0, 32, bit_body, tuple(jnp.zeros((8, TQ), I32) for _ in rows))
    thrs = [(tu ^ sign)[0:1, :] for tu in tus]

    def count_both(c, carry):
        out = []
        for r in rows:
            cg, ce = carry[r]
            key = key_ref[r, c]
            out.append((cg + _fold_keys(jnp.where(key > thrs[r], 1.0, 0.0), jnp.sum),
                        ce + _fold_keys(jnp.where(key == thrs[r], 1.0, 0.0), jnp.sum)))
        return tuple(out)

    zero8 = jnp.zeros((8, TQ), F32)
    cges = lax.fori_loop(0, n_kc, count_both, tuple((zero8, zero8) for _ in rows))
    qrow = qi * TQ + lax.broadcasted_iota(I32, (1, TQ), 1)
    needs, simple_all = [], None
    for r in rows:
        cnt_gt = jnp.sum(cges[r][0], axis=0, keepdims=True)
        cnt_eq = jnp.sum(cges[r][1], axis=0, keepdims=True)
        need = kf - cnt_gt
        needs.append(need)
        simple = (cnt_eq <= need) | ((thrs[r] == NEG_INF_KEY) & (qrow < k_top))
        simple_all = simple if simple_all is None else (simple_all & simple)
    all_simple = jnp.min(jnp.where(simple_all, 1.0, 0.0)) > 0.5

    @pl.when(all_simple)
    def _():
        def body(c, carry):
            cm = causal_mask(c)
            for r in rows:
                sel = (key_ref[r, c] >= thrs[r]) & cm
                am_ref[r, c] = jnp.where(sel, 0.0, -jnp.inf)
            return carry
        lax.fori_loop(0, n_kc, body, 0)

    @pl.when(jnp.logical_not(all_simple))
    def _():
        lower = jnp.where(lax.broadcasted_iota(I32, (TQ, TQ), 1)
                          <= lax.broadcasted_iota(I32, (TQ, TQ), 0), 1.0, 0.0).astype(BF16)

        def body(c, seens):
            cm = causal_mask(c)
            out = []
            for r in rows:
                key = key_ref[r, c]
                eq = key == thrs[r]
                eqf = jnp.where(eq, 1.0, 0.0)
                rank = seens[r] + _dot(lower, eqf.astype(BF16))
                sel = (key > thrs[r]) | (eq & (rank <= needs[r]))
                am_ref[r, c] = jnp.where(sel & cm, 0.0, -jnp.inf)
                out.append(seens[r] + jnp.sum(eqf, axis=0, keepdims=True))
            return tuple(out)
        lax.fori_loop(0, n_kc, body, tuple(jnp.zeros((1, TQ), F32) for _ in rows))

    q4s = []
    for r in rows:
        qb = qb_ref[r]
        q4s.append(jnp.concatenate([qb[:, h * R_KV:(h + 1) * R_KV] for h in range(H_B)], axis=0))
    scale = R_KV ** -0.5

    def pass1(c, ms):
        kb = pl.multiple_of(c * TQ, TQ)
        sts = [_dot_nt(kvn_ref[r, pl.ds(kb, TQ), :], q4s[r]) for r in rows]
        bias = bias_ref[qi - c]
        out = []
        for r in rows:
            am = am_ref[r, c]
            lg = sts[r] * scale + bias + jnp.concatenate([am] * H_B, axis=1)
            lg_ref[r, c] = lg
            out.append(jnp.maximum(ms[r], _fold_keys(lg, jnp.max)))
        return tuple(out)

    ms = lax.fori_loop(0, n_kc, pass1, tuple(jnp.full((8, hq), -jnp.inf, F32) for _ in rows))
    mrows = [jnp.max(m, axis=0, keepdims=True) for m in ms]
    acc_ref[...] = jnp.zeros_like(acc_ref)

    def pass2(c, ls):
        ps = [jnp.exp(lg_ref[r, c] - mrows[r]) for r in rows]
        for r in rows:
            acc_ref[r] += _dot(kvt_ref[r, c], ps[r].astype(BF16))
        return tuple(l + _fold_keys(p, jnp.sum) for l, p in zip(ls, ps))

    ls = lax.fori_loop(0, n_kc, pass2, tuple(jnp.zeros((8, hq), F32) for _ in rows))
    for r in rows:
        ot = (acc_ref[r] / jnp.sum(ls[r], axis=0, keepdims=True)).astype(BF16)
        for h in range(H_B):
            hs = slice(h * DV_B, (h + 1) * DV_B)
            y = _dot_tn(ot[:, h * TQ:(h + 1) * TQ], wuv_ref[h])
            o_ref[r, :, hs] = y * _silu(zb_ref[r, :, hs])


def _dsa_attention(iq, ikw, qb, zb, ckv, g_kv, w_uv_bf16, bias_tiles, k_top, rb):
    bsz, seq, _ = iq.shape
    n_q = seq // TQ
    hq = H_B * TQ
    kern = functools.partial(_dsa_kernel, rb=rb, seq=seq, k_top=k_top)
    return pl.pallas_call(
        kern,
        grid=(bsz // rb, n_q),
        in_specs=[pl.BlockSpec((rb, TQ, H_IDX * D_IDX), lambda b, i: (b, i, 0)),
                  pl.BlockSpec((rb, TQ, LANES), lambda b, i: (b, i, 0)),
                  pl.BlockSpec((rb, TQ, H_B * R_KV), lambda b, i: (b, i, 0)),
                  pl.BlockSpec((rb, TQ, W_B), lambda b, i: (b, i, 0)),
                  pl.BlockSpec((rb, seq, LANES), lambda b, i: (b, 0, 0)),
                  pl.BlockSpec((rb, seq, R_KV), lambda b, i: (b, 0, 0)),
                  pl.BlockSpec((1, R_KV), lambda b, i: (0, 0)),
                  pl.BlockSpec((H_B, R_KV, DV_B), lambda b, i: (0, 0, 0)),
                  pl.BlockSpec((n_q, TQ, hq), lambda b, i: (0, 0, 0))],
        out_specs=pl.BlockSpec((rb, TQ, W_B), lambda b, i: (b, i, 0)),
        out_shape=jax.ShapeDtypeStruct((bsz, seq, W_B), F32),
        scratch_shapes=[pltpu.VMEM((rb, seq, R_KV), BF16),
                        pltpu.VMEM((rb, n_q, R_KV, TQ), BF16),
                        pltpu.VMEM((rb, seq, LANES), BF16),
                        pltpu.VMEM((rb, seq, LANES), BF16),
                        pltpu.VMEM((rb, n_q, TQ, TQ), I32),
                        pltpu.VMEM((rb, n_q, TQ, TQ), F32),
                        pltpu.VMEM((rb, n_q, TQ, hq), F32),
                        pltpu.VMEM((rb, R_KV, hq), F32)],
        compiler_params=pltpu.CompilerParams(dimension_semantics=("arbitrary", "arbitrary"),
                                             vmem_limit_bytes=VMEM_LIMIT),
        name="dsa",
    )(iq, ikw, qb, zb, ikw, ckv, g_kv.reshape(1, R_KV), w_uv_bf16, bias_tiles)


def _out_kernel(x_ref, oa_ref, ob_ref, mod_ref, g_ref, w_ref, o_ref):
    mix_in = jnp.concatenate([oa_ref[0], ob_ref[0]], axis=-1).astype(BF16)
    mix = _dot(mix_in, w_ref[...])
    ms = jnp.mean(mix * mix, axis=-1, keepdims=True)
    normed = (mix * lax.rsqrt(ms + EPS)) * g_ref[...]
    o_ref[0] = x_ref[0] + mod_ref[0, 2:3, :] * normed


def _output(x, o_a, o_b, mod3, g_post, w_out_bf16, tm):
    bsz, seq, d = x.shape
    return pl.pallas_call(
        _out_kernel,
        grid=(bsz, seq // tm),
        in_specs=[pl.BlockSpec((1, tm, d), lambda b, i: (b, i, 0)),
                  pl.BlockSpec((1, tm, W_A), lambda b, i: (b, i, 0)),
                  pl.BlockSpec((1, tm, W_B), lambda b, i: (b, i, 0)),
                  pl.BlockSpec((1, 3, d), lambda b, i: (b, 0, 0)),
                  pl.BlockSpec((1, d), lambda b, i: (0, 0)),
                  pl.BlockSpec((W_A + W_B, d), lambda b, i: (0, 0))],
        out_specs=pl.BlockSpec((1, tm, d), lambda b, i: (b, i, 0)),
        out_shape=jax.ShapeDtypeStruct((bsz, seq, d), F32),
        compiler_params=pltpu.CompilerParams(dimension_semantics=("arbitrary", "arbitrary"),
                                             vmem_limit_bytes=VMEM_LIMIT),
        name="outproj",
    )(x, o_a, o_b, mod3, g_post.reshape(1, d), w_out_bf16)


def _pad_lanes(w):
    return jnp.pad(w, ((0, 0), (0, LANES - w.shape[1])))


def _pad_input_projection(w_in):
    o_ba = 3 * W_A + W_A
    o_qb = o_ba + 2 * H_A
    o_ik = o_qb + H_B * R_KV + R_KV + W_B + H_IDX * D_IDX
    return jnp.concatenate([w_in[:, :o_ba], _pad_lanes(w_in[:, o_ba:o_qb]),
                            w_in[:, o_qb:o_ik], _pad_lanes(w_in[:, o_ik:])], axis=1)


def _head_lanes(v):
    return jnp.zeros((1, LANES), F32).at[0, H_A:2 * H_A].set(v.astype(F32))


def kernel(x, c, w_ada, b_ada, g_pre, w_in, conv_w, a_log, dt_bias, g_gdn, g_kv, w_uv, rel_bias, w_out, g_post):
    bsz, seq, d = x.shape
    depth = w_ada.shape[0]
    assert seq % TQ == 0 and seq % CHUNK == 0
    k_top = min(TOPK_MAX, seq // 4)
    tm = min(512, seq)
    tl = min(256, seq)
    rb = 4 if bsz % 4 == 0 else (2 if bsz % 2 == 0 else 1)
    bias_tiles = _bias_tiles(rel_bias, seq // TQ)
    for layer in range(depth):
        mod3 = _modulation(c, w_ada[layer], b_ada[layer]).reshape(bsz, 3, d)
        w_pad = _pad_input_projection(w_in[layer]).astype(BF16)
        qkv, za, ba, qb, ckv, zb, iq, ikw = _projection(x, mod3, g_pre[layer], w_pad, tm)
        o_a = _gated_deltanet(qkv, za, ba, conv_w[layer], _head_lanes(a_log[layer]),
                              _head_lanes(dt_bias[layer]), g_gdn[layer], rb, tl, 4)
        o_b = _dsa_attention(iq, ikw, qb, zb, ckv, g_kv[layer], w_uv[layer].astype(BF16),
                             bias_tiles, k_top, 2 if bsz % 2 == 0 else 1)
        x = _output(x, o_a, o_b, mod3, g_post[layer], w_out[layer].astype(BF16), tm)
    return x
```

```python
import functools
import math

import jax
import jax.numpy as jnp
from jax import lax
from jax.experimental import pallas as pl
from jax.experimental.pallas import tpu as pltpu

F32 = jnp.float32
BF16 = jnp.bfloat16
I32 = jnp.int32
HIGHEST = lax.Precision.HIGHEST

H_A, DK_A, DV_A = 4, 128, 128
W_A = H_A * DV_A
CONV_W = 4
CHUNK = 64
H_B, R_KV, DV_B = 4, 128, 128
W_B = H_B * DV_B
H_IDX, D_IDX = 8, 64
TOPK_MAX = 256
N_BUCKETS, MAX_EXACT, MAX_DIST = 32, 16, 128
EPS = 1e-6

LANES = 128
TQ = 128
VMEM_LIMIT = 52 * 1024 * 1024

C_QKV = (0, 3 * W_A)
C_ZA = (C_QKV[1], C_QKV[1] + W_A)
C_BA = (C_ZA[1], C_ZA[1] + LANES)
C_QB = (C_BA[1], C_BA[1] + H_B * R_KV)
C_CKV = (C_QB[1], C_QB[1] + R_KV)
C_ZB = (C_CKV[1], C_CKV[1] + W_B)
C_IQ = (C_ZB[1], C_ZB[1] + H_IDX * D_IDX)
C_IKW = (C_IQ[1], C_IQ[1] + LANES)
D_IN_PAD = C_IKW[1]
IW_LANE = D_IDX

NEG_INF_KEY = -2139095041


def _sigmoid(x):
    return 1.0 / (1.0 + jnp.exp(-x))


def _silu(x):
    return x * _sigmoid(x)


def _softplus(x):
    return jnp.maximum(x, 0.0) + jnp.log1p(jnp.exp(-jnp.abs(x)))


def _dot(a, b, precision=None):
    return jnp.dot(a, b, precision=precision, preferred_element_type=F32)


def _dot_nt(a, b, precision=None):
    return lax.dot_general(a, b, (((1,), (1,)), ((), ())), precision=precision,
                           preferred_element_type=F32)


def _dot_tn(a, b, precision=None):
    return lax.dot_general(a, b, (((0,), (0,)), ((), ())), precision=precision,
                           preferred_element_type=F32)


def _bdot(a, b, precision=None):
    return _dot(a.astype(BF16), b.astype(BF16))


def _bdot_nt(a, b, precision=None):
    return _dot_nt(a.astype(BF16), b.astype(BF16))


def _bdot_tn(a, b, precision=None):
    return _dot_tn(a.astype(BF16), b.astype(BF16))


def _mod_kernel(c_ref, w_ref, b_ref, o_ref):
    c = c_ref[...]
    o_ref[...] = _dot(_silu(c), w_ref[...], HIGHEST) + b_ref[...]


def _modulation(c, w_ada, b_ada):
    bsz, d = c.shape
    n = w_ada.shape[1]
    tn = 512
    return pl.pallas_call(
        _mod_kernel,
        grid=(n // tn,),
        in_specs=[pl.BlockSpec((bsz, d), lambda j: (0, 0)),
                  pl.BlockSpec((d, tn), lambda j: (0, j)),
                  pl.BlockSpec((1, tn), lambda j: (0, j))],
        out_specs=pl.BlockSpec((bsz, tn), lambda j: (0, j)),
        out_shape=jax.ShapeDtypeStruct((bsz, n), F32),
        compiler_params=pltpu.CompilerParams(dimension_semantics=("arbitrary",),
                                             vmem_limit_bytes=VMEM_LIMIT),
        name="mod",
    )(c, w_ada, b_ada.reshape(1, n))


def _proj_kernel(x_ref, mod_ref, g_ref, w_ref,
                 qkv_ref, za_ref, ba_ref, qb_ref, ckv_ref, zb_ref, iq_ref, ikw_ref):
    x = x_ref[0]
    ms = jnp.mean(x * x, axis=-1, keepdims=True)
    xn = x * lax.rsqrt(ms + EPS)
    shift = mod_ref[0, 0:1, :]
    scale = mod_ref[0, 1:2, :]
    h = (xn * g_ref[...]) * (1.0 + scale) + shift
    hb = h.astype(BF16)

    def mm(cols):
        return _dot(hb, w_ref[:, cols[0]:cols[1]])

    qkv_ref[0] = mm(C_QKV)
    za_ref[0] = mm(C_ZA)
    ba_ref[0] = mm(C_BA)
    qb_ref[0] = mm(C_QB).astype(BF16)
    ckv_ref[0] = mm(C_CKV)
    zb_ref[0] = mm(C_ZB)
    iq_ref[0] = mm(C_IQ).astype(BF16)
    ikw_ref[0] = mm(C_IKW)


def _projection(x, mod3, g_pre, w_pad, tm):
    bsz, seq, d = x.shape
    widths = [(C_QKV, F32), (C_ZA, F32), (C_BA, F32), (C_QB, BF16),
              (C_CKV, F32), (C_ZB, F32), (C_IQ, BF16), (C_IKW, F32)]
    out_shape = [jax.ShapeDtypeStruct((bsz, seq, c[1] - c[0]), dt) for c, dt in widths]
    out_specs = [pl.BlockSpec((1, tm, c[1] - c[0]), lambda b, i: (b, i, 0)) for c, _ in widths]
    return pl.pallas_call(
        _proj_kernel,
        grid=(bsz, seq // tm),
        in_specs=[pl.BlockSpec((1, tm, d), lambda b, i: (b, i, 0)),
                  pl.BlockSpec((1, 3, d), lambda b, i: (b, 0, 0)),
                  pl.BlockSpec((1, d), lambda b, i: (0, 0)),
                  pl.BlockSpec((d, D_IN_PAD), lambda b, i: (0, 0))],
        out_specs=out_specs,
        out_shape=out_shape,
        compiler_params=pltpu.CompilerParams(dimension_semantics=("arbitrary", "arbitrary"),
                                             vmem_limit_bytes=VMEM_LIMIT),
        name="proj",
    )(x, mod3, g_pre.reshape(1, d), w_pad)


def _tri_inverse_many(lmats, level_masks, eye):
    ts = [eye - jnp.where(level_masks[0], lm, 0.0) for lm in lmats]
    for m in level_masks[1:]:
        xs = [_bdot(jnp.where(m, lm, 0.0), t) for lm, t in zip(lmats, ts)]
        ts = [t - _bdot(t, x) for t, x in zip(ts, xs)]
    return ts


def _gdn_kernel(qkv_ref, za_ref, ba_ref, cw_ref, alog_ref, dtb_ref, gn_ref, o_ref,
                ext_ref, s_ref, gc_ref, beta_ref, gct_ref, u_ref, wq_ref, kg_ref, at_ref,
                *, rb, tl, group_a):
    n_ch = tl // CHUNK
    li = pl.program_id(1)

    @pl.when(li == 0)
    def _():
        ext_ref[:, 0:8, :] = jnp.zeros((rb, 8, 3 * W_A), F32)
        s_ref[...] = jnp.zeros_like(s_ref)

    @pl.when(li > 0)
    def _():
        ext_ref[:, 0:8, :] = ext_ref[:, tl:tl + 8, :]

    ext_ref[:, 8:8 + tl, :] = qkv_ref[...]

    r_i = lax.broadcasted_iota(I32, (tl, tl), 0)
    c_i = lax.broadcasted_iota(I32, (tl, tl), 1)
    shift = CHUNK.bit_length() - 1
    tri = jnp.where((c_i <= r_i) & ((r_i >> shift) == (c_i >> shift)), 1.0, 0.0)
    for r in range(rb):
        ba = ba_ref[r]
        beta_ref[r] = _sigmoid(ba)
        g = -jnp.exp(alog_ref[...]) * _softplus(ba + dtb_ref[...])
        gc = _dot(tri, g, HIGHEST)
        gc_ref[r] = gc
        gct = gc.T
        for c in range(n_ch):
            gct_ref[r * n_ch + c] = gct[:, c * CHUNK:(c + 1) * CHUNK]

    row = lax.broadcasted_iota(I32, (CHUNK, CHUNK), 0)
    col = lax.broadcasted_iota(I32, (CHUNK, CHUNK), 1)
    causal = row >= col
    strict = row > col
    eye = jnp.where(row == col, 1.0, 0.0)
    level_masks = []
    s = 1
    while s < CHUNK:
        ls = s.bit_length() - 1
        level_masks.append(((row >> (ls + 1)) == (col >> (ls + 1)))
                           & (((row >> ls) & 1) == 1) & (((col >> ls) & 1) == 0))
        s *= 2

    def phase_a(ig, carry):
        probs = []
        for j in range(group_a):
            it = ig * group_a + j
            r = it // n_ch
            c = it - r * n_ch
            base = pl.multiple_of(c * CHUNK, CHUNK)
            gc_c = gc_ref[r, pl.ds(base, CHUNK), :]
            beta_c = beta_ref[r, pl.ds(base, CHUNK), :]
            gct_c = gct_ref[it]
            for h in range(H_A):
                def conv_silu(sec):
                    c0 = sec * W_A + h * DK_A
                    win = ext_ref[r, pl.ds(base, CHUNK + 8), c0:c0 + DK_A]
                    acc = None
                    for t in range(CONV_W):
                        sh = CONV_W - 1 - t
                        u = win if sh == 0 else pltpu.roll(win, sh, axis=0)
                        term = u[8:8 + CHUNK] * cw_ref[t:t + 1, c0:c0 + DK_A]
                        acc = term if acc is None else acc + term
                    return _silu(acc)

                q = conv_silu(0)
                k = conv_silu(1)
                v = conv_silu(2)
                q = q * lax.rsqrt(jnp.sum(q * q, axis=-1, keepdims=True) + EPS) * (DK_A ** -0.5)
                k = k * lax.rsqrt(jnp.sum(k * k, axis=-1, keepdims=True) + EPS)
                beta = beta_c[:, h:h + 1]
                gcol = gc_c[:, H_A + h:H_A + h + 1]
                grow = gct_c[H_A + h:H_A + h + 1, :]
                glast = gc_c[CHUNK - 1:CHUNK, H_A + h:H_A + h + 1]
                decay = jnp.exp(jnp.where(causal, gcol - grow, -jnp.inf))
                ecol = jnp.exp(gcol)
                kb = k * beta
                hc = slice(h * DV_A, (h + 1) * DV_A)
                wq_ref[it, h, CHUNK:2 * CHUNK, :] = (q * ecol).astype(BF16)
                kg_ref[r, pl.ds(base, CHUNK), hc] = (k * jnp.exp(glast - gcol)).astype(BF16)
                a2 = _bdot_nt(jnp.concatenate([kb, q], axis=0), k)
                at_ref[it, h] = (a2[CHUNK:] * decay).astype(BF16)
                probs.append(dict(
                    r=r, it=it, h=h, base=base, hc=hc,
                    lmat=jnp.where(strict, a2[:CHUNK] * decay, 0.0),
                    rhs=jnp.concatenate([v * beta, kb * ecol], axis=1).astype(BF16)))
        tmats = _tri_inverse_many([p["lmat"] for p in probs], level_masks, eye)
        for p, tmat in zip(probs, tmats):
            uw = _dot(tmat.astype(BF16), p["rhs"])
            u_ref[p["r"], pl.ds(p["base"], CHUNK), p["hc"]] = uw[:, :DV_A]
            wq_ref[p["it"], p["h"], 0:CHUNK, :] = uw[:, DV_A:].astype(BF16)
        return carry

    lax.fori_loop(0, rb * n_ch // group_a, phase_a, 0)

    def phase_b(c, carry):
        base = pl.multiple_of(c * CHUNK, CHUNK)
        chains = [(r, h) for r in range(rb) for h in range(H_A)]
        hcs = [slice(h * DV_A, (h + 1) * DV_A) for _, h in chains]
        sts = [s_ref[r, h] for r, h in chains]
        wss = [_dot(wq_ref[r * n_ch + c, h], st.astype(BF16)) for (r, h), st in zip(chains, sts)]
        vnbs = [(u_ref[r, pl.ds(base, CHUNK), hc] - ws[:CHUNK]).astype(BF16)
                for (r, h), hc, ws in zip(chains, hcs, wss)]
        upds = [_dot_tn(kg_ref[r, pl.ds(base, CHUNK), hc], vnb)
                for (r, h), hc, vnb in zip(chains, hcs, vnbs)]
        for (r, h), st, upd in zip(chains, sts, upds):
            glast = gc_ref[r, pl.ds(base + CHUNK - 1, 1), :][:, H_A + h:H_A + h + 1]
            s_ref[r, h] = st * jnp.exp(glast) + upd
        for (r, h), hc, ws, vnb in zip(chains, hcs, wss, vnbs):
            o = ws[CHUNK:] + _dot(at_ref[r * n_ch + c, h], vnb)
            on = o * lax.rsqrt(jnp.mean(o * o, axis=-1, keepdims=True) + EPS) * gn_ref[...]
            z = za_ref[r, pl.ds(base, CHUNK), hc]
            o_ref[r, pl.ds(base, CHUNK), hc] = on * _silu(z)
        return carry

    lax.fori_loop(0, n_ch, phase_b, 0)


def _gated_deltanet(qkv, za, ba, conv_w, alog_vec, dtb_vec, g_norm, rb, tl, group_a):
    bsz, seq, _ = qkv.shape
    n_ch = tl // CHUNK
    kern = functools.partial(_gdn_kernel, rb=rb, tl=tl, group_a=group_a)
    return pl.pallas_call(
        kern,
        grid=(bsz // rb, seq // tl),
        in_specs=[pl.BlockSpec((rb, tl, 3 * W_A), lambda b, i: (b, i, 0)),
                  pl.BlockSpec((rb, tl, W_A), lambda b, i: (b, i, 0)),
                  pl.BlockSpec((rb, tl, LANES), lambda b, i: (b, i, 0)),
                  pl.BlockSpec((CONV_W, 3 * W_A), lambda b, i: (0, 0)),
                  pl.BlockSpec((1, LANES), lambda b, i: (0, 0)),
                  pl.BlockSpec((1, LANES), lambda b, i: (0, 0)),
                  pl.BlockSpec((1, DV_A), lambda b, i: (0, 0))],
        out_specs=pl.BlockSpec((rb, tl, W_A), lambda b, i: (b, i, 0)),
        out_shape=jax.ShapeDtypeStruct((bsz, seq, W_A), F32),
        scratch_shapes=[pltpu.VMEM((rb, tl + 8, 3 * W_A), F32),
                        pltpu.VMEM((rb, H_A, DK_A, DV_A), F32),
                        pltpu.VMEM((rb, tl, LANES), F32),
                        pltpu.VMEM((rb, tl, LANES), F32),
                        pltpu.VMEM((rb * n_ch, LANES, CHUNK), F32),
                        pltpu.VMEM((rb, tl, W_A), F32),
                        pltpu.VMEM((rb * n_ch, H_A, 2 * CHUNK, DK_A), BF16),
                        pltpu.VMEM((rb, tl, W_A), BF16),
                        pltpu.VMEM((rb * n_ch, H_A, CHUNK, CHUNK), BF16)],
        compiler_params=pltpu.CompilerParams(dimension_semantics=("arbitrary", "arbitrary"),
                                             vmem_limit_bytes=VMEM_LIMIT),
        name="gdn",
    )(qkv, za, ba, conv_w, alog_vec, dtb_vec, g_norm.reshape(1, DV_A))


def _bias_kernel(rb_ref, o_ref):
    d = pl.program_id(0)
    kj = lax.broadcasted_iota(I32, (TQ, TQ), 0)
    qi = lax.broadcasted_iota(I32, (TQ, TQ), 1)
    dist = d * TQ + qi - kj
    n = jnp.maximum(dist, 0)
    nf = jnp.maximum(n, 1).astype(F32)
    large = MAX_EXACT + (jnp.log(nf / MAX_EXACT) / math.log(MAX_DIST / MAX_EXACT)
                         * (N_BUCKETS - MAX_EXACT)).astype(I32)
    large = jnp.minimum(large, N_BUCKETS - 1)
    bucket = jnp.where(n < MAX_EXACT, n, large)
    for h in range(H_B):
        acc = jnp.zeros((TQ, TQ), F32)
        for kb in range(N_BUCKETS):
            acc = jnp.where(bucket == kb, rb_ref[kb, h], acc)
        o_ref[0, :, h * TQ:(h + 1) * TQ] = acc


def _bias_tiles(rel_bias, n_diag):
    return pl.pallas_call(
        _bias_kernel,
        grid=(n_diag,),
        in_specs=[pl.BlockSpec(memory_space=pltpu.SMEM)],
        out_specs=pl.BlockSpec((1, TQ, H_B * TQ), lambda d: (d, 0, 0)),
        out_shape=jax.ShapeDtypeStruct((n_diag, TQ, H_B * TQ), F32),
        compiler_params=pltpu.CompilerParams(dimension_semantics=("arbitrary",)),
        name="bias",
    )(rel_bias)


def _fold_keys(x, op):
    x = op(x.reshape(4, TQ // 32, 8, x.shape[-1]), axis=1)
    return op(x, axis=0)


def _dsa_kernel(iq_ref, ikwq_ref, qb_ref, zb_ref, ikw_ref, ckv_ref, gkv_ref, wuv_ref, bias_ref,
                o_ref,
                kvn_ref, kvt_ref, iklo_ref, ikhi_ref, key_ref, am_ref, lg_ref, acc_ref,
                *, rb, seq, k_top):
    qi = pl.program_id(1)
    n_kc = qi + 1
    hq = H_B * TQ
    rows = range(rb)

    @pl.when(qi == 0)
    def _():
        for r in rows:
            for c in range(seq // TQ):
                sl = slice(c * TQ, (c + 1) * TQ)
                ckv = ckv_ref[r, sl, :]
                ms = jnp.mean(ckv * ckv, axis=-1, keepdims=True)
                kvn = (ckv * lax.rsqrt(ms + EPS)) * gkv_ref[...]
                kvn_ref[r, sl, :] = kvn.astype(BF16)
                kvt_ref[r, c] = kvn.T.astype(BF16)
                ikw = ikw_ref[r, sl, :]
                lane = lax.broadcasted_iota(I32, ikw.shape, 1)
                lo = jnp.where(lane < D_IDX, ikw, 0.0)
                iklo_ref[r, sl, :] = lo.astype(BF16)
                ikhi_ref[r, sl, :] = pltpu.roll(lo, D_IDX, axis=1).astype(BF16)

    iq4s, iwts = [], []
    for r in rows:
        iq = iq_ref[r]
        iq4s.append(jnp.concatenate(
            [iq[:, p * LANES:(p + 1) * LANES] for p in range(H_IDX // 2)], axis=0))
        iwts.append((ikwq_ref[r] * (H_IDX ** -0.5 * D_IDX ** -0.5)).T)

    key_j = lax.broadcasted_iota(I32, (TQ, TQ), 0)
    qry_t = qi * TQ + lax.broadcasted_iota(I32, (TQ, TQ), 1)

    def causal_mask(c):
        return (c * TQ + key_j) <= qry_t

    q4s = []
    for r in rows:
        qb = qb_ref[r]
        q4s.append(jnp.concatenate([qb[:, h * R_KV:(h + 1) * R_KV] for h in range(H_B)], axis=0))
    scale = R_KV ** -0.5

    def pair_loop(body, carry):
        carry = lax.fori_loop(0, n_kc >> 1, lambda i, cr: body((2 * i, 2 * i + 1), cr), carry)
        return lax.cond((n_kc & 1) == 1, lambda cr: body((n_kc - 1,), cr), lambda cr: cr, carry)

    def score_chunks(cs, carry):
        items = [(r, c, pl.ds(pl.multiple_of(c * TQ, TQ), TQ)) for c in cs for r in rows]
        res = [_dot_nt(iklo_ref[r, ks, :], iq4s[r]) for r, _, ks in items]
        ros = [_dot_nt(ikhi_ref[r, ks, :], iq4s[r]) for r, _, ks in items]
        sts = [_dot_nt(kvn_ref[r, ks, :], q4s[r]) for r, _, ks in items]
        for (r, c, _), re, ro, st in zip(items, res, ros, sts):
            s = None
            for p in range(H_IDX // 2):
                ps = slice(p * TQ, (p + 1) * TQ)
                we = iwts[r][IW_LANE + 2 * p:IW_LANE + 2 * p + 1, :]
                wo = iwts[r][IW_LANE + 2 * p + 1:IW_LANE + 2 * p + 2, :]
                t = jnp.maximum(re[:, ps], 0.0) * we + jnp.maximum(ro[:, ps], 0.0) * wo
                s = t if s is None else s + t
            s = jnp.where(s == 0.0, 0.0, s)
            s = jnp.where(causal_mask(c), s, -jnp.inf)
            bits = pltpu.bitcast(s, I32)
            key_ref[r, c] = jnp.where(bits < 0, bits ^ 0x7FFFFFFF, bits)
            lg_ref[r, c] = st * scale + bias_ref[qi - c]
        return carry

    pair_loop(score_chunks, 0)

    def allsum(x):
        return jnp.broadcast_to(jnp.sum(x, axis=0, keepdims=True), x.shape)

    def count_ge(cands):
        def body(cs, accs):
            out = list(accs)
            for c in cs:
                for r in rows:
                    hit = jnp.where(key_ref[r, c] >= cands[r][0:1, :], 1.0, 0.0)
                    out[r] = out[r] + _fold_keys(hit, jnp.sum)
            return tuple(out)
        accs = pair_loop(body, tuple(jnp.zeros((8, TQ), F32) for _ in rows))
        return [allsum(a) for a in accs]

    sign = jnp.int32(-2 ** 31)
    kf = float(k_top)

    def bit_body(i, tus):
        bit = lax.shift_left(jnp.int32(1), 31 - i)
        cands = [tu | bit for tu in tus]
        cnts = count_ge([cu ^ sign for cu in cands])
        return tuple(jnp.where(cnt >= kf, cu, tu) for cnt, cu, tu in zip(cnts, cands, tus))

    tus = lax.fori_loop(0, 32, bit_body, tuple(jnp.zeros((8, TQ), I32) for _ in rows))
    thrs = [(tu ^ sign)[0:1, :] for tu in tus]

    def count_both(c, carry):
        out = []
        for r in rows:
            cg, ce = carry[r]
            key = key_ref[r, c]
            out.append((cg + _fold_keys(jnp.where(key > thrs[r], 1.0, 0.0), jnp.sum),
                        ce + _fold_keys(jnp.where(key == thrs[r], 1.0, 0.0), jnp.sum)))
        return tuple(out)

    zero8 = jnp.zeros((8, TQ), F32)
    cges = lax.fori_loop(0, n_kc, count_both, tuple((zero8, zero8) for _ in rows))
    qrow = qi * TQ + lax.broadcasted_iota(I32, (1, TQ), 1)
    needs, simple_all = [], None
    for r in rows:
        cnt_gt = jnp.sum(cges[r][0], axis=0, keepdims=True)
        cnt_eq = jnp.sum(cges[r][1], axis=0, keepdims=True)
        need = kf - cnt_gt
        needs.append(need)
        simple = (cnt_eq <= need) | ((thrs[r] == NEG_INF_KEY) & (qrow < k_top))
        simple_all = simple if simple_all is None else (simple_all & simple)
    all_simple = jnp.min(jnp.where(simple_all, 1.0, 0.0)) > 0.5

    @pl.when(all_simple)
    def _():
        def body(c, carry):
            cm = causal_mask(c)
            for r in rows:
                sel = (key_ref[r, c] >= thrs[r]) & cm
                am_ref[r, c] = jnp.where(sel, 0.0, -jnp.inf)
            return carry
        lax.fori_loop(0, n_kc, body, 0)

    @pl.when(jnp.logical_not(all_simple))
    def _():
        lower = jnp.where(lax.broadcasted_iota(I32, (TQ, TQ), 1)
                          <= lax.broadcasted_iota(I32, (TQ, TQ), 0), 1.0, 0.0).astype(BF16)

        def body(c, seens):
            cm = causal_mask(c)
            out = []
            for r in rows:
                key = key_ref[r, c]
                eq = key == thrs[r]
                eqf = jnp.where(eq, 1.0, 0.0)
                rank = seens[r] + _dot(lower, eqf.astype(BF16))
                sel = (key > thrs[r]) | (eq & (rank <= needs[r]))
                am_ref[r, c] = jnp.where(sel & cm, 0.0, -jnp.inf)
                out.append(seens[r] + jnp.sum(eqf, axis=0, keepdims=True))
            return tuple(out)
        lax.fori_loop(0, n_kc, body, tuple(jnp.zeros((1, TQ), F32) for _ in rows))

    def pass1(cs, ms):
        out = list(ms)
        for c in cs:
            for r in rows:
                lg = lg_ref[r, c] + jnp.concatenate([am_ref[r, c]] * H_B, axis=1)
                lg_ref[r, c] = lg
                out[r] = jnp.maximum(out[r], _fold_keys(lg, jnp.max))
        return tuple(out)

    ms = pair_loop(pass1, tuple(jnp.full((8, hq), -jnp.inf, F32) for _ in rows))
    mrows = [jnp.max(m, axis=0, keepdims=True) for m in ms]
    acc_ref[...] = jnp.zeros_like(acc_ref)

    def pass2(cs, ls):
        out = list(ls)
        for r in rows:
            ps = [jnp.exp(lg_ref[r, c] - mrows[r]) for c in cs]
            kvt = jnp.concatenate([kvt_ref[r, c] for c in cs], axis=1)
            acc_ref[r] += _dot(kvt, jnp.concatenate(ps, axis=0).astype(BF16))
            for p in ps:
                out[r] = out[r] + _fold_keys(p, jnp.sum)
        return tuple(out)

    ls = pair_loop(pass2, tuple(jnp.zeros((8, hq), F32) for _ in rows))
    for r in rows:
        ot = (acc_ref[r] / jnp.sum(ls[r], axis=0, keepdims=True)).astype(BF16)
        for h in range(H_B):
            hs = slice(h * DV_B, (h + 1) * DV_B)
            y = _dot_tn(ot[:, h * TQ:(h + 1) * TQ], wuv_ref[h])
            o_ref[r, :, hs] = y * _silu(zb_ref[r, :, hs])


def _dsa_attention(iq, ikw, qb, zb, ckv, g_kv, w_uv_bf16, bias_tiles, k_top, rb):
    bsz, seq, _ = iq.shape
    n_q = seq // TQ
    hq = H_B * TQ
    kern = functools.partial(_dsa_kernel, rb=rb, seq=seq, k_top=k_top)
    return pl.pallas_call(
        kern,
        grid=(bsz // rb, n_q),
        in_specs=[pl.BlockSpec((rb, TQ, H_IDX * D_IDX), lambda b, i: (b, i, 0)),
                  pl.BlockSpec((rb, TQ, LANES), lambda b, i: (b, i, 0)),
                  pl.BlockSpec((rb, TQ, H_B * R_KV), lambda b, i: (b, i, 0)),
                  pl.BlockSpec((rb, TQ, W_B), lambda b, i: (b, i, 0)),
                  pl.BlockSpec((rb, seq, LANES), lambda b, i: (b, 0, 0)),
                  pl.BlockSpec((rb, seq, R_KV), lambda b, i: (b, 0, 0)),
                  pl.BlockSpec((1, R_KV), lambda b, i: (0, 0)),
                  pl.BlockSpec((H_B, R_KV, DV_B), lambda b, i: (0, 0, 0)),
                  pl.BlockSpec((n_q, TQ, hq), lambda b, i: (0, 0, 0))],
        out_specs=pl.BlockSpec((rb, TQ, W_B), lambda b, i: (b, i, 0)),
        out_shape=jax.ShapeDtypeStruct((bsz, seq, W_B), F32),
        scratch_shapes=[pltpu.VMEM((rb, seq, R_KV), BF16),
                        pltpu.VMEM((rb, n_q, R_KV, TQ), BF16),
                        pltpu.VMEM((rb, seq, LANES), BF16),
                        pltpu.VMEM((rb, seq, LANES), BF16),
                        pltpu.VMEM((rb, n_q, TQ, TQ), I32),
                        pltpu.VMEM((rb, n_q, TQ, TQ), F32),
                        pltpu.VMEM((rb, n_q, TQ, hq), F32),
                        pltpu.VMEM((rb, R_KV, hq), F32)],
        compiler_params=pltpu.CompilerParams(dimension_semantics=("arbitrary", "arbitrary"),
                                             vmem_limit_bytes=VMEM_LIMIT),
        name="dsa",
    )(iq, ikw, qb, zb, ikw, ckv, g_kv.reshape(1, R_KV), w_uv_bf16, bias_tiles)


def _out_kernel(x_ref, oa_ref, ob_ref, mod_ref, g_ref, w_ref, o_ref):
    mix_in = jnp.concatenate([oa_ref[0], ob_ref[0]], axis=-1).astype(BF16)
    mix = _dot(mix_in, w_ref[...])
    ms = jnp.mean(mix * mix, axis=-1, keepdims=True)
    normed = (mix * lax.rsqrt(ms + EPS)) * g_ref[...]
    o_ref[0] = x_ref[0] + mod_ref[0, 2:3, :] * normed


def _output(x, o_a, o_b, mod3, g_post, w_out_bf16, tm):
    bsz, seq, d = x.shape
    return pl.pallas_call(
        _out_kernel,
        grid=(bsz, seq // tm),
        in_specs=[pl.BlockSpec((1, tm, d), lambda b, i: (b, i, 0)),
                  pl.BlockSpec((1, tm, W_A), lambda b, i: (b, i, 0)),
                  pl.BlockSpec((1, tm, W_B), lambda b, i: (b, i, 0)),
                  pl.BlockSpec((1, 3, d), lambda b, i: (b, 0, 0)),
                  pl.BlockSpec((1, d), lambda b, i: (0, 0)),
                  pl.BlockSpec((W_A + W_B, d), lambda b, i: (0, 0))],
        out_specs=pl.BlockSpec((1, tm, d), lambda b, i: (b, i, 0)),
        out_shape=jax.ShapeDtypeStruct((bsz, seq, d), F32),
        compiler_params=pltpu.CompilerParams(dimension_semantics=("arbitrary", "arbitrary"),
                                             vmem_limit_bytes=VMEM_LIMIT),
        name="outproj",
    )(x, o_a, o_b, mod3, g_post.reshape(1, d), w_out_bf16)


def _pad_lanes(w):
    return jnp.pad(w, ((0, 0), (0, LANES - w.shape[1])))


def _pad_input_projection(w_in):
    o_ba = 3 * W_A + W_A
    o_qb = o_ba + 2 * H_A
    o_ik = o_qb + H_B * R_KV + R_KV + W_B + H_IDX * D_IDX
    return jnp.concatenate([w_in[:, :o_ba], _pad_lanes(w_in[:, o_ba:o_qb]),
                            w_in[:, o_qb:o_ik], _pad_lanes(w_in[:, o_ik:])], axis=1)


def _head_lanes(v):
    return jnp.zeros((1, LANES), F32).at[0, H_A:2 * H_A].set(v.astype(F32))


def kernel(x, c, w_ada, b_ada, g_pre, w_in, conv_w, a_log, dt_bias, g_gdn, g_kv, w_uv, rel_bias, w_out, g_post):
    bsz, seq, d = x.shape
    depth = w_ada.shape[0]
    assert seq % TQ == 0 and seq % CHUNK == 0
    k_top = min(TOPK_MAX, seq // 4)
    tm = min(512, seq)
    tl = min(256, seq)
    rb = 4 if bsz % 4 == 0 else (2 if bsz % 2 == 0 else 1)
    bias_tiles = _bias_tiles(rel_bias, seq // TQ)
    for layer in range(depth):
        mod3 = _modulation(c, w_ada[layer], b_ada[layer]).reshape(bsz, 3, d)
        w_pad = _pad_input_projection(w_in[layer]).astype(BF16)
        qkv, za, ba, qb, ckv, zb, iq, ikw = _projection(x, mod3, g_pre[layer], w_pad, tm)
        o_a = _gated_deltanet(qkv, za, ba, conv_w[layer], _head_lanes(a_log[layer]),
                              _head_lanes(dt_bias[layer]), g_gdn[layer], rb, tl, 4)
        o_b = _dsa_attention(iq, ikw, qb, zb, ckv, g_kv[layer], w_uv[layer].astype(BF16),
                             bias_tiles, k_top, 2 if bsz % 2 == 0 else 1)
        x = _output(x, o_a, o_b, mod3, g_post[layer], w_out[layer].astype(BF16), tm)
    return x
```

```python
import functools
import math

import jax
import jax.numpy as jnp
from jax import lax
from jax.experimental import pallas as pl
from jax.experimental.pallas import tpu as pltpu

F32 = jnp.float32
BF16 = jnp.bfloat16
I32 = jnp.int32
I16 = jnp.int16
I16_MIN = -32768
HIGHEST = lax.Precision.HIGHEST

H_A, DK_A, DV_A = 4, 128, 128
W_A = H_A * DV_A
CONV_W = 4
CHUNK = 64
H_B, R_KV, DV_B = 4, 128, 128
W_B = H_B * DV_B
H_IDX, D_IDX = 8, 64
TOPK_MAX = 256
N_BUCKETS, MAX_EXACT, MAX_DIST = 32, 16, 128
EPS = 1e-6

LANES = 128
TQ = 128
VMEM_LIMIT = 52 * 1024 * 1024

C_QKV = (0, 3 * W_A)
C_ZA = (C_QKV[1], C_QKV[1] + W_A)
C_BA = (C_ZA[1], C_ZA[1] + LANES)
C_QB = (C_BA[1], C_BA[1] + H_B * R_KV)
C_CKV = (C_QB[1], C_QB[1] + R_KV)
C_ZB = (C_CKV[1], C_CKV[1] + W_B)
C_IQ = (C_ZB[1], C_ZB[1] + H_IDX * D_IDX)
C_IKW = (C_IQ[1], C_IQ[1] + LANES)
D_IN_PAD = C_IKW[1]
IW_LANE = D_IDX

NEG_INF_KEY = -2139095041


def _sigmoid(x):
    return 1.0 / (1.0 + jnp.exp(-x))


def _silu(x):
    return x * _sigmoid(x)


def _softplus(x):
    return jnp.maximum(x, 0.0) + jnp.log1p(jnp.exp(-jnp.abs(x)))


def _dot(a, b, precision=None):
    return jnp.dot(a, b, precision=precision, preferred_element_type=F32)


def _dot_nt(a, b, precision=None):
    return lax.dot_general(a, b, (((1,), (1,)), ((), ())), precision=precision,
                           preferred_element_type=F32)


def _dot_tn(a, b, precision=None):
    return lax.dot_general(a, b, (((0,), (0,)), ((), ())), precision=precision,
                           preferred_element_type=F32)


def _bdot(a, b, precision=None):
    return _dot(a.astype(BF16), b.astype(BF16))


def _bdot_nt(a, b, precision=None):
    return _dot_nt(a.astype(BF16), b.astype(BF16))


def _bdot_tn(a, b, precision=None):
    return _dot_tn(a.astype(BF16), b.astype(BF16))


def _mod_kernel(c_ref, w_ref, b_ref, o_ref):
    c = c_ref[...]
    o_ref[...] = _dot(_silu(c), w_ref[...], HIGHEST) + b_ref[...]


def _modulation(c, w_ada, b_ada):
    bsz, d = c.shape
    n = w_ada.shape[1]
    tn = 512
    return pl.pallas_call(
        _mod_kernel,
        grid=(n // tn,),
        in_specs=[pl.BlockSpec((bsz, d), lambda j: (0, 0)),
                  pl.BlockSpec((d, tn), lambda j: (0, j)),
                  pl.BlockSpec((1, tn), lambda j: (0, j))],
        out_specs=pl.BlockSpec((bsz, tn), lambda j: (0, j)),
        out_shape=jax.ShapeDtypeStruct((bsz, n), F32),
        compiler_params=pltpu.CompilerParams(dimension_semantics=("arbitrary",),
                                             vmem_limit_bytes=VMEM_LIMIT),
        name="mod",
    )(c, w_ada, b_ada.reshape(1, n))


def _proj_kernel(x_ref, mod_ref, g_ref, w_ref,
                 qkv_ref, za_ref, ba_ref, qb_ref, ckv_ref, zb_ref, iq_ref, ikw_ref):
    x = x_ref[0]
    ms = jnp.mean(x * x, axis=-1, keepdims=True)
    xn = x * lax.rsqrt(ms + EPS)
    shift = mod_ref[0, 0:1, :]
    scale = mod_ref[0, 1:2, :]
    h = (xn * g_ref[...]) * (1.0 + scale) + shift
    hb = h.astype(BF16)

    def mm(cols):
        return _dot(hb, w_ref[:, cols[0]:cols[1]])

    qkv_ref[0] = mm(C_QKV)
    za_ref[0] = mm(C_ZA)
    ba_ref[0] = mm(C_BA)
    qb_ref[0] = mm(C_QB).astype(BF16)
    ckv_ref[0] = mm(C_CKV)
    zb_ref[0] = mm(C_ZB)
    iq_ref[0] = mm(C_IQ).astype(BF16)
    ikw_ref[0] = mm(C_IKW)


def _projection(x, mod3, g_pre, w_pad, tm):
    bsz, seq, d = x.shape
    widths = [(C_QKV, F32), (C_ZA, F32), (C_BA, F32), (C_QB, BF16),
              (C_CKV, F32), (C_ZB, F32), (C_IQ, BF16), (C_IKW, F32)]
    out_shape = [jax.ShapeDtypeStruct((bsz, seq, c[1] - c[0]), dt) for c, dt in widths]
    out_specs = [pl.BlockSpec((1, tm, c[1] - c[0]), lambda b, i: (b, i, 0)) for c, _ in widths]
    return pl.pallas_call(
        _proj_kernel,
        grid=(bsz, seq // tm),
        in_specs=[pl.BlockSpec((1, tm, d), lambda b, i: (b, i, 0)),
                  pl.BlockSpec((1, 3, d), lambda b, i: (b, 0, 0)),
                  pl.BlockSpec((1, d), lambda b, i: (0, 0)),
                  pl.BlockSpec((d, D_IN_PAD), lambda b, i: (0, 0))],
        out_specs=out_specs,
        out_shape=out_shape,
        compiler_params=pltpu.CompilerParams(dimension_semantics=("arbitrary", "arbitrary"),
                                             vmem_limit_bytes=VMEM_LIMIT),
        name="proj",
    )(x, mod3, g_pre.reshape(1, d), w_pad)


def _tri_inverse_many(lmats, level_masks, eye):
    ts = [eye - jnp.where(level_masks[0], lm, 0.0) for lm in lmats]
    for m in level_masks[1:]:
        xs = [_bdot(jnp.where(m, lm, 0.0), t) for lm, t in zip(lmats, ts)]
        ts = [t - _bdot(t, x) for t, x in zip(ts, xs)]
    return ts


def _gdn_kernel(qkv_ref, za_ref, ba_ref, cw_ref, alog_ref, dtb_ref, gn_ref, o_ref,
                ext_ref, s_ref, gc_ref, beta_ref, gct_ref, u_ref, wq_ref, kg_ref, at_ref,
                *, rb, tl, group_a):
    n_ch = tl // CHUNK
    li = pl.program_id(1)

    @pl.when(li == 0)
    def _():
        ext_ref[:, 0:8, :] = jnp.zeros((rb, 8, 3 * W_A), F32)
        s_ref[...] = jnp.zeros_like(s_ref)

    @pl.when(li > 0)
    def _():
        ext_ref[:, 0:8, :] = ext_ref[:, tl:tl + 8, :]

    ext_ref[:, 8:8 + tl, :] = qkv_ref[...]

    r_i = lax.broadcasted_iota(I32, (tl, tl), 0)
    c_i = lax.broadcasted_iota(I32, (tl, tl), 1)
    shift = CHUNK.bit_length() - 1
    tri = jnp.where((c_i <= r_i) & ((r_i >> shift) == (c_i >> shift)), 1.0, 0.0)
    for r in range(rb):
        ba = ba_ref[r]
        beta_ref[r] = _sigmoid(ba)
        g = -jnp.exp(alog_ref[...]) * _softplus(ba + dtb_ref[...])
        gc = _dot(tri, g, HIGHEST)
        gc_ref[r] = gc
        gct = gc.T
        for c in range(n_ch):
            gct_ref[r * n_ch + c] = gct[:, c * CHUNK:(c + 1) * CHUNK]

    row = lax.broadcasted_iota(I32, (CHUNK, CHUNK), 0)
    col = lax.broadcasted_iota(I32, (CHUNK, CHUNK), 1)
    causal = row >= col
    strict = row > col
    eye = jnp.where(row == col, 1.0, 0.0)
    level_masks = []
    s = 1
    while s < CHUNK:
        ls = s.bit_length() - 1
        level_masks.append(((row >> (ls + 1)) == (col >> (ls + 1)))
                           & (((row >> ls) & 1) == 1) & (((col >> ls) & 1) == 0))
        s *= 2

    def phase_a(ig, carry):
        probs = []
        for j in range(group_a):
            it = ig * group_a + j
            r = it // n_ch
            c = it - r * n_ch
            base = pl.multiple_of(c * CHUNK, CHUNK)
            gc_c = gc_ref[r, pl.ds(base, CHUNK), :]
            beta_c = beta_ref[r, pl.ds(base, CHUNK), :]
            gct_c = gct_ref[it]
            for h in range(H_A):
                def conv_silu(sec):
                    c0 = sec * W_A + h * DK_A
                    win = ext_ref[r, pl.ds(base, CHUNK + 8), c0:c0 + DK_A]
                    acc = None
                    for t in range(CONV_W):
                        sh = CONV_W - 1 - t
                        u = win if sh == 0 else pltpu.roll(win, sh, axis=0)
                        term = u[8:8 + CHUNK] * cw_ref[t:t + 1, c0:c0 + DK_A]
                        acc = term if acc is None else acc + term
                    return _silu(acc)

                q = conv_silu(0)
                k = conv_silu(1)
                v = conv_silu(2)
                q = q * lax.rsqrt(jnp.sum(q * q, axis=-1, keepdims=True) + EPS) * (DK_A ** -0.5)
                k = k * lax.rsqrt(jnp.sum(k * k, axis=-1, keepdims=True) + EPS)
                beta = beta_c[:, h:h + 1]
                gcol = gc_c[:, H_A + h:H_A + h + 1]
                grow = gct_c[H_A + h:H_A + h + 1, :]
                glast = gc_c[CHUNK - 1:CHUNK, H_A + h:H_A + h + 1]
                decay = jnp.exp(jnp.where(causal, gcol - grow, -jnp.inf))
                ecol = jnp.exp(gcol)
                kb = k * beta
                hc = slice(h * DV_A, (h + 1) * DV_A)
                wq_ref[it, h, CHUNK:2 * CHUNK, :] = (q * ecol).astype(BF16)
                kg_ref[r, pl.ds(base, CHUNK), hc] = (k * jnp.exp(glast - gcol)).astype(BF16)
                a2 = _bdot_nt(jnp.concatenate([kb, q], axis=0), k)
                at_ref[it, h] = (a2[CHUNK:] * decay).astype(BF16)
                probs.append(dict(
                    r=r, it=it, h=h, base=base, hc=hc,
                    lmat=jnp.where(strict, a2[:CHUNK] * decay, 0.0),
                    rhs=jnp.concatenate([v * beta, kb * ecol], axis=1).astype(BF16)))
        tmats = _tri_inverse_many([p["lmat"] for p in probs], level_masks, eye)
        for p, tmat in zip(probs, tmats):
            uw = _dot(tmat.astype(BF16), p["rhs"])
            u_ref[p["r"], pl.ds(p["base"], CHUNK), p["hc"]] = uw[:, :DV_A]
            wq_ref[p["it"], p["h"], 0:CHUNK, :] = uw[:, DV_A:].astype(BF16)
        return carry

    lax.fori_loop(0, rb * n_ch // group_a, phase_a, 0)

    def phase_b(c, carry):
        base = pl.multiple_of(c * CHUNK, CHUNK)
        chains = [(r, h) for r in range(rb) for h in range(H_A)]
        hcs = [slice(h * DV_A, (h + 1) * DV_A) for _, h in chains]
        sts = [s_ref[r, h] for r, h in chains]
        wss = [_dot(wq_ref[r * n_ch + c, h], st.astype(BF16)) for (r, h), st in zip(chains, sts)]
        vnbs = [(u_ref[r, pl.ds(base, CHUNK), hc] - ws[:CHUNK]).astype(BF16)
                for (r, h), hc, ws in zip(chains, hcs, wss)]
        upds = [_dot_tn(kg_ref[r, pl.ds(base, CHUNK), hc], vnb)
                for (r, h), hc, vnb in zip(chains, hcs, vnbs)]
        for (r, h), st, upd in zip(chains, sts, upds):
            glast = gc_ref[r, pl.ds(base + CHUNK - 1, 1), :][:, H_A + h:H_A + h + 1]
            s_ref[r, h] = st * jnp.exp(glast) + upd
        for (r, h), hc, ws, vnb in zip(chains, hcs, wss, vnbs):
            o = ws[CHUNK:] + _dot(at_ref[r * n_ch + c, h], vnb)
            on = o * lax.rsqrt(jnp.mean(o * o, axis=-1, keepdims=True) + EPS) * gn_ref[...]
            z = za_ref[r, pl.ds(base, CHUNK), hc]
            o_ref[r, pl.ds(base, CHUNK), hc] = on * _silu(z)
        return carry

    lax.fori_loop(0, n_ch, phase_b, 0)


def _gated_deltanet(qkv, za, ba, conv_w, alog_vec, dtb_vec, g_norm, rb, tl, group_a):
    bsz, seq, _ = qkv.shape
    n_ch = tl // CHUNK
    kern = functools.partial(_gdn_kernel, rb=rb, tl=tl, group_a=group_a)
    return pl.pallas_call(
        kern,
        grid=(bsz // rb, seq // tl),
        in_specs=[pl.BlockSpec((rb, tl, 3 * W_A), lambda b, i: (b, i, 0)),
                  pl.BlockSpec((rb, tl, W_A), lambda b, i: (b, i, 0)),
                  pl.BlockSpec((rb, tl, LANES), lambda b, i: (b, i, 0)),
                  pl.BlockSpec((CONV_W, 3 * W_A), lambda b, i: (0, 0)),
                  pl.BlockSpec((1, LANES), lambda b, i: (0, 0)),
                  pl.BlockSpec((1, LANES), lambda b, i: (0, 0)),
                  pl.BlockSpec((1, DV_A), lambda b, i: (0, 0))],
        out_specs=pl.BlockSpec((rb, tl, W_A), lambda b, i: (b, i, 0)),
        out_shape=jax.ShapeDtypeStruct((bsz, seq, W_A), F32),
        scratch_shapes=[pltpu.VMEM((rb, tl + 8, 3 * W_A), F32),
                        pltpu.VMEM((rb, H_A, DK_A, DV_A), F32),
                        pltpu.VMEM((rb, tl, LANES), F32),
                        pltpu.VMEM((rb, tl, LANES), F32),
                        pltpu.VMEM((rb * n_ch, LANES, CHUNK), F32),
                        pltpu.VMEM((rb, tl, W_A), F32),
                        pltpu.VMEM((rb * n_ch, H_A, 2 * CHUNK, DK_A), BF16),
                        pltpu.VMEM((rb, tl, W_A), BF16),
                        pltpu.VMEM((rb * n_ch, H_A, CHUNK, CHUNK), BF16)],
        compiler_params=pltpu.CompilerParams(dimension_semantics=("arbitrary", "arbitrary"),
                                             vmem_limit_bytes=VMEM_LIMIT),
        name="gdn",
    )(qkv, za, ba, conv_w, alog_vec, dtb_vec, g_norm.reshape(1, DV_A))


def _bias_kernel(rb_ref, o_ref):
    d = pl.program_id(0)
    kj = lax.broadcasted_iota(I32, (TQ, TQ), 0)
    qi = lax.broadcasted_iota(I32, (TQ, TQ), 1)
    dist = d * TQ + qi - kj
    n = jnp.maximum(dist, 0)
    nf = jnp.maximum(n, 1).astype(F32)
    large = MAX_EXACT + (jnp.log(nf / MAX_EXACT) / math.log(MAX_DIST / MAX_EXACT)
                         * (N_BUCKETS - MAX_EXACT)).astype(I32)
    large = jnp.minimum(large, N_BUCKETS - 1)
    bucket = jnp.where(n < MAX_EXACT, n, large)
    for h in range(H_B):
        acc = jnp.zeros((TQ, TQ), F32)
        for kb in range(N_BUCKETS):
            acc = jnp.where(bucket == kb, rb_ref[kb, h], acc)
        o_ref[0, :, h * TQ:(h + 1) * TQ] = acc


def _bias_tiles(rel_bias, n_diag):
    return pl.pallas_call(
        _bias_kernel,
        grid=(n_diag,),
        in_specs=[pl.BlockSpec(memory_space=pltpu.SMEM)],
        out_specs=pl.BlockSpec((1, TQ, H_B * TQ), lambda d: (d, 0, 0)),
        out_shape=jax.ShapeDtypeStruct((n_diag, TQ, H_B * TQ), F32),
        compiler_params=pltpu.CompilerParams(dimension_semantics=("arbitrary",)),
        name="bias",
    )(rel_bias)


def _fold_keys(x, op):
    x = op(x.reshape(4, TQ // 32, 8, x.shape[-1]), axis=1)
    return op(x, axis=0)


def _dsa_kernel(iq_ref, ikwq_ref, qb_ref, zb_ref, ikw_ref, ckv_ref, gkv_ref, wuv_ref, bias_ref,
                o_ref,
                kvn_ref, kvt_ref, iklo_ref, ikhi_ref, key_ref, khi_ref, klo_ref, am_ref, lg_ref,
                acc_ref,
                *, rb, seq, k_top):
    qi = pl.program_id(1)
    n_kc = qi + 1
    hq = H_B * TQ
    rows = range(rb)

    @pl.when(qi == 0)
    def _():
        for r in rows:
            for c in range(seq // TQ):
                sl = slice(c * TQ, (c + 1) * TQ)
                ckv = ckv_ref[r, sl, :]
                ms = jnp.mean(ckv * ckv, axis=-1, keepdims=True)
                kvn = (ckv * lax.rsqrt(ms + EPS)) * gkv_ref[...]
                kvn_ref[r, sl, :] = kvn.astype(BF16)
                kvt_ref[r, c] = kvn.T.astype(BF16)
                ikw = ikw_ref[r, sl, :]
                lane = lax.broadcasted_iota(I32, ikw.shape, 1)
                lo = jnp.where(lane < D_IDX, ikw, 0.0)
                iklo_ref[r, sl, :] = lo.astype(BF16)
                ikhi_ref[r, sl, :] = pltpu.roll(lo, D_IDX, axis=1).astype(BF16)

    iq4s, iwts = [], []
    for r in rows:
        iq = iq_ref[r]
        iq4s.append(jnp.concatenate(
            [iq[:, p * LANES:(p + 1) * LANES] for p in range(H_IDX // 2)], axis=0))
        iwts.append((ikwq_ref[r] * (H_IDX ** -0.5 * D_IDX ** -0.5)).T)

    key_j = lax.broadcasted_iota(I32, (TQ, TQ), 0)
    qry_t = qi * TQ + lax.broadcasted_iota(I32, (TQ, TQ), 1)

    def causal_mask(c):
        return (c * TQ + key_j) <= qry_t

    q4s = []
    for r in rows:
        qb = qb_ref[r]
        q4s.append(jnp.concatenate([qb[:, h * R_KV:(h + 1) * R_KV] for h in range(H_B)], axis=0))
    scale = R_KV ** -0.5

    def pair_loop(body, carry):
        carry = lax.fori_loop(0, n_kc >> 1, lambda i, cr: body((2 * i, 2 * i + 1), cr), carry)
        return lax.cond((n_kc & 1) == 1, lambda cr: body((n_kc - 1,), cr), lambda cr: cr, carry)

    def score_chunks(cs, carry):
        items = [(r, c, pl.ds(pl.multiple_of(c * TQ, TQ), TQ)) for c in cs for r in rows]
        res = [_dot_nt(iklo_ref[r, ks, :], iq4s[r]) for r, _, ks in items]
        ros = [_dot_nt(ikhi_ref[r, ks, :], iq4s[r]) for r, _, ks in items]
        sts = [_dot_nt(kvn_ref[r, ks, :], q4s[r]) for r, _, ks in items]
        for (r, c, _), re, ro, st in zip(items, res, ros, sts):
            s = None
            for p in range(H_IDX // 2):
                ps = slice(p * TQ, (p + 1) * TQ)
                we = iwts[r][IW_LANE + 2 * p:IW_LANE + 2 * p + 1, :]
                wo = iwts[r][IW_LANE + 2 * p + 1:IW_LANE + 2 * p + 2, :]
                t = jnp.maximum(re[:, ps], 0.0) * we + jnp.maximum(ro[:, ps], 0.0) * wo
                s = t if s is None else s + t
            s = jnp.where(s == 0.0, 0.0, s)
            s = jnp.where(causal_mask(c), s, -jnp.inf)
            bits = pltpu.bitcast(s, I32)
            key = jnp.where(bits < 0, bits ^ 0x7FFFFFFF, bits)
            key_ref[r, c] = key
            khi_ref[r, c] = (key >> 16).astype(I16)
            klo_ref[r, c] = ((key & 0xFFFF) + I16_MIN).astype(I16)
            lg_ref[r, c] = st * scale + bias_ref[qi - c]
        return carry

    pair_loop(score_chunks, 0)

    kf = float(k_top)
    n_pairs = (n_kc + 1) >> 1
    one16 = jnp.ones((TQ, TQ), I16)
    zero16 = jnp.zeros((TQ, TQ), I16)
    min16 = jnp.full((TQ, TQ), I16_MIN, I16)

    @pl.when((n_kc & 1) == 1)
    def _():
        for r in rows:
            khi_ref[r, n_kc] = min16
            klo_ref[r, n_kc] = min16

    def fold16(x):
        parts = [x[i * 16:(i + 1) * 16] for i in range(TQ // 16)]
        while len(parts) > 1:
            parts = [a + b for a, b in zip(parts[0::2], parts[1::2])]
        return parts[0]

    def total16(acc):
        return jnp.sum(acc.astype(F32), axis=0, keepdims=True)

    def radix16(src_ref, bases):
        def count_ge(cands):
            def body(i, accs):
                out = list(accs)
                for c in (2 * i, 2 * i + 1):
                    for r in rows:
                        hit = jnp.where(src_ref[r, c] >= cands[r], one16, zero16)
                        out[r] = out[r] + fold16(hit)
                return tuple(out)
            accs = lax.fori_loop(0, n_pairs, body,
                                 tuple(jnp.zeros((16, TQ), I16) for _ in rows))
            return [total16(a) for a in accs]

        def bit_body(i, tus):
            bit = lax.shift_left(jnp.int32(1), 15 - i)
            cands = [tu | bit for tu in tus]
            cnts = count_ge([(cu + I16_MIN).astype(I16)[0:1, :] for cu in cands])
            return tuple(jnp.where(cnt + base >= kf, cu, tu)
                         for cnt, base, cu, tu in zip(cnts, bases, cands, tus))

        return lax.fori_loop(0, 16, bit_body, tuple(jnp.zeros((16, TQ), I32) for _ in rows))

    his = radix16(khi_ref, [0.0 for _ in rows])
    hi16s = [(hu + I16_MIN).astype(I16)[0:1, :] for hu in his]

    def split_low(i, accs):
        out = list(accs)
        for c in (2 * i, 2 * i + 1):
            for r in rows:
                khi = khi_ref[r, c]
                out[r] = out[r] + fold16(jnp.where(khi > hi16s[r], one16, zero16))
                klo_ref[r, c] = jnp.where(khi == hi16s[r], klo_ref[r, c], min16)
        return tuple(out)

    above = lax.fori_loop(0, n_pairs, split_low, tuple(jnp.zeros((16, TQ), I16) for _ in rows))
    los = radix16(klo_ref, [total16(a) for a in above])
    thrs = [((hu + I16_MIN) * 65536 + lu)[0:1, :] for hu, lu in zip(his, los)]

    def count_both(c, carry):
        out = []
        for r in rows:
            cg, ce = carry[r]
            key = key_ref[r, c]
            out.append((cg + _fold_keys(jnp.where(key > thrs[r], 1.0, 0.0), jnp.sum),
                        ce + _fold_keys(jnp.where(key == thrs[r], 1.0, 0.0), jnp.sum)))
        return tuple(out)

    zero8 = jnp.zeros((8, TQ), F32)
    cges = lax.fori_loop(0, n_kc, count_both, tuple((zero8, zero8) for _ in rows))
    qrow = qi * TQ + lax.broadcasted_iota(I32, (1, TQ), 1)
    needs, simple_all = [], None
    for r in rows:
        cnt_gt = jnp.sum(cges[r][0], axis=0, keepdims=True)
        cnt_eq = jnp.sum(cges[r][1], axis=0, keepdims=True)
        need = kf - cnt_gt
        needs.append(need)
        simple = (cnt_eq <= need) | ((thrs[r] == NEG_INF_KEY) & (qrow < k_top))
        simple_all = simple if simple_all is None else (simple_all & simple)
    all_simple = jnp.min(jnp.where(simple_all, 1.0, 0.0)) > 0.5

    @pl.when(all_simple)
    def _():
        def body(c, carry):
            cm = causal_mask(c)
            for r in rows:
                sel = (key_ref[r, c] >= thrs[r]) & cm
                am_ref[r, c] = jnp.where(sel, 0.0, -jnp.inf)
            return carry
        lax.fori_loop(0, n_kc, body, 0)

    @pl.when(jnp.logical_not(all_simple))
    def _():
        lower = jnp.where(lax.broadcasted_iota(I32, (TQ, TQ), 1)
                          <= lax.broadcasted_iota(I32, (TQ, TQ), 0), 1.0, 0.0).astype(BF16)

        def body(c, seens):
            cm = causal_mask(c)
            out = []
            for r in rows:
                key = key_ref[r, c]
                eq = key == thrs[r]
                eqf = jnp.where(eq, 1.0, 0.0)
                rank = seens[r] + _dot(lower, eqf.astype(BF16))
                sel = (key > thrs[r]) | (eq & (rank <= needs[r]))
                am_ref[r, c] = jnp.where(sel & cm, 0.0, -jnp.inf)
                out.append(seens[r] + jnp.sum(eqf, axis=0, keepdims=True))
            return tuple(out)
        lax.fori_loop(0, n_kc, body, tuple(jnp.zeros((1, TQ), F32) for _ in rows))

    def pass1(cs, ms):
        out = list(ms)
        for c in cs:
            for r in rows:
                lg = lg_ref[r, c] + jnp.concatenate([am_ref[r, c]] * H_B, axis=1)
                lg_ref[r, c] = lg
                out[r] = jnp.maximum(out[r], _fold_keys(lg, jnp.max))
        return tuple(out)

    ms = pair_loop(pass1, tuple(jnp.full((8, hq), -jnp.inf, F32) for _ in rows))
    mrows = [jnp.max(m, axis=0, keepdims=True) for m in ms]
    acc_ref[...] = jnp.zeros_like(acc_ref)

    def pass2(cs, ls):
        out = list(ls)
        for r in rows:
            ps = [jnp.exp(lg_ref[r, c] - mrows[r]) for c in cs]
            kvt = jnp.concatenate([kvt_ref[r, c] for c in cs], axis=1)
            acc_ref[r] += _dot(kvt, jnp.concatenate(ps, axis=0).astype(BF16))
            for p in ps:
                out[r] = out[r] + _fold_keys(p, jnp.sum)
        return tuple(out)

    ls = pair_loop(pass2, tuple(jnp.zeros((8, hq), F32) for _ in rows))
    for r in rows:
        ot = (acc_ref[r] / jnp.sum(ls[r], axis=0, keepdims=True)).astype(BF16)
        for h in range(H_B):
            hs = slice(h * DV_B, (h + 1) * DV_B)
            y = _dot_tn(ot[:, h * TQ:(h + 1) * TQ], wuv_ref[h])
            o_ref[r, :, hs] = y * _silu(zb_ref[r, :, hs])


def _dsa_attention(iq, ikw, qb, zb, ckv, g_kv, w_uv_bf16, bias_tiles, k_top, rb):
    bsz, seq, _ = iq.shape
    n_q = seq // TQ
    hq = H_B * TQ
    kern = functools.partial(_dsa_kernel, rb=rb, seq=seq, k_top=k_top)
    return pl.pallas_call(
        kern,
        grid=(bsz // rb, n_q),
        in_specs=[pl.BlockSpec((rb, TQ, H_IDX * D_IDX), lambda b, i: (b, i, 0)),
                  pl.BlockSpec((rb, TQ, LANES), lambda b, i: (b, i, 0)),
                  pl.BlockSpec((rb, TQ, H_B * R_KV), lambda b, i: (b, i, 0)),
                  pl.BlockSpec((rb, TQ, W_B), lambda b, i: (b, i, 0)),
                  pl.BlockSpec((rb, seq, LANES), lambda b, i: (b, 0, 0)),
                  pl.BlockSpec((rb, seq, R_KV), lambda b, i: (b, 0, 0)),
                  pl.BlockSpec((1, R_KV), lambda b, i: (0, 0)),
                  pl.BlockSpec((H_B, R_KV, DV_B), lambda b, i: (0, 0, 0)),
                  pl.BlockSpec((n_q, TQ, hq), lambda b, i: (0, 0, 0))],
        out_specs=pl.BlockSpec((rb, TQ, W_B), lambda b, i: (b, i, 0)),
        out_shape=jax.ShapeDtypeStruct((bsz, seq, W_B), F32),
        scratch_shapes=[pltpu.VMEM((rb, seq, R_KV), BF16),
                        pltpu.VMEM((rb, n_q, R_KV, TQ), BF16),
                        pltpu.VMEM((rb, seq, LANES), BF16),
                        pltpu.VMEM((rb, seq, LANES), BF16),
                        pltpu.VMEM((rb, n_q, TQ, TQ), I32),
                        pltpu.VMEM((rb, n_q + 1, TQ, TQ), I16),
                        pltpu.VMEM((rb, n_q + 1, TQ, TQ), I16),
                        pltpu.VMEM((rb, n_q, TQ, TQ), F32),
                        pltpu.VMEM((rb, n_q, TQ, hq), F32),
                        pltpu.VMEM((rb, R_KV, hq), F32)],
        compiler_params=pltpu.CompilerParams(dimension_semantics=("arbitrary", "arbitrary"),
                                             vmem_limit_bytes=VMEM_LIMIT),
        name="dsa",
    )(iq, ikw, qb, zb, ikw, ckv, g_kv.reshape(1, R_KV), w_uv_bf16, bias_tiles)


def _out_kernel(x_ref, oa_ref, ob_ref, mod_ref, g_ref, w_ref, o_ref):
    mix_in = jnp.concatenate([oa_ref[0], ob_ref[0]], axis=-1).astype(BF16)
    mix = _dot(mix_in, w_ref[...])
    ms = jnp.mean(mix * mix, axis=-1, keepdims=True)
    normed = (mix * lax.rsqrt(ms + EPS)) * g_ref[...]
    o_ref[0] = x_ref[0] + mod_ref[0, 2:3, :] * normed


def _output(x, o_a, o_b, mod3, g_post, w_out_bf16, tm):
    bsz, seq, d = x.shape
    return pl.pallas_call(
        _out_kernel,
        grid=(bsz, seq // tm),
        in_specs=[pl.BlockSpec((1, tm, d), lambda b, i: (b, i, 0)),
                  pl.BlockSpec((1, tm, W_A), lambda b, i: (b, i, 0)),
                  pl.BlockSpec((1, tm, W_B), lambda b, i: (b, i, 0)),
                  pl.BlockSpec((1, 3, d), lambda b, i: (b, 0, 0)),
                  pl.BlockSpec((1, d), lambda b, i: (0, 0)),
                  pl.BlockSpec((W_A + W_B, d), lambda b, i: (0, 0))],
        out_specs=pl.BlockSpec((1, tm, d), lambda b, i: (b, i, 0)),
        out_shape=jax.ShapeDtypeStruct((bsz, seq, d), F32),
        compiler_params=pltpu.CompilerParams(dimension_semantics=("arbitrary", "arbitrary"),
                                             vmem_limit_bytes=VMEM_LIMIT),
        name="outproj",
    )(x, o_a, o_b, mod3, g_post.reshape(1, d), w_out_bf16)


def _pad_lanes(w):
    return jnp.pad(w, ((0, 0), (0, LANES - w.shape[1])))


def _pad_input_projection(w_in):
    o_ba = 3 * W_A + W_A
    o_qb = o_ba + 2 * H_A
    o_ik = o_qb + H_B * R_KV + R_KV + W_B + H_IDX * D_IDX
    return jnp.concatenate([w_in[:, :o_ba], _pad_lanes(w_in[:, o_ba:o_qb]),
                            w_in[:, o_qb:o_ik], _pad_lanes(w_in[:, o_ik:])], axis=1)


def _head_lanes(v):
    return jnp.zeros((1, LANES), F32).at[0, H_A:2 * H_A].set(v.astype(F32))


def kernel(x, c, w_ada, b_ada, g_pre, w_in, conv_w, a_log, dt_bias, g_gdn, g_kv, w_uv, rel_bias, w_out, g_post):
    bsz, seq, d = x.shape
    depth = w_ada.shape[0]
    assert seq % TQ == 0 and seq % CHUNK == 0
    k_top = min(TOPK_MAX, seq // 4)
    tm = min(512, seq)
    tl = min(256, seq)
    rb = 4 if bsz % 4 == 0 else (2 if bsz % 2 == 0 else 1)
    bias_tiles = _bias_tiles(rel_bias, seq // TQ)
    for layer in range(depth):
        mod3 = _modulation(c, w_ada[layer], b_ada[layer]).reshape(bsz, 3, d)
        w_pad = _pad_input_projection(w_in[layer]).astype(BF16)
        qkv, za, ba, qb, ckv, zb, iq, ikw = _projection(x, mod3, g_pre[layer], w_pad, tm)
        o_a = _gated_deltanet(qkv, za, ba, conv_w[layer], _head_lanes(a_log[layer]),
                              _head_lanes(dt_bias[layer]), g_gdn[layer], rb, tl, 4)
        o_b = _dsa_attention(iq, ikw, qb, zb, ckv, g_kv[layer], w_uv[layer].astype(BF16),
                             bias_tiles, k_top, 2 if bsz % 2 == 0 else 1)
        x = _output(x, o_a, o_b, mod3, g_post[layer], w_out[layer].astype(BF16), tm)
    return x
```

```python
import functools
import math

import jax
import jax.numpy as jnp
from jax import lax
from jax.experimental import pallas as pl
from jax.experimental.pallas import tpu as pltpu

F32 = jnp.float32
BF16 = jnp.bfloat16
I32 = jnp.int32
HIGHEST = lax.Precision.HIGHEST

H_A, DK_A, DV_A = 4, 128, 128
W_A = H_A * DV_A
CONV_W = 4
CHUNK = 64
H_B, R_KV, DV_B = 4, 128, 128
W_B = H_B * DV_B
H_IDX, D_IDX = 8, 64
TOPK_MAX = 256
N_BUCKETS, MAX_EXACT, MAX_DIST = 32, 16, 128
EPS = 1e-6

LANES = 128
TQ = 128
VMEM_LIMIT = 52 * 1024 * 1024

C_QKV = (0, 3 * W_A)
C_ZA = (C_QKV[1], C_QKV[1] + W_A)
C_BA = (C_ZA[1], C_ZA[1] + LANES)
C_QB = (C_BA[1], C_BA[1] + H_B * R_KV)
C_CKV = (C_QB[1], C_QB[1] + R_KV)
C_ZB = (C_CKV[1], C_CKV[1] + W_B)
C_IQ = (C_ZB[1], C_ZB[1] + H_IDX * D_IDX)
C_IKW = (C_IQ[1], C_IQ[1] + LANES)
D_IN_PAD = C_IKW[1]
IW_LANE = D_IDX

NEG_INF_KEY = -2139095041


def _sigmoid(x):
    return 1.0 / (1.0 + jnp.exp(-x))


def _silu(x):
    return x * _sigmoid(x)


def _softplus(x):
    return jnp.maximum(x, 0.0) + jnp.log1p(jnp.exp(-jnp.abs(x)))


def _dot(a, b, precision=None):
    return jnp.dot(a, b, precision=precision, preferred_element_type=F32)


def _dot_nt(a, b, precision=None):
    return lax.dot_general(a, b, (((1,), (1,)), ((), ())), precision=precision,
                           preferred_element_type=F32)


def _dot_tn(a, b, precision=None):
    return lax.dot_general(a, b, (((0,), (0,)), ((), ())), precision=precision,
                           preferred_element_type=F32)


def _bdot(a, b, precision=None):
    return _dot(a.astype(BF16), b.astype(BF16))


def _bdot_nt(a, b, precision=None):
    return _dot_nt(a.astype(BF16), b.astype(BF16))


def _bdot_tn(a, b, precision=None):
    return _dot_tn(a.astype(BF16), b.astype(BF16))


def _mod_kernel(c_ref, w_ref, b_ref, o_ref):
    c = c_ref[...]
    o_ref[...] = _dot(_silu(c), w_ref[...], HIGHEST) + b_ref[...]


def _modulation(c, w_ada, b_ada):
    bsz, d = c.shape
    n = w_ada.shape[1]
    tn = 512
    return pl.pallas_call(
        _mod_kernel,
        grid=(n // tn,),
        in_specs=[pl.BlockSpec((bsz, d), lambda j: (0, 0)),
                  pl.BlockSpec((d, tn), lambda j: (0, j)),
                  pl.BlockSpec((1, tn), lambda j: (0, j))],
        out_specs=pl.BlockSpec((bsz, tn), lambda j: (0, j)),
        out_shape=jax.ShapeDtypeStruct((bsz, n), F32),
        compiler_params=pltpu.CompilerParams(dimension_semantics=("arbitrary",),
                                             vmem_limit_bytes=VMEM_LIMIT),
        name="mod",
    )(c, w_ada, b_ada.reshape(1, n))


def _proj_kernel(x_ref, mod_ref, g_ref, w_ref,
                 qkv_ref, za_ref, ba_ref, qb_ref, ckv_ref, zb_ref, iq_ref, ikw_ref):
    x = x_ref[0]
    ms = jnp.mean(x * x, axis=-1, keepdims=True)
    xn = x * lax.rsqrt(ms + EPS)
    shift = mod_ref[0, 0:1, :]
    scale = mod_ref[0, 1:2, :]
    h = (xn * g_ref[...]) * (1.0 + scale) + shift
    hb = h.astype(BF16)

    def mm(cols):
        return _dot(hb, w_ref[:, cols[0]:cols[1]])

    qkv_ref[0] = mm(C_QKV)
    za_ref[0] = mm(C_ZA)
    ba_ref[0] = mm(C_BA)
    qb_ref[0] = mm(C_QB).astype(BF16)
    ckv_ref[0] = mm(C_CKV)
    zb_ref[0] = mm(C_ZB)
    iq_ref[0] = mm(C_IQ).astype(BF16)
    ikw_ref[0] = mm(C_IKW)


def _projection(x, mod3, g_pre, w_pad, tm):
    bsz, seq, d = x.shape
    widths = [(C_QKV, F32), (C_ZA, F32), (C_BA, F32), (C_QB, BF16),
              (C_CKV, F32), (C_ZB, F32), (C_IQ, BF16), (C_IKW, F32)]
    out_shape = [jax.ShapeDtypeStruct((bsz, seq, c[1] - c[0]), dt) for c, dt in widths]
    out_specs = [pl.BlockSpec((1, tm, c[1] - c[0]), lambda b, i: (b, i, 0)) for c, _ in widths]
    return pl.pallas_call(
        _proj_kernel,
        grid=(bsz, seq // tm),
        in_specs=[pl.BlockSpec((1, tm, d), lambda b, i: (b, i, 0)),
                  pl.BlockSpec((1, 3, d), lambda b, i: (b, 0, 0)),
                  pl.BlockSpec((1, d), lambda b, i: (0, 0)),
                  pl.BlockSpec((d, D_IN_PAD), lambda b, i: (0, 0))],
        out_specs=out_specs,
        out_shape=out_shape,
        compiler_params=pltpu.CompilerParams(dimension_semantics=("arbitrary", "arbitrary"),
                                             vmem_limit_bytes=VMEM_LIMIT),
        name="proj",
    )(x, mod3, g_pre.reshape(1, d), w_pad)


def _tri_inverse_many(lmats, level_masks, eye):
    ts = [eye - jnp.where(level_masks[0], lm, 0.0) for lm in lmats]
    for m in level_masks[1:]:
        xs = [_bdot(jnp.where(m, lm, 0.0), t) for lm, t in zip(lmats, ts)]
        ts = [t - _bdot(t, x) for t, x in zip(ts, xs)]
    return ts


def _gdn_kernel(qkv_ref, za_ref, ba_ref, cw_ref, alog_ref, dtb_ref, gn_ref, o_ref,
                ext_ref, s_ref, gc_ref, beta_ref, gct_ref, u_ref, wq_ref, kg_ref, at_ref,
                *, rb, tl, group_a):
    n_ch = tl // CHUNK
    li = pl.program_id(1)

    @pl.when(li == 0)
    def _():
        ext_ref[:, 0:8, :] = jnp.zeros((rb, 8, 3 * W_A), F32)
        s_ref[...] = jnp.zeros_like(s_ref)

    @pl.when(li > 0)
    def _():
        ext_ref[:, 0:8, :] = ext_ref[:, tl:tl + 8, :]

    ext_ref[:, 8:8 + tl, :] = qkv_ref[...]

    r_i = lax.broadcasted_iota(I32, (tl, tl), 0)
    c_i = lax.broadcasted_iota(I32, (tl, tl), 1)
    shift = CHUNK.bit_length() - 1
    tri = jnp.where((c_i <= r_i) & ((r_i >> shift) == (c_i >> shift)), 1.0, 0.0)
    for r in range(rb):
        ba = ba_ref[r]
        beta_ref[r] = _sigmoid(ba)
        g = -jnp.exp(alog_ref[...]) * _softplus(ba + dtb_ref[...])
        gc = _dot(tri, g, HIGHEST)
        gc_ref[r] = gc
        gct = gc.T
        for c in range(n_ch):
            gct_ref[r * n_ch + c] = gct[:, c * CHUNK:(c + 1) * CHUNK]

    row = lax.broadcasted_iota(I32, (CHUNK, CHUNK), 0)
    col = lax.broadcasted_iota(I32, (CHUNK, CHUNK), 1)
    causal = row >= col
    strict = row > col
    eye = jnp.where(row == col, 1.0, 0.0)
    level_masks = []
    s = 1
    while s < CHUNK:
        ls = s.bit_length() - 1
        level_masks.append(((row >> (ls + 1)) == (col >> (ls + 1)))
                           & (((row >> ls) & 1) == 1) & (((col >> ls) & 1) == 0))
        s *= 2

    def phase_a(ig, carry):
        probs = []
        for j in range(group_a):
            it = ig * group_a + j
            r = it // n_ch
            c = it - r * n_ch
            base = pl.multiple_of(c * CHUNK, CHUNK)
            gc_c = gc_ref[r, pl.ds(base, CHUNK), :]
            beta_c = beta_ref[r, pl.ds(base, CHUNK), :]
            gct_c = gct_ref[it]
            for h in range(H_A):
                def conv_silu(sec):
                    c0 = sec * W_A + h * DK_A
                    win = ext_ref[r, pl.ds(base, CHUNK + 8), c0:c0 + DK_A]
                    acc = None
                    for t in range(CONV_W):
                        sh = CONV_W - 1 - t
                        u = win if sh == 0 else pltpu.roll(win, sh, axis=0)
                        term = u[8:8 + CHUNK] * cw_ref[t:t + 1, c0:c0 + DK_A]
                        acc = term if acc is None else acc + term
                    return _silu(acc)

                q = conv_silu(0)
                k = conv_silu(1)
                v = conv_silu(2)
                q = q * lax.rsqrt(jnp.sum(q * q, axis=-1, keepdims=True) + EPS) * (DK_A ** -0.5)
                k = k * lax.rsqrt(jnp.sum(k * k, axis=-1, keepdims=True) + EPS)
                beta = beta_c[:, h:h + 1]
                gcol = gc_c[:, H_A + h:H_A + h + 1]
                grow = gct_c[H_A + h:H_A + h + 1, :]
                glast = gc_c[CHUNK - 1:CHUNK, H_A + h:H_A + h + 1]
                decay = jnp.exp(jnp.where(causal, gcol - grow, -jnp.inf))
                ecol = jnp.exp(gcol)
                kb = k * beta
                hc = slice(h * DV_A, (h + 1) * DV_A)
                wq_ref[it, h, CHUNK:2 * CHUNK, :] = (q * ecol).astype(BF16)
                kg_ref[r, pl.ds(base, CHUNK), hc] = (k * jnp.exp(glast - gcol)).astype(BF16)
                a2 = _bdot_nt(jnp.concatenate([kb, q], axis=0), k)
                at_ref[it, h] = (a2[CHUNK:] * decay).astype(BF16)
                probs.append(dict(
                    r=r, it=it, h=h, base=base, hc=hc,
                    lmat=jnp.where(strict, a2[:CHUNK] * decay, 0.0),
                    rhs=jnp.concatenate([v * beta, kb * ecol], axis=1).astype(BF16)))
        tmats = _tri_inverse_many([p["lmat"] for p in probs], level_masks, eye)
        for p, tmat in zip(probs, tmats):
            uw = _dot(tmat.astype(BF16), p["rhs"])
            u_ref[p["r"], pl.ds(p["base"], CHUNK), p["hc"]] = uw[:, :DV_A]
            wq_ref[p["it"], p["h"], 0:CHUNK, :] = uw[:, DV_A:].astype(BF16)
        return carry

    lax.fori_loop(0, rb * n_ch // group_a, phase_a, 0)

    def phase_b(c, carry):
        base = pl.multiple_of(c * CHUNK, CHUNK)
        chains = [(r, h) for r in range(rb) for h in range(H_A)]
        hcs = [slice(h * DV_A, (h + 1) * DV_A) for _, h in chains]
        sts = [s_ref[r, h] for r, h in chains]
        wss = [_dot(wq_ref[r * n_ch + c, h], st.astype(BF16)) for (r, h), st in zip(chains, sts)]
        vnbs = [(u_ref[r, pl.ds(base, CHUNK), hc] - ws[:CHUNK]).astype(BF16)
                for (r, h), hc, ws in zip(chains, hcs, wss)]
        upds = [_dot_tn(kg_ref[r, pl.ds(base, CHUNK), hc], vnb)
                for (r, h), hc, vnb in zip(chains, hcs, vnbs)]
        for (r, h), st, upd in zip(chains, sts, upds):
            glast = gc_ref[r, pl.ds(base + CHUNK - 1, 1), :][:, H_A + h:H_A + h + 1]
            s_ref[r, h] = st * jnp.exp(glast) + upd
        for (r, h), hc, ws, vnb in zip(chains, hcs, wss, vnbs):
            o = ws[CHUNK:] + _dot(at_ref[r * n_ch + c, h], vnb)
            on = o * lax.rsqrt(jnp.mean(o * o, axis=-1, keepdims=True) + EPS) * gn_ref[...]
            z = za_ref[r, pl.ds(base, CHUNK), hc]
            o_ref[r, pl.ds(base, CHUNK), hc] = on * _silu(z)
        return carry

    lax.fori_loop(0, n_ch, phase_b, 0)


def _gated_deltanet(qkv, za, ba, conv_w, alog_vec, dtb_vec, g_norm, rb, tl, group_a):
    bsz, seq, _ = qkv.shape
    n_ch = tl // CHUNK
    kern = functools.partial(_gdn_kernel, rb=rb, tl=tl, group_a=group_a)
    return pl.pallas_call(
        kern,
        grid=(bsz // rb, seq // tl),
        in_specs=[pl.BlockSpec((rb, tl, 3 * W_A), lambda b, i: (b, i, 0)),
                  pl.BlockSpec((rb, tl, W_A), lambda b, i: (b, i, 0)),
                  pl.BlockSpec((rb, tl, LANES), lambda b, i: (b, i, 0)),
                  pl.BlockSpec((CONV_W, 3 * W_A), lambda b, i: (0, 0)),
                  pl.BlockSpec((1, LANES), lambda b, i: (0, 0)),
                  pl.BlockSpec((1, LANES), lambda b, i: (0, 0)),
                  pl.BlockSpec((1, DV_A), lambda b, i: (0, 0))],
        out_specs=pl.BlockSpec((rb, tl, W_A), lambda b, i: (b, i, 0)),
        out_shape=jax.ShapeDtypeStruct((bsz, seq, W_A), F32),
        scratch_shapes=[pltpu.VMEM((rb, tl + 8, 3 * W_A), F32),
                        pltpu.VMEM((rb, H_A, DK_A, DV_A), F32),
                        pltpu.VMEM((rb, tl, LANES), F32),
                        pltpu.VMEM((rb, tl, LANES), F32),
                        pltpu.VMEM((rb * n_ch, LANES, CHUNK), F32),
                        pltpu.VMEM((rb, tl, W_A), F32),
                        pltpu.VMEM((rb * n_ch, H_A, 2 * CHUNK, DK_A), BF16),
                        pltpu.VMEM((rb, tl, W_A), BF16),
                        pltpu.VMEM((rb * n_ch, H_A, CHUNK, CHUNK), BF16)],
        compiler_params=pltpu.CompilerParams(dimension_semantics=("arbitrary", "arbitrary"),
                                             vmem_limit_bytes=VMEM_LIMIT),
        name="gdn",
    )(qkv, za, ba, conv_w, alog_vec, dtb_vec, g_norm.reshape(1, DV_A))


def _bias_kernel(rb_ref, o_ref):
    d = pl.program_id(0)
    kj = lax.broadcasted_iota(I32, (TQ, TQ), 0)
    qi = lax.broadcasted_iota(I32, (TQ, TQ), 1)
    dist = d * TQ + qi - kj
    n = jnp.maximum(dist, 0)
    nf = jnp.maximum(n, 1).astype(F32)
    large = MAX_EXACT + (jnp.log(nf / MAX_EXACT) / math.log(MAX_DIST / MAX_EXACT)
                         * (N_BUCKETS - MAX_EXACT)).astype(I32)
    large = jnp.minimum(large, N_BUCKETS - 1)
    bucket = jnp.where(n < MAX_EXACT, n, large)
    for h in range(H_B):
        acc = jnp.zeros((TQ, TQ), F32)
        for kb in range(N_BUCKETS):
            acc = jnp.where(bucket == kb, rb_ref[kb, h], acc)
        o_ref[0, :, h * TQ:(h + 1) * TQ] = acc


def _bias_tiles(rel_bias, n_diag):
    return pl.pallas_call(
        _bias_kernel,
        grid=(n_diag,),
        in_specs=[pl.BlockSpec(memory_space=pltpu.SMEM)],
        out_specs=pl.BlockSpec((1, TQ, H_B * TQ), lambda d: (d, 0, 0)),
        out_shape=jax.ShapeDtypeStruct((n_diag, TQ, H_B * TQ), F32),
        compiler_params=pltpu.CompilerParams(dimension_semantics=("arbitrary",)),
        name="bias",
    )(rel_bias)


def _fold_keys(x, op):
    x = op(x.reshape(4, TQ // 32, 8, x.shape[-1]), axis=1)
    return op(x, axis=0)


def _dsa_kernel(iq_ref, ikwq_ref, qb_ref, zb_ref, ikw_ref, ckv_ref, gkv_ref, wuv_ref, bias_ref,
                o_ref,
                kvn_ref, kvt_ref, iklo_ref, ikhi_ref, key_ref, am_ref, lg_ref, acc_ref,
                *, rb, seq, k_top):
    qi = pl.program_id(1)
    n_kc = qi + 1
    hq = H_B * TQ
    rows = range(rb)

    @pl.when(qi == 0)
    def _():
        for r in rows:
            for c in range(seq // TQ):
                sl = slice(c * TQ, (c + 1) * TQ)
                ckv = ckv_ref[r, sl, :]
                ms = jnp.mean(ckv * ckv, axis=-1, keepdims=True)
                kvn = (ckv * lax.rsqrt(ms + EPS)) * gkv_ref[...]
                kvn_ref[r, sl, :] = kvn.astype(BF16)
                kvt_ref[r, c] = kvn.T.astype(BF16)
                ikw = ikw_ref[r, sl, :]
                lane = lax.broadcasted_iota(I32, ikw.shape, 1)
                lo = jnp.where(lane < D_IDX, ikw, 0.0)
                iklo_ref[r, sl, :] = lo.astype(BF16)
                ikhi_ref[r, sl, :] = pltpu.roll(lo, D_IDX, axis=1).astype(BF16)

    iq4s, iwts = [], []
    for r in rows:
        iq = iq_ref[r]
        iq4s.append(jnp.concatenate(
            [iq[:, p * LANES:(p + 1) * LANES] for p in range(H_IDX // 2)], axis=0))
        iwts.append((ikwq_ref[r] * (H_IDX ** -0.5 * D_IDX ** -0.5)).T)

    key_j = lax.broadcasted_iota(I32, (TQ, TQ), 0)
    qry_t = qi * TQ + lax.broadcasted_iota(I32, (TQ, TQ), 1)

    def causal_mask(c):
        return (c * TQ + key_j) <= qry_t

    q4s = []
    for r in rows:
        qb = qb_ref[r]
        q4s.append(jnp.concatenate([qb[:, h * R_KV:(h + 1) * R_KV] for h in range(H_B)], axis=0))
    scale = R_KV ** -0.5

    def pair_loop(body, carry):
        carry = lax.fori_loop(0, n_kc >> 1, lambda i, cr: body((2 * i, 2 * i + 1), cr), carry)
        return lax.cond((n_kc & 1) == 1, lambda cr: body((n_kc - 1,), cr), lambda cr: cr, carry)

    def score_chunks(cs, carry):
        nc = len(cs)
        ks = pl.ds(pl.multiple_of(cs[0] * TQ, TQ), nc * TQ)
        rels = [_dot_nt(jnp.concatenate([iklo_ref[r, ks, :], ikhi_ref[r, ks, :]], axis=0),
                        iq4s[r]) for r in rows]
        sts = [_dot_nt(kvn_ref[r, ks, :], q4s[r]) for r in rows]
        for j, c in enumerate(cs):
            for r in rows:
                re = rels[r][j * TQ:(j + 1) * TQ]
                ro = rels[r][(nc + j) * TQ:(nc + j + 1) * TQ]
                s = None
                for p in range(H_IDX // 2):
                    ps = slice(p * TQ, (p + 1) * TQ)
                    we = iwts[r][IW_LANE + 2 * p:IW_LANE + 2 * p + 1, :]
                    wo = iwts[r][IW_LANE + 2 * p + 1:IW_LANE + 2 * p + 2, :]
                    t = jnp.maximum(re[:, ps], 0.0) * we + jnp.maximum(ro[:, ps], 0.0) * wo
                    s = t if s is None else s + t
                s = jnp.where(s == 0.0, 0.0, s)
                s = jnp.where(causal_mask(c), s, -jnp.inf)
                bits = pltpu.bitcast(s, I32)
                key_ref[r, c] = jnp.where(bits < 0, bits ^ 0x7FFFFFFF, bits)
                lg_ref[r, c] = sts[r][j * TQ:(j + 1) * TQ] * scale + bias_ref[qi - c]
        return carry

    pair_loop(score_chunks, 0)

    kf = float(k_top)
    n_pairs = (n_kc + 1) >> 1
    sign = jnp.int32(-2 ** 31)

    @pl.when((n_kc & 1) == 1)
    def _():
        for r in rows:
            key_ref[r, n_kc] = jnp.full((TQ, TQ), -2 ** 31, I32)

    def count_ge(cands):
        def body(i, accs):
            out = list(accs)
            for c in (2 * i, 2 * i + 1):
                for r in rows:
                    hit = jnp.where(key_ref[r, c] >= cands[r], 1.0, 0.0)
                    out[r] = out[r] + _fold_keys(hit, jnp.sum)
            return tuple(out)
        accs = lax.fori_loop(0, n_pairs, body, tuple(jnp.zeros((8, TQ), F32) for _ in rows))
        return [jnp.sum(a, axis=0, keepdims=True) for a in accs]

    def bit_body(i, tus):
        bit = lax.shift_left(jnp.int32(1), 31 - i)
        cands = [tu | bit for tu in tus]
        cnts = count_ge([(cu ^ sign)[0:1, :] for cu in cands])
        return tuple(jnp.where(cnt >= kf, cu, tu) for cnt, cu, tu in zip(cnts, cands, tus))

    tus = lax.fori_loop(0, 32, bit_body, tuple(jnp.zeros((8, TQ), I32) for _ in rows))
    thrs = [(tu ^ sign)[0:1, :] for tu in tus]

    def count_both(c, carry):
        out = []
        for r in rows:
            cg, ce = carry[r]
            key = key_ref[r, c]
            out.append((cg + _fold_keys(jnp.where(key > thrs[r], 1.0, 0.0), jnp.sum),
                        ce + _fold_keys(jnp.where(key == thrs[r], 1.0, 0.0), jnp.sum)))
        return tuple(out)

    zero8 = jnp.zeros((8, TQ), F32)
    cges = lax.fori_loop(0, n_kc, count_both, tuple((zero8, zero8) for _ in rows))
    qrow = qi * TQ + lax.broadcasted_iota(I32, (1, TQ), 1)
    needs, simple_all = [], None
    for r in rows:
        cnt_gt = jnp.sum(cges[r][0], axis=0, keepdims=True)
        cnt_eq = jnp.sum(cges[r][1], axis=0, keepdims=True)
        need = kf - cnt_gt
        needs.append(need)
        simple = (cnt_eq <= need) | ((thrs[r] == NEG_INF_KEY) & (qrow < k_top))
        simple_all = simple if simple_all is None else (simple_all & simple)
    all_simple = jnp.min(jnp.where(simple_all, 1.0, 0.0)) > 0.5

    @pl.when(all_simple)
    def _():
        def body(c, carry):
            cm = causal_mask(c)
            for r in rows:
                sel = (key_ref[r, c] >= thrs[r]) & cm
                am_ref[r, c] = jnp.where(sel, 0.0, -jnp.inf)
            return carry
        lax.fori_loop(0, n_kc, body, 0)

    @pl.when(jnp.logical_not(all_simple))
    def _():
        lower = jnp.where(lax.broadcasted_iota(I32, (TQ, TQ), 1)
                          <= lax.broadcasted_iota(I32, (TQ, TQ), 0), 1.0, 0.0).astype(BF16)

        def body(c, seens):
            cm = causal_mask(c)
            out = []
            for r in rows:
                key = key_ref[r, c]
                eq = key == thrs[r]
                eqf = jnp.where(eq, 1.0, 0.0)
                rank = seens[r] + _dot(lower, eqf.astype(BF16))
                sel = (key > thrs[r]) | (eq & (rank <= needs[r]))
                am_ref[r, c] = jnp.where(sel & cm, 0.0, -jnp.inf)
                out.append(seens[r] + jnp.sum(eqf, axis=0, keepdims=True))
            return tuple(out)
        lax.fori_loop(0, n_kc, body, tuple(jnp.zeros((1, TQ), F32) for _ in rows))

    def pass1(cs, ms):
        out = list(ms)
        for c in cs:
            for r in rows:
                lg = lg_ref[r, c] + jnp.concatenate([am_ref[r, c]] * H_B, axis=1)
                lg_ref[r, c] = lg
                out[r] = jnp.maximum(out[r], _fold_keys(lg, jnp.max))
        return tuple(out)

    ms = pair_loop(pass1, tuple(jnp.full((8, hq), -jnp.inf, F32) for _ in rows))
    mrows = [jnp.max(m, axis=0, keepdims=True) for m in ms]
    acc_ref[...] = jnp.zeros_like(acc_ref)

    def pass2(cs, ls):
        out = list(ls)
        for r in rows:
            ps = [jnp.exp(lg_ref[r, c] - mrows[r]) for c in cs]
            kvt = jnp.concatenate([kvt_ref[r, c] for c in cs], axis=1)
            acc_ref[r] += _dot(kvt, jnp.concatenate(ps, axis=0).astype(BF16))
            for p in ps:
                out[r] = out[r] + _fold_keys(p, jnp.sum)
        return tuple(out)

    ls = pair_loop(pass2, tuple(jnp.zeros((8, hq), F32) for _ in rows))
    for r in rows:
        ot = (acc_ref[r] / jnp.sum(ls[r], axis=0, keepdims=True)).astype(BF16)
        for h in range(H_B):
            hs = slice(h * DV_B, (h + 1) * DV_B)
            y = _dot_tn(ot[:, h * TQ:(h + 1) * TQ], wuv_ref[h])
            o_ref[r, :, hs] = y * _silu(zb_ref[r, :, hs])


def _dsa_attention(iq, ikw, qb, zb, ckv, g_kv, w_uv_bf16, bias_tiles, k_top, rb):
    bsz, seq, _ = iq.shape
    n_q = seq // TQ
    hq = H_B * TQ
    kern = functools.partial(_dsa_kernel, rb=rb, seq=seq, k_top=k_top)
    return pl.pallas_call(
        kern,
        grid=(bsz // rb, n_q),
        in_specs=[pl.BlockSpec((rb, TQ, H_IDX * D_IDX), lambda b, i: (b, i, 0)),
                  pl.BlockSpec((rb, TQ, LANES), lambda b, i: (b, i, 0)),
                  pl.BlockSpec((rb, TQ, H_B * R_KV), lambda b, i: (b, i, 0)),
                  pl.BlockSpec((rb, TQ, W_B), lambda b, i: (b, i, 0)),
                  pl.BlockSpec((rb, seq, LANES), lambda b, i: (b, 0, 0)),
                  pl.BlockSpec((rb, seq, R_KV), lambda b, i: (b, 0, 0)),
                  pl.BlockSpec((1, R_KV), lambda b, i: (0, 0)),
                  pl.BlockSpec((H_B, R_KV, DV_B), lambda b, i: (0, 0, 0)),
                  pl.BlockSpec((n_q, TQ, hq), lambda b, i: (0, 0, 0))],
        out_specs=pl.BlockSpec((rb, TQ, W_B), lambda b, i: (b, i, 0)),
        out_shape=jax.ShapeDtypeStruct((bsz, seq, W_B), F32),
        scratch_shapes=[pltpu.VMEM((rb, seq, R_KV), BF16),
                        pltpu.VMEM((rb, n_q, R_KV, TQ), BF16),
                        pltpu.VMEM((rb, seq, LANES), BF16),
                        pltpu.VMEM((rb, seq, LANES), BF16),
                        pltpu.VMEM((rb, n_q + 1, TQ, TQ), I32),
                        pltpu.VMEM((rb, n_q, TQ, TQ), F32),
                        pltpu.VMEM((rb, n_q, TQ, hq), F32),
                        pltpu.VMEM((rb, R_KV, hq), F32)],
        compiler_params=pltpu.CompilerParams(dimension_semantics=("arbitrary", "arbitrary"),
                                             vmem_limit_bytes=VMEM_LIMIT),
        name="dsa",
    )(iq, ikw, qb, zb, ikw, ckv, g_kv.reshape(1, R_KV), w_uv_bf16, bias_tiles)


def _out_kernel(x_ref, oa_ref, ob_ref, mod_ref, g_ref, w_ref, o_ref):
    mix_in = jnp.concatenate([oa_ref[0], ob_ref[0]], axis=-1).astype(BF16)
    mix = _dot(mix_in, w_ref[...])
    ms = jnp.mean(mix * mix, axis=-1, keepdims=True)
    normed = (mix * lax.rsqrt(ms + EPS)) * g_ref[...]
    o_ref[0] = x_ref[0] + mod_ref[0, 2:3, :] * normed


def _output(x, o_a, o_b, mod3, g_post, w_out_bf16, tm):
    bsz, seq, d = x.shape
    return pl.pallas_call(
        _out_kernel,
        grid=(bsz, seq // tm),
        in_specs=[pl.BlockSpec((1, tm, d), lambda b, i: (b, i, 0)),
                  pl.BlockSpec((1, tm, W_A), lambda b, i: (b, i, 0)),
                  pl.BlockSpec((1, tm, W_B), lambda b, i: (b, i, 0)),
                  pl.BlockSpec((1, 3, d), lambda b, i: (b, 0, 0)),
                  pl.BlockSpec((1, d), lambda b, i: (0, 0)),
                  pl.BlockSpec((W_A + W_B, d), lambda b, i: (0, 0))],
        out_specs=pl.BlockSpec((1, tm, d), lambda b, i: (b, i, 0)),
        out_shape=jax.ShapeDtypeStruct((bsz, seq, d), F32),
        compiler_params=pltpu.CompilerParams(dimension_semantics=("arbitrary", "arbitrary"),
                                             vmem_limit_bytes=VMEM_LIMIT),
        name="outproj",
    )(x, o_a, o_b, mod3, g_post.reshape(1, d), w_out_bf16)


def _pad_lanes(w):
    return jnp.pad(w, ((0, 0), (0, LANES - w.shape[1])))


def _pad_input_projection(w_in):
    o_ba = 3 * W_A + W_A
    o_qb = o_ba + 2 * H_A
    o_ik = o_qb + H_B * R_KV + R_KV + W_B + H_IDX * D_IDX
    return jnp.concatenate([w_in[:, :o_ba], _pad_lanes(w_in[:, o_ba:o_qb]),
                            w_in[:, o_qb:o_ik], _pad_lanes(w_in[:, o_ik:])], axis=1)


def _head_lanes(v):
    return jnp.zeros((1, LANES), F32).at[0, H_A:2 * H_A].set(v.astype(F32))


def kernel(x, c, w_ada, b_ada, g_pre, w_in, conv_w, a_log, dt_bias, g_gdn, g_kv, w_uv, rel_bias, w_out, g_post):
    bsz, seq, d = x.shape
    depth = w_ada.shape[0]
    assert seq % TQ == 0 and seq % CHUNK == 0
    k_top = min(TOPK_MAX, seq // 4)
    tm = min(512, seq)
    tl = min(256, seq)
    rb = 4 if bsz % 4 == 0 else (2 if bsz % 2 == 0 else 1)
    bias_tiles = _bias_tiles(rel_bias, seq // TQ)
    for layer in range(depth):
        mod3 = _modulation(c, w_ada[layer], b_ada[layer]).reshape(bsz, 3, d)
        w_pad = _pad_input_projection(w_in[layer]).astype(BF16)
        qkv, za, ba, qb, ckv, zb, iq, ikw = _projection(x, mod3, g_pre[layer], w_pad, tm)
        o_a = _gated_deltanet(qkv, za, ba, conv_w[layer], _head_lanes(a_log[layer]),
                              _head_lanes(dt_bias[layer]), g_gdn[layer], rb, tl, 4)
        o_b = _dsa_attention(iq, ikw, qb, zb, ckv, g_kv[layer], w_uv[layer].astype(BF16),
                             bias_tiles, k_top, 2 if bsz % 2 == 0 else 1)
        x = _output(x, o_a, o_b, mod3, g_post[layer], w_out[layer].astype(BF16), tm)
    return x
```

```python
import functools
import math

import jax
import jax.numpy as jnp
from jax import lax
from jax.experimental import pallas as pl
from jax.experimental.pallas import tpu as pltpu

F32 = jnp.float32
BF16 = jnp.bfloat16
I32 = jnp.int32
HIGHEST = lax.Precision.HIGHEST

H_A, DK_A, DV_A = 4, 128, 128
W_A = H_A * DV_A
CONV_W = 4
CHUNK = 64
H_B, R_KV, DV_B = 4, 128, 128
W_B = H_B * DV_B
H_IDX, D_IDX = 8, 64
TOPK_MAX = 256
N_BUCKETS, MAX_EXACT, MAX_DIST = 32, 16, 128
EPS = 1e-6

LANES = 128
TQ = 128
VMEM_LIMIT = 52 * 1024 * 1024

C_QKV = (0, 3 * W_A)
C_ZA = (C_QKV[1], C_QKV[1] + W_A)
C_BA = (C_ZA[1], C_ZA[1] + LANES)
C_QB = (C_BA[1], C_BA[1] + H_B * R_KV)
C_CKV = (C_QB[1], C_QB[1] + R_KV)
C_ZB = (C_CKV[1], C_CKV[1] + W_B)
C_IQ = (C_ZB[1], C_ZB[1] + H_IDX * D_IDX)
C_IKW = (C_IQ[1], C_IQ[1] + LANES)
D_IN_PAD = C_IKW[1]
IW_LANE = D_IDX

NEG_INF_KEY = -2139095041


def _sigmoid(x):
    return 1.0 / (1.0 + jnp.exp(-x))


def _silu(x):
    return x * _sigmoid(x)


def _softplus(x):
    return jnp.maximum(x, 0.0) + jnp.log1p(jnp.exp(-jnp.abs(x)))


def _dot(a, b, precision=None):
    return jnp.dot(a, b, precision=precision, preferred_element_type=F32)


def _dot_nt(a, b, precision=None):
    return lax.dot_general(a, b, (((1,), (1,)), ((), ())), precision=precision,
                           preferred_element_type=F32)


def _dot_tn(a, b, precision=None):
    return lax.dot_general(a, b, (((0,), (0,)), ((), ())), precision=precision,
                           preferred_element_type=F32)


def _bdot(a, b, precision=None):
    return _dot(a.astype(BF16), b.astype(BF16))


def _bdot_nt(a, b, precision=None):
    return _dot_nt(a.astype(BF16), b.astype(BF16))


def _bdot_tn(a, b, precision=None):
    return _dot_tn(a.astype(BF16), b.astype(BF16))


def _mod_kernel(c_ref, w_ref, b_ref, o_ref):
    c = c_ref[...]
    o_ref[...] = _dot(_silu(c), w_ref[...], HIGHEST) + b_ref[...]


def _modulation(c, w_ada, b_ada):
    bsz, d = c.shape
    n = w_ada.shape[1]
    tn = 512
    return pl.pallas_call(
        _mod_kernel,
        grid=(n // tn,),
        in_specs=[pl.BlockSpec((bsz, d), lambda j: (0, 0)),
                  pl.BlockSpec((d, tn), lambda j: (0, j)),
                  pl.BlockSpec((1, tn), lambda j: (0, j))],
        out_specs=pl.BlockSpec((bsz, tn), lambda j: (0, j)),
        out_shape=jax.ShapeDtypeStruct((bsz, n), F32),
        compiler_params=pltpu.CompilerParams(dimension_semantics=("arbitrary",),
                                             vmem_limit_bytes=VMEM_LIMIT),
        name="mod",
    )(c, w_ada, b_ada.reshape(1, n))


def _proj_kernel(x_ref, mod_ref, g_ref, w_ref, cw_ref,
                 qkv_ref, za_ref, ba_ref, qb_ref, ckv_ref, zb_ref, iq_ref, ikw_ref,
                 ext_ref, *, tm):
    x = x_ref[0]
    ms = jnp.mean(x * x, axis=-1, keepdims=True)
    xn = x * lax.rsqrt(ms + EPS)
    shift = mod_ref[0, 0:1, :]
    scale = mod_ref[0, 1:2, :]
    h = (xn * g_ref[...]) * (1.0 + scale) + shift
    hb = h.astype(BF16)

    def mm(cols):
        return _dot(hb, w_ref[:, cols[0]:cols[1]])

    @pl.when(pl.program_id(1) == 0)
    def _():
        ext_ref[0:8, :] = jnp.zeros((8, 3 * W_A), F32)

    def conv_section(sec):
        rows = 128
        for r0 in range(0, tm, rows):
            for hh in range(H_A):
                cb = sec * H_A + hh
                cs = slice(cb * DK_A, (cb + 1) * DK_A)
                acc = None
                for t in range(CONV_W):
                    lo = r0 + 8 - (CONV_W - 1) + t
                    term = ext_ref[lo:lo + rows, cs] * cw_ref[t:t + 1, cs]
                    acc = term if acc is None else acc + term
                y = _silu(acc)
                if sec < 2:
                    y = y * lax.rsqrt(jnp.sum(y * y, axis=-1, keepdims=True) + EPS)
                if sec == 0:
                    y = y * (DK_A ** -0.5)
                qkv_ref[0, r0:r0 + rows, cs] = y
        ext_ref[0:8, sec * W_A:(sec + 1) * W_A] = ext_ref[tm:tm + 8, sec * W_A:(sec + 1) * W_A]

    def mm_section(sec):
        c0 = C_QKV[0] + sec * W_A
        ext_ref[8:8 + tm, sec * W_A:(sec + 1) * W_A] = mm((c0, c0 + W_A))

    mm_section(0)
    mm_section(1)
    conv_section(0)
    mm_section(2)
    conv_section(1)
    za_ref[0] = mm(C_ZA)
    ba_ref[0] = mm(C_BA)
    qb_ref[0] = mm(C_QB).astype(BF16)
    conv_section(2)
    ckv_ref[0] = mm(C_CKV)
    zb_ref[0] = mm(C_ZB)
    iq_ref[0] = mm(C_IQ).astype(BF16)
    ikw_ref[0] = mm(C_IKW)


def _projection(x, mod3, g_pre, w_pad, conv_w, tm):
    bsz, seq, d = x.shape
    widths = [(C_QKV, F32), (C_ZA, F32), (C_BA, F32), (C_QB, BF16),
              (C_CKV, F32), (C_ZB, F32), (C_IQ, BF16), (C_IKW, F32)]
    out_shape = [jax.ShapeDtypeStruct((bsz, seq, c[1] - c[0]), dt) for c, dt in widths]
    out_specs = [pl.BlockSpec((1, tm, c[1] - c[0]), lambda b, i: (b, i, 0)) for c, _ in widths]
    return pl.pallas_call(
        functools.partial(_proj_kernel, tm=tm),
        grid=(bsz, seq // tm),
        in_specs=[pl.BlockSpec((1, tm, d), lambda b, i: (b, i, 0)),
                  pl.BlockSpec((1, 3, d), lambda b, i: (b, 0, 0)),
                  pl.BlockSpec((1, d), lambda b, i: (0, 0)),
                  pl.BlockSpec((d, D_IN_PAD), lambda b, i: (0, 0)),
                  pl.BlockSpec((CONV_W, 3 * W_A), lambda b, i: (0, 0))],
        out_specs=out_specs,
        out_shape=out_shape,
        scratch_shapes=[pltpu.VMEM((tm + 8, 3 * W_A), F32)],
        compiler_params=pltpu.CompilerParams(dimension_semantics=("arbitrary", "arbitrary"),
                                             vmem_limit_bytes=VMEM_LIMIT),
        name="proj",
    )(x, mod3, g_pre.reshape(1, d), w_pad, conv_w)


def _tri_inverse_many(lmats, level_masks, eye):
    ts = [eye - jnp.where(level_masks[0], lm, 0.0) for lm in lmats]
    for m in level_masks[1:]:
        xs = [_bdot(jnp.where(m, lm, 0.0), t) for lm, t in zip(lmats, ts)]
        ts = [t - _bdot(t, x) for t, x in zip(ts, xs)]
    return ts


def _gdn_kernel(qkv_ref, za_ref, ba_ref, alog_ref, dtb_ref, gn_ref, o_ref,
                s_ref, gc_ref, beta_ref, gct_ref, u_ref, wq_ref, kg_ref, at_ref,
                *, rb, tl, group_a):
    n_ch = tl // CHUNK
    li = pl.program_id(1)

    @pl.when(li == 0)
    def _():
        s_ref[...] = jnp.zeros_like(s_ref)

    r_i = lax.broadcasted_iota(I32, (tl, tl), 0)
    c_i = lax.broadcasted_iota(I32, (tl, tl), 1)
    shift = CHUNK.bit_length() - 1
    tri = jnp.where((c_i <= r_i) & ((r_i >> shift) == (c_i >> shift)), 1.0, 0.0)
    for r in range(rb):
        ba = ba_ref[r]
        beta_ref[r] = _sigmoid(ba)
        g = -jnp.exp(alog_ref[...]) * _softplus(ba + dtb_ref[...])
        gc = _dot(tri, g, HIGHEST)
        gc_ref[r] = gc
        gct = gc.T
        for c in range(n_ch):
            gct_ref[r * n_ch + c] = gct[:, c * CHUNK:(c + 1) * CHUNK]

    row = lax.broadcasted_iota(I32, (CHUNK, CHUNK), 0)
    col = lax.broadcasted_iota(I32, (CHUNK, CHUNK), 1)
    causal = row >= col
    strict = row > col
    eye = jnp.where(row == col, 1.0, 0.0)
    level_masks = []
    s = 1
    while s < CHUNK:
        ls = s.bit_length() - 1
        level_masks.append(((row >> (ls + 1)) == (col >> (ls + 1)))
                           & (((row >> ls) & 1) == 1) & (((col >> ls) & 1) == 0))
        s *= 2

    def phase_a(ig, carry):
        probs = []
        for j in range(group_a):
            it = ig * group_a + j
            r = it // n_ch
            c = it - r * n_ch
            base = pl.multiple_of(c * CHUNK, CHUNK)
            gc_c = gc_ref[r, pl.ds(base, CHUNK), :]
            beta_c = beta_ref[r, pl.ds(base, CHUNK), :]
            gct_c = gct_ref[it]
            for h in range(H_A):
                q, k, v = (qkv_ref[r, pl.ds(base, CHUNK),
                                   sec * W_A + h * DK_A:sec * W_A + (h + 1) * DK_A]
                           for sec in range(3))
                beta = beta_c[:, h:h + 1]
                gcol = gc_c[:, H_A + h:H_A + h + 1]
                grow = gct_c[H_A + h:H_A + h + 1, :]
                glast = gc_c[CHUNK - 1:CHUNK, H_A + h:H_A + h + 1]
                decay = jnp.exp(jnp.where(causal, gcol - grow, -jnp.inf))
                ecol = jnp.exp(gcol)
                kb = k * beta
                hc = slice(h * DV_A, (h + 1) * DV_A)
                wq_ref[it, h, CHUNK:2 * CHUNK, :] = (q * ecol).astype(BF16)
                kg_ref[r, pl.ds(base, CHUNK), hc] = (k * jnp.exp(glast - gcol)).astype(BF16)
                a2 = _bdot_nt(jnp.concatenate([kb, q], axis=0), k)
                at_ref[it, h] = (a2[CHUNK:] * decay).astype(BF16)
                probs.append(dict(
                    r=r, it=it, h=h, base=base, hc=hc,
                    lmat=jnp.where(strict, a2[:CHUNK] * decay, 0.0),
                    rhs=jnp.concatenate([v * beta, kb * ecol], axis=1).astype(BF16)))
        tmats = _tri_inverse_many([p["lmat"] for p in probs], level_masks, eye)
        for p, tmat in zip(probs, tmats):
            uw = _dot(tmat.astype(BF16), p["rhs"])
            u_ref[p["r"], pl.ds(p["base"], CHUNK), p["hc"]] = uw[:, :DV_A]
            wq_ref[p["it"], p["h"], 0:CHUNK, :] = uw[:, DV_A:].astype(BF16)
        return carry

    lax.fori_loop(0, rb * n_ch // group_a, phase_a, 0)

    def phase_b(c, carry):
        base = pl.multiple_of(c * CHUNK, CHUNK)
        chains = [(r, h) for r in range(rb) for h in range(H_A)]
        hcs = [slice(h * DV_A, (h + 1) * DV_A) for _, h in chains]
        sts = [s_ref[r, h] for r, h in chains]
        wss = [_dot(wq_ref[r * n_ch + c, h], st.astype(BF16)) for (r, h), st in zip(chains, sts)]
        vnbs = [(u_ref[r, pl.ds(base, CHUNK), hc] - ws[:CHUNK]).astype(BF16)
                for (r, h), hc, ws in zip(chains, hcs, wss)]
        upds = [_dot_tn(kg_ref[r, pl.ds(base, CHUNK), hc], vnb)
                for (r, h), hc, vnb in zip(chains, hcs, vnbs)]
        for (r, h), st, upd in zip(chains, sts, upds):
            glast = gc_ref[r, pl.ds(base + CHUNK - 1, 1), :][:, H_A + h:H_A + h + 1]
            s_ref[r, h] = st * jnp.exp(glast) + upd
        for (r, h), hc, ws, vnb in zip(chains, hcs, wss, vnbs):
            o = ws[CHUNK:] + _dot(at_ref[r * n_ch + c, h], vnb)
            on = o * lax.rsqrt(jnp.mean(o * o, axis=-1, keepdims=True) + EPS) * gn_ref[...]
            z = za_ref[r, pl.ds(base, CHUNK), hc]
            o_ref[r, pl.ds(base, CHUNK), hc] = on * _silu(z)
        return carry

    lax.fori_loop(0, n_ch, phase_b, 0)


def _gated_deltanet(qkv, za, ba, alog_vec, dtb_vec, g_norm, rb, tl, group_a):
    bsz, seq, _ = qkv.shape
    n_ch = tl // CHUNK
    kern = functools.partial(_gdn_kernel, rb=rb, tl=tl, group_a=group_a)
    return pl.pallas_call(
        kern,
        grid=(bsz // rb, seq // tl),
        in_specs=[pl.BlockSpec((rb, tl, 3 * W_A), lambda b, i: (b, i, 0)),
                  pl.BlockSpec((rb, tl, W_A), lambda b, i: (b, i, 0)),
                  pl.BlockSpec((rb, tl, LANES), lambda b, i: (b, i, 0)),
                  pl.BlockSpec((1, LANES), lambda b, i: (0, 0)),
                  pl.BlockSpec((1, LANES), lambda b, i: (0, 0)),
                  pl.BlockSpec((1, DV_A), lambda b, i: (0, 0))],
        out_specs=pl.BlockSpec((rb, tl, W_A), lambda b, i: (b, i, 0)),
        out_shape=jax.ShapeDtypeStruct((bsz, seq, W_A), F32),
        scratch_shapes=[pltpu.VMEM((rb, H_A, DK_A, DV_A), F32),
                        pltpu.VMEM((rb, tl, LANES), F32),
                        pltpu.VMEM((rb, tl, LANES), F32),
                        pltpu.VMEM((rb * n_ch, LANES, CHUNK), F32),
                        pltpu.VMEM((rb, tl, W_A), F32),
                        pltpu.VMEM((rb * n_ch, H_A, 2 * CHUNK, DK_A), BF16),
                        pltpu.VMEM((rb, tl, W_A), BF16),
                        pltpu.VMEM((rb * n_ch, H_A, CHUNK, CHUNK), BF16)],
        compiler_params=pltpu.CompilerParams(dimension_semantics=("arbitrary", "arbitrary"),
                                             vmem_limit_bytes=VMEM_LIMIT),
        name="gdn",
    )(qkv, za, ba, alog_vec, dtb_vec, g_norm.reshape(1, DV_A))


def _bias_kernel(rb_ref, o_ref):
    d = pl.program_id(0)
    kj = lax.broadcasted_iota(I32, (TQ, TQ), 0)
    qi = lax.broadcasted_iota(I32, (TQ, TQ), 1)
    dist = d * TQ + qi - kj
    n = jnp.maximum(dist, 0)
    nf = jnp.maximum(n, 1).astype(F32)
    large = MAX_EXACT + (jnp.log(nf / MAX_EXACT) / math.log(MAX_DIST / MAX_EXACT)
                         * (N_BUCKETS - MAX_EXACT)).astype(I32)
    large = jnp.minimum(large, N_BUCKETS - 1)
    bucket = jnp.where(n < MAX_EXACT, n, large)
    for h in range(H_B):
        acc = jnp.zeros((TQ, TQ), F32)
        for kb in range(N_BUCKETS):
            acc = jnp.where(bucket == kb, rb_ref[kb, h], acc)
        o_ref[0, :, h * TQ:(h + 1) * TQ] = acc


def _bias_tiles(rel_bias, n_diag):
    return pl.pallas_call(
        _bias_kernel,
        grid=(n_diag,),
        in_specs=[pl.BlockSpec(memory_space=pltpu.SMEM)],
        out_specs=pl.BlockSpec((1, TQ, H_B * TQ), lambda d: (d, 0, 0)),
        out_shape=jax.ShapeDtypeStruct((n_diag, TQ, H_B * TQ), F32),
        compiler_params=pltpu.CompilerParams(dimension_semantics=("arbitrary",)),
        name="bias",
    )(rel_bias)


def _fold_keys(x, op):
    x = op(x.reshape(4, TQ // 32, 8, x.shape[-1]), axis=1)
    return op(x, axis=0)


def _dsa_kernel(iq_ref, ikwq_ref, qb_ref, zb_ref, ikw_ref, ckv_ref, gkv_ref, wuv_ref, bias_ref,
                o_ref,
                kvn_ref, kvt_ref, iklo_ref, ikhi_ref, key_ref, am_ref, lg_ref, acc_ref,
                *, rb, seq, k_top):
    qi = pl.program_id(1)
    n_kc = qi + 1
    hq = H_B * TQ
    rows = range(rb)

    @pl.when(qi == 0)
    def _():
        for r in rows:
            for c in range(seq // TQ):
                sl = slice(c * TQ, (c + 1) * TQ)
                ckv = ckv_ref[r, sl, :]
                ms = jnp.mean(ckv * ckv, axis=-1, keepdims=True)
                kvn = (ckv * lax.rsqrt(ms + EPS)) * gkv_ref[...]
                kvn_ref[r, sl, :] = kvn.astype(BF16)
                kvt_ref[r, c] = kvn.T.astype(BF16)
                ikw = ikw_ref[r, sl, :]
                lane = lax.broadcasted_iota(I32, ikw.shape, 1)
                lo = jnp.where(lane < D_IDX, ikw, 0.0)
                iklo_ref[r, sl, :] = lo.astype(BF16)
                ikhi_ref[r, sl, :] = pltpu.roll(lo, D_IDX, axis=1).astype(BF16)

    iq4s, iwts = [], []
    for r in rows:
        iq = iq_ref[r]
        iq4s.append(jnp.concatenate(
            [iq[:, p * LANES:(p + 1) * LANES] for p in range(H_IDX // 2)], axis=0))
        iwts.append((ikwq_ref[r] * (H_IDX ** -0.5 * D_IDX ** -0.5)).T)

    key_j = lax.broadcasted_iota(I32, (TQ, TQ), 0)
    qry_t = qi * TQ + lax.broadcasted_iota(I32, (TQ, TQ), 1)

    def causal_mask(c):
        return (c * TQ + key_j) <= qry_t

    q4s = []
    for r in rows:
        qb = qb_ref[r]
        q4s.append(jnp.concatenate([qb[:, h * R_KV:(h + 1) * R_KV] for h in range(H_B)], axis=0))
    scale = R_KV ** -0.5

    def pair_loop(body, carry):
        carry = lax.fori_loop(0, n_kc >> 1, lambda i, cr: body((2 * i, 2 * i + 1), cr), carry)
        return lax.cond((n_kc & 1) == 1, lambda cr: body((n_kc - 1,), cr), lambda cr: cr, carry)

    def score_chunks(cs, carry):
        nc = len(cs)
        ks = pl.ds(pl.multiple_of(cs[0] * TQ, TQ), nc * TQ)
        rels = [_dot_nt(jnp.concatenate([iklo_ref[r, ks, :], ikhi_ref[r, ks, :]], axis=0),
                        iq4s[r]) for r in rows]
        sts = [_dot_nt(kvn_ref[r, ks, :], q4s[r]) for r in rows]
        for j, c in enumerate(cs):
            for r in rows:
                re = rels[r][j * TQ:(j + 1) * TQ]
                ro = rels[r][(nc + j) * TQ:(nc + j + 1) * TQ]
                s = None
                for p in range(H_IDX // 2):
                    ps = slice(p * TQ, (p + 1) * TQ)
                    we = iwts[r][IW_LANE + 2 * p:IW_LANE + 2 * p + 1, :]
                    wo = iwts[r][IW_LANE + 2 * p + 1:IW_LANE + 2 * p + 2, :]
                    t = jnp.maximum(re[:, ps], 0.0) * we + jnp.maximum(ro[:, ps], 0.0) * wo
                    s = t if s is None else s + t
                s = jnp.where(s == 0.0, 0.0, s)
                s = jnp.where(causal_mask(c), s, -jnp.inf)
                bits = pltpu.bitcast(s, I32)
                key_ref[r, c] = jnp.where(bits < 0, bits ^ 0x7FFFFFFF, bits)
                lg_ref[r, c] = sts[r][j * TQ:(j + 1) * TQ] * scale + bias_ref[qi - c]
        return carry

    pair_loop(score_chunks, 0)

    kf = float(k_top)
    n_pairs = (n_kc + 1) >> 1
    sign = jnp.int32(-2 ** 31)

    @pl.when((n_kc & 1) == 1)
    def _():
        for r in rows:
            key_ref[r, n_kc] = jnp.full((TQ, TQ), -2 ** 31, I32)

    def count_ge(cands):
        def body(i, accs):
            out = list(accs)
            for c in (2 * i, 2 * i + 1):
                for r in rows:
                    hit = jnp.where(key_ref[r, c] >= cands[r], 1.0, 0.0)
                    out[r] = out[r] + _fold_keys(hit, jnp.sum)
            return tuple(out)
        accs = lax.fori_loop(0, n_pairs, body, tuple(jnp.zeros((8, TQ), F32) for _ in rows))
        return [jnp.sum(a, axis=0, keepdims=True) for a in accs]

    def bit_body(i, tus):
        bit = lax.shift_left(jnp.int32(1), 31 - i)
        cands = [tu | bit for tu in tus]
        cnts = count_ge([(cu ^ sign)[0:1, :] for cu in cands])
        return tuple(jnp.where(cnt >= kf, cu, tu) for cnt, cu, tu in zip(cnts, cands, tus))

    tus = lax.fori_loop(0, 32, bit_body, tuple(jnp.zeros((8, TQ), I32) for _ in rows))
    thrs = [(tu ^ sign)[0:1, :] for tu in tus]

    def count_both(c, carry):
        out = []
        for r in rows:
            cg, ce = carry[r]
            key = key_ref[r, c]
            out.append((cg + _fold_keys(jnp.where(key > thrs[r], 1.0, 0.0), jnp.sum),
                        ce + _fold_keys(jnp.where(key == thrs[r], 1.0, 0.0), jnp.sum)))
        return tuple(out)

    zero8 = jnp.zeros((8, TQ), F32)
    cges = lax.fori_loop(0, n_kc, count_both, tuple((zero8, zero8) for _ in rows))
    qrow = qi * TQ + lax.broadcasted_iota(I32, (1, TQ), 1)
    needs, simple_all = [], None
    for r in rows:
        cnt_gt = jnp.sum(cges[r][0], axis=0, keepdims=True)
        cnt_eq = jnp.sum(cges[r][1], axis=0, keepdims=True)
        need = kf - cnt_gt
        needs.append(need)
        simple = (cnt_eq <= need) | ((thrs[r] == NEG_INF_KEY) & (qrow < k_top))
        simple_all = simple if simple_all is None else (simple_all & simple)
    all_simple = jnp.min(jnp.where(simple_all, 1.0, 0.0)) > 0.5

    @pl.when(all_simple)
    def _():
        def body(c, carry):
            cm = causal_mask(c)
            for r in rows:
                sel = (key_ref[r, c] >= thrs[r]) & cm
                am_ref[r, c] = jnp.where(sel, 0.0, -jnp.inf)
            return carry
        lax.fori_loop(0, n_kc, body, 0)

    @pl.when(jnp.logical_not(all_simple))
    def _():
        lower = jnp.where(lax.broadcasted_iota(I32, (TQ, TQ), 1)
                          <= lax.broadcasted_iota(I32, (TQ, TQ), 0), 1.0, 0.0).astype(BF16)

        def body(c, seens):
            cm = causal_mask(c)
            out = []
            for r in rows:
                key = key_ref[r, c]
                eq = key == thrs[r]
                eqf = jnp.where(eq, 1.0, 0.0)
                rank = seens[r] + _dot(lower, eqf.astype(BF16))
                sel = (key > thrs[r]) | (eq & (rank <= needs[r]))
                am_ref[r, c] = jnp.where(sel & cm, 0.0, -jnp.inf)
                out.append(seens[r] + jnp.sum(eqf, axis=0, keepdims=True))
            return tuple(out)
        lax.fori_loop(0, n_kc, body, tuple(jnp.zeros((1, TQ), F32) for _ in rows))

    def pass1(cs, ms):
        out = list(ms)
        for c in cs:
            for r in rows:
                lg = lg_ref[r, c] + jnp.concatenate([am_ref[r, c]] * H_B, axis=1)
                lg_ref[r, c] = lg
                out[r] = jnp.maximum(out[r], _fold_keys(lg, jnp.max))
        return tuple(out)

    ms = pair_loop(pass1, tuple(jnp.full((8, hq), -jnp.inf, F32) for _ in rows))
    mrows = [jnp.max(m, axis=0, keepdims=True) for m in ms]
    acc_ref[...] = jnp.zeros_like(acc_ref)

    def pass2(cs, ls):
        out = list(ls)
        for r in rows:
            ps = [jnp.exp(lg_ref[r, c] - mrows[r]) for c in cs]
            kvt = jnp.concatenate([kvt_ref[r, c] for c in cs], axis=1)
            acc_ref[r] += _dot(kvt, jnp.concatenate(ps, axis=0).astype(BF16))
            for p in ps:
                out[r] = out[r] + _fold_keys(p, jnp.sum)
        return tuple(out)

    ls = pair_loop(pass2, tuple(jnp.zeros((8, hq), F32) for _ in rows))
    for r in rows:
        ot = (acc_ref[r] / jnp.sum(ls[r], axis=0, keepdims=True)).astype(BF16)
        for h in range(H_B):
            hs = slice(h * DV_B, (h + 1) * DV_B)
            y = _dot_tn(ot[:, h * TQ:(h + 1) * TQ], wuv_ref[h])
            o_ref[r, :, hs] = y * _silu(zb_ref[r, :, hs])


def _dsa_attention(iq, ikw, qb, zb, ckv, g_kv, w_uv_bf16, bias_tiles, k_top, rb):
    bsz, seq, _ = iq.shape
    n_q = seq // TQ
    hq = H_B * TQ
    kern = functools.partial(_dsa_kernel, rb=rb, seq=seq, k_top=k_top)
    return pl.pallas_call(
        kern,
        grid=(bsz // rb, n_q),
        in_specs=[pl.BlockSpec((rb, TQ, H_IDX * D_IDX), lambda b, i: (b, i, 0)),
                  pl.BlockSpec((rb, TQ, LANES), lambda b, i: (b, i, 0)),
                  pl.BlockSpec((rb, TQ, H_B * R_KV), lambda b, i: (b, i, 0)),
                  pl.BlockSpec((rb, TQ, W_B), lambda b, i: (b, i, 0)),
                  pl.BlockSpec((rb, seq, LANES), lambda b, i: (b, 0, 0)),
                  pl.BlockSpec((rb, seq, R_KV), lambda b, i: (b, 0, 0)),
                  pl.BlockSpec((1, R_KV), lambda b, i: (0, 0)),
                  pl.BlockSpec((H_B, R_KV, DV_B), lambda b, i: (0, 0, 0)),
                  pl.BlockSpec((n_q, TQ, hq), lambda b, i: (0, 0, 0))],
        out_specs=pl.BlockSpec((rb, TQ, W_B), lambda b, i: (b, i, 0)),
        out_shape=jax.ShapeDtypeStruct((bsz, seq, W_B), F32),
        scratch_shapes=[pltpu.VMEM((rb, seq, R_KV), BF16),
                        pltpu.VMEM((rb, n_q, R_KV, TQ), BF16),
                        pltpu.VMEM((rb, seq, LANES), BF16),
                        pltpu.VMEM((rb, seq, LANES), BF16),
                        pltpu.VMEM((rb, n_q + 1, TQ, TQ), I32),
                        pltpu.VMEM((rb, n_q, TQ, TQ), F32),
                        pltpu.VMEM((rb, n_q, TQ, hq), F32),
                        pltpu.VMEM((rb, R_KV, hq), F32)],
        compiler_params=pltpu.CompilerParams(dimension_semantics=("arbitrary", "arbitrary"),
                                             vmem_limit_bytes=VMEM_LIMIT),
        name="dsa",
    )(iq, ikw, qb, zb, ikw, ckv, g_kv.reshape(1, R_KV), w_uv_bf16, bias_tiles)


def _out_kernel(x_ref, oa_ref, ob_ref, mod_ref, g_ref, w_ref, o_ref):
    mix_in = jnp.concatenate([oa_ref[0], ob_ref[0]], axis=-1).astype(BF16)
    mix = _dot(mix_in, w_ref[...])
    ms = jnp.mean(mix * mix, axis=-1, keepdims=True)
    normed = (mix * lax.rsqrt(ms + EPS)) * g_ref[...]
    o_ref[0] = x_ref[0] + mod_ref[0, 2:3, :] * normed


def _output(x, o_a, o_b, mod3, g_post, w_out_bf16, tm):
    bsz, seq, d = x.shape
    return pl.pallas_call(
        _out_kernel,
        grid=(bsz, seq // tm),
        in_specs=[pl.BlockSpec((1, tm, d), lambda b, i: (b, i, 0)),
                  pl.BlockSpec((1, tm, W_A), lambda b, i: (b, i, 0)),
                  pl.BlockSpec((1, tm, W_B), lambda b, i: (b, i, 0)),
                  pl.BlockSpec((1, 3, d), lambda b, i: (b, 0, 0)),
                  pl.BlockSpec((1, d), lambda b, i: (0, 0)),
                  pl.BlockSpec((W_A + W_B, d), lambda b, i: (0, 0))],
        out_specs=pl.BlockSpec((1, tm, d), lambda b, i: (b, i, 0)),
        out_shape=jax.ShapeDtypeStruct((bsz, seq, d), F32),
        compiler_params=pltpu.CompilerParams(dimension_semantics=("arbitrary", "arbitrary"),
                                             vmem_limit_bytes=VMEM_LIMIT),
        name="outproj",
    )(x, o_a, o_b, mod3, g_post.reshape(1, d), w_out_bf16)


def _pad_lanes(w):
    return jnp.pad(w, ((0, 0), (0, LANES - w.shape[1])))


def _pad_input_projection(w_in):
    o_ba = 3 * W_A + W_A
    o_qb = o_ba + 2 * H_A
    o_ik = o_qb + H_B * R_KV + R_KV + W_B + H_IDX * D_IDX
    return jnp.concatenate([w_in[:, :o_ba], _pad_lanes(w_in[:, o_ba:o_qb]),
                            w_in[:, o_qb:o_ik], _pad_lanes(w_in[:, o_ik:])], axis=1)


def _head_lanes(v):
    return jnp.zeros((1, LANES), F32).at[0, H_A:2 * H_A].set(v.astype(F32))


def kernel(x, c, w_ada, b_ada, g_pre, w_in, conv_w, a_log, dt_bias, g_gdn, g_kv, w_uv, rel_bias, w_out, g_post):
    bsz, seq, d = x.shape
    depth = w_ada.shape[0]
    assert seq % TQ == 0 and seq % CHUNK == 0
    k_top = min(TOPK_MAX, seq // 4)
    tm = min(512, seq)
    tl = min(256, seq)
    rb = 4 if bsz % 4 == 0 else (2 if bsz % 2 == 0 else 1)
    bias_tiles = _bias_tiles(rel_bias, seq // TQ)
    for layer in range(depth):
        mod3 = _modulation(c, w_ada[layer], b_ada[layer]).reshape(bsz, 3, d)
        w_pad = _pad_input_projection(w_in[layer]).astype(BF16)
        qkv, za, ba, qb, ckv, zb, iq, ikw = _projection(x, mod3, g_pre[layer], w_pad,
                                                        conv_w[layer], tm)
        o_a = _gated_deltanet(qkv, za, ba, _head_lanes(a_log[layer]),
                              _head_lanes(dt_bias[layer]), g_gdn[layer], rb, tl, 4)
        o_b = _dsa_attention(iq, ikw, qb, zb, ckv, g_kv[layer], w_uv[layer].astype(BF16),
                             bias_tiles, k_top, 2 if bsz % 2 == 0 else 1)
        x = _output(x, o_a, o_b, mod3, g_post[layer], w_out[layer].astype(BF16), tm)
    return x
```

```python
import functools
import math

import jax
import jax.numpy as jnp
from jax import lax
from jax.experimental import pallas as pl
from jax.experimental.pallas import tpu as pltpu

F32 = jnp.float32
BF16 = jnp.bfloat16
I32 = jnp.int32
HIGHEST = lax.Precision.HIGHEST

H_A, DK_A, DV_A = 4, 128, 128
W_A = H_A * DV_A
CONV_W = 4
CHUNK = 64
H_B, R_KV, DV_B = 4, 128, 128
W_B = H_B * DV_B
H_IDX, D_IDX = 8, 64
TOPK_MAX = 256
N_BUCKETS, MAX_EXACT, MAX_DIST = 32, 16, 128
EPS = 1e-6
LOG2E = math.log2(math.e)

LANES = 128
TQ = 128
VMEM_LIMIT = 52 * 1024 * 1024

C_QKV = (0, 3 * W_A)
C_ZA = (C_QKV[1], C_QKV[1] + W_A)
C_BA = (C_ZA[1], C_ZA[1] + LANES)
C_QB = (C_BA[1], C_BA[1] + H_B * R_KV)
C_CKV = (C_QB[1], C_QB[1] + R_KV)
C_ZB = (C_CKV[1], C_CKV[1] + W_B)
C_IQ = (C_ZB[1], C_ZB[1] + H_IDX * D_IDX)
C_IKW = (C_IQ[1], C_IQ[1] + LANES)
D_IN_PAD = C_IKW[1]
IW_LANE = D_IDX

NEG_INF_KEY = -2139095041


def _sigmoid(x):
    return 1.0 / (1.0 + jnp.exp(-x))


def _silu(x):
    return x * _sigmoid(x)


def _softplus(x):
    return jnp.maximum(x, 0.0) + jnp.log1p(jnp.exp(-jnp.abs(x)))


def _dot(a, b, precision=None):
    return jnp.dot(a, b, precision=precision, preferred_element_type=F32)


def _dot_nt(a, b, precision=None):
    return lax.dot_general(a, b, (((1,), (1,)), ((), ())), precision=precision,
                           preferred_element_type=F32)


def _dot_tn(a, b, precision=None):
    return lax.dot_general(a, b, (((0,), (0,)), ((), ())), precision=precision,
                           preferred_element_type=F32)


def _bdot(a, b, precision=None):
    return _dot(a.astype(BF16), b.astype(BF16))


def _bdot_nt(a, b, precision=None):
    return _dot_nt(a.astype(BF16), b.astype(BF16))


def _bdot_tn(a, b, precision=None):
    return _dot_tn(a.astype(BF16), b.astype(BF16))


def _mod_kernel(c_ref, w_ref, b_ref, o_ref):
    c = c_ref[...]
    o_ref[...] = _dot(_silu(c), w_ref[...], HIGHEST) + b_ref[...]


def _modulation(c, w_ada, b_ada):
    bsz, d = c.shape
    n = w_ada.shape[1]
    tn = 512
    return pl.pallas_call(
        _mod_kernel,
        grid=(n // tn,),
        in_specs=[pl.BlockSpec((bsz, d), lambda j: (0, 0)),
                  pl.BlockSpec((d, tn), lambda j: (0, j)),
                  pl.BlockSpec((1, tn), lambda j: (0, j))],
        out_specs=pl.BlockSpec((bsz, tn), lambda j: (0, j)),
        out_shape=jax.ShapeDtypeStruct((bsz, n), F32),
        compiler_params=pltpu.CompilerParams(dimension_semantics=("arbitrary",),
                                             vmem_limit_bytes=VMEM_LIMIT),
        name="mod",
    )(c, w_ada, b_ada.reshape(1, n))


def _proj_kernel(x_ref, mod_ref, g_ref, w_ref,
                 qkv_ref, za_ref, ba_ref, qb_ref, ckv_ref, zb_ref, iq_ref, ikw_ref):
    x = x_ref[0]
    ms = jnp.mean(x * x, axis=-1, keepdims=True)
    xn = x * lax.rsqrt(ms + EPS)
    shift = mod_ref[0, 0:1, :]
    scale = mod_ref[0, 1:2, :]
    h = (xn * g_ref[...]) * (1.0 + scale) + shift
    hb = h.astype(BF16)

    def mm(cols):
        return _dot(hb, w_ref[:, cols[0]:cols[1]])

    qkv_ref[0] = mm(C_QKV)
    za_ref[0] = mm(C_ZA)
    ba_ref[0] = mm(C_BA)
    qb_ref[0] = mm(C_QB).astype(BF16)
    ckv_ref[0] = mm(C_CKV)
    zb_ref[0] = mm(C_ZB)
    iq_ref[0] = mm(C_IQ).astype(BF16)
    ikw_ref[0] = mm(C_IKW)


def _projection(x, mod3, g_pre, w_pad, tm):
    bsz, seq, d = x.shape
    widths = [(C_QKV, F32), (C_ZA, F32), (C_BA, F32), (C_QB, BF16),
              (C_CKV, F32), (C_ZB, F32), (C_IQ, BF16), (C_IKW, F32)]
    out_shape = [jax.ShapeDtypeStruct((bsz, seq, c[1] - c[0]), dt) for c, dt in widths]
    out_specs = [pl.BlockSpec((1, tm, c[1] - c[0]), lambda b, i: (b, i, 0)) for c, _ in widths]
    return pl.pallas_call(
        _proj_kernel,
        grid=(bsz, seq // tm),
        in_specs=[pl.BlockSpec((1, tm, d), lambda b, i: (b, i, 0)),
                  pl.BlockSpec((1, 3, d), lambda b, i: (b, 0, 0)),
                  pl.BlockSpec((1, d), lambda b, i: (0, 0)),
                  pl.BlockSpec((d, D_IN_PAD), lambda b, i: (0, 0))],
        out_specs=out_specs,
        out_shape=out_shape,
        compiler_params=pltpu.CompilerParams(dimension_semantics=("arbitrary", "arbitrary"),
                                             vmem_limit_bytes=VMEM_LIMIT),
        name="proj",
    )(x, mod3, g_pre.reshape(1, d), w_pad)


def _tri_inverse_many(lmats, level_masks, eye):
    ts = [eye - jnp.where(level_masks[0], lm, 0.0) for lm in lmats]
    for m in level_masks[1:]:
        xs = [_bdot(jnp.where(m, lm, 0.0), t) for lm, t in zip(lmats, ts)]
        ts = [t - _bdot(t, x) for t, x in zip(ts, xs)]
    return ts


def _gdn_kernel(qkv_ref, za_ref, ba_ref, cw_ref, alog_ref, dtb_ref, gn_ref, o_ref,
                ext_ref, s_ref, gc_ref, beta_ref, gct_ref, u_ref, wq_ref, kg_ref, at_ref,
                *, rb, tl, group_a):
    n_ch = tl // CHUNK
    li = pl.program_id(1)

    @pl.when(li == 0)
    def _():
        ext_ref[:, 0:8, :] = jnp.zeros((rb, 8, 3 * W_A), F32)
        s_ref[...] = jnp.zeros_like(s_ref)

    @pl.when(li > 0)
    def _():
        ext_ref[:, 0:8, :] = ext_ref[:, tl:tl + 8, :]

    ext_ref[:, 8:8 + tl, :] = qkv_ref[...]

    r_i = lax.broadcasted_iota(I32, (tl, tl), 0)
    c_i = lax.broadcasted_iota(I32, (tl, tl), 1)
    shift = CHUNK.bit_length() - 1
    tri = jnp.where((c_i <= r_i) & ((r_i >> shift) == (c_i >> shift)), 1.0, 0.0)
    for r in range(rb):
        ba = ba_ref[r]
        beta_ref[r] = _sigmoid(ba)
        g = -jnp.exp(alog_ref[...]) * _softplus(ba + dtb_ref[...])
        gc = _dot(tri, g, HIGHEST)
        gc_ref[r] = gc
        gct = gc.T
        for c in range(n_ch):
            gct_ref[r * n_ch + c] = gct[:, c * CHUNK:(c + 1) * CHUNK]

    row = lax.broadcasted_iota(I32, (CHUNK, CHUNK), 0)
    col = lax.broadcasted_iota(I32, (CHUNK, CHUNK), 1)
    causal = row >= col
    strict = row > col
    eye = jnp.where(row == col, 1.0, 0.0)
    level_masks = []
    s = 1
    while s < CHUNK:
        ls = s.bit_length() - 1
        level_masks.append(((row >> (ls + 1)) == (col >> (ls + 1)))
                           & (((row >> ls) & 1) == 1) & (((col >> ls) & 1) == 0))
        s *= 2

    def phase_a(ig, carry):
        probs = []
        for j in range(group_a):
            it = ig * group_a + j
            r = it // n_ch
            c = it - r * n_ch
            base = pl.multiple_of(c * CHUNK, CHUNK)
            gc_c = gc_ref[r, pl.ds(base, CHUNK), :]
            beta_c = beta_ref[r, pl.ds(base, CHUNK), :]
            gct_c = gct_ref[it]
            for h in range(H_A):
                def conv_silu(sec):
                    c0 = sec * W_A + h * DK_A
                    win = ext_ref[r, pl.ds(base, CHUNK + 8), c0:c0 + DK_A]
                    acc = None
                    for t in range(CONV_W):
                        sh = CONV_W - 1 - t
                        u = win if sh == 0 else pltpu.roll(win, sh, axis=0)
                        term = u[8:8 + CHUNK] * cw_ref[t:t + 1, c0:c0 + DK_A]
                        acc = term if acc is None else acc + term
                    return _silu(acc)

                q = conv_silu(0)
                k = conv_silu(1)
                v = conv_silu(2)
                q = q * lax.rsqrt(jnp.sum(q * q, axis=-1, keepdims=True) + EPS) * (DK_A ** -0.5)
                k = k * lax.rsqrt(jnp.sum(k * k, axis=-1, keepdims=True) + EPS)
                beta = beta_c[:, h:h + 1]
                gcol = gc_c[:, H_A + h:H_A + h + 1]
                grow = gct_c[H_A + h:H_A + h + 1, :]
                glast = gc_c[CHUNK - 1:CHUNK, H_A + h:H_A + h + 1]
                decay = jnp.exp(jnp.where(causal, gcol - grow, -jnp.inf))
                ecol = jnp.exp(gcol)
                kb = k * beta
                hc = slice(h * DV_A, (h + 1) * DV_A)
                wq_ref[it, h, CHUNK:2 * CHUNK, :] = (q * ecol).astype(BF16)
                kg_ref[r, pl.ds(base, CHUNK), hc] = (k * jnp.exp(glast - gcol)).astype(BF16)
                a2 = _bdot_nt(jnp.concatenate([kb, q], axis=0), k)
                at_ref[it, h] = (a2[CHUNK:] * decay).astype(BF16)
                probs.append(dict(
                    r=r, it=it, h=h, base=base, hc=hc,
                    lmat=jnp.where(strict, a2[:CHUNK] * decay, 0.0),
                    rhs=jnp.concatenate([v * beta, kb * ecol], axis=1).astype(BF16)))
        tmats = _tri_inverse_many([p["lmat"] for p in probs], level_masks, eye)
        for p, tmat in zip(probs, tmats):
            uw = _dot(tmat.astype(BF16), p["rhs"])
            u_ref[p["r"], pl.ds(p["base"], CHUNK), p["hc"]] = uw[:, :DV_A]
            wq_ref[p["it"], p["h"], 0:CHUNK, :] = uw[:, DV_A:].astype(BF16)
        return carry

    lax.fori_loop(0, rb * n_ch // group_a, phase_a, 0)

    def phase_b(c, carry):
        base = pl.multiple_of(c * CHUNK, CHUNK)
        chains = [(r, h) for r in range(rb) for h in range(H_A)]
        hcs = [slice(h * DV_A, (h + 1) * DV_A) for _, h in chains]
        sts = [s_ref[r, h] for r, h in chains]
        wss = [_dot(wq_ref[r * n_ch + c, h], st.astype(BF16)) for (r, h), st in zip(chains, sts)]
        vnbs = [(u_ref[r, pl.ds(base, CHUNK), hc] - ws[:CHUNK]).astype(BF16)
                for (r, h), hc, ws in zip(chains, hcs, wss)]
        upds = [_dot_tn(kg_ref[r, pl.ds(base, CHUNK), hc], vnb)
                for (r, h), hc, vnb in zip(chains, hcs, vnbs)]
        for (r, h), st, upd in zip(chains, sts, upds):
            glast = gc_ref[r, pl.ds(base + CHUNK - 1, 1), :][:, H_A + h:H_A + h + 1]
            s_ref[r, h] = st * jnp.exp(glast) + upd
        for (r, h), hc, ws, vnb in zip(chains, hcs, wss, vnbs):
            o = ws[CHUNK:] + _dot(at_ref[r * n_ch + c, h], vnb)
            on = o * lax.rsqrt(jnp.mean(o * o, axis=-1, keepdims=True) + EPS) * gn_ref[...]
            z = za_ref[r, pl.ds(base, CHUNK), hc]
            o_ref[r, pl.ds(base, CHUNK), hc] = on * _silu(z)
        return carry

    lax.fori_loop(0, n_ch, phase_b, 0)


def _gated_deltanet(qkv, za, ba, conv_w, alog_vec, dtb_vec, g_norm, rb, tl, group_a):
    bsz, seq, _ = qkv.shape
    n_ch = tl // CHUNK
    kern = functools.partial(_gdn_kernel, rb=rb, tl=tl, group_a=group_a)
    return pl.pallas_call(
        kern,
        grid=(bsz // rb, seq // tl),
        in_specs=[pl.BlockSpec((rb, tl, 3 * W_A), lambda b, i: (b, i, 0)),
                  pl.BlockSpec((rb, tl, W_A), lambda b, i: (b, i, 0)),
                  pl.BlockSpec((rb, tl, LANES), lambda b, i: (b, i, 0)),
                  pl.BlockSpec((CONV_W, 3 * W_A), lambda b, i: (0, 0)),
                  pl.BlockSpec((1, LANES), lambda b, i: (0, 0)),
                  pl.BlockSpec((1, LANES), lambda b, i: (0, 0)),
                  pl.BlockSpec((1, DV_A), lambda b, i: (0, 0))],
        out_specs=pl.BlockSpec((rb, tl, W_A), lambda b, i: (b, i, 0)),
        out_shape=jax.ShapeDtypeStruct((bsz, seq, W_A), F32),
        scratch_shapes=[pltpu.VMEM((rb, tl + 8, 3 * W_A), F32),
                        pltpu.VMEM((rb, H_A, DK_A, DV_A), F32),
                        pltpu.VMEM((rb, tl, LANES), F32),
                        pltpu.VMEM((rb, tl, LANES), F32),
                        pltpu.VMEM((rb * n_ch, LANES, CHUNK), F32),
                        pltpu.VMEM((rb, tl, W_A), F32),
                        pltpu.VMEM((rb * n_ch, H_A, 2 * CHUNK, DK_A), BF16),
                        pltpu.VMEM((rb, tl, W_A), BF16),
                        pltpu.VMEM((rb * n_ch, H_A, CHUNK, CHUNK), BF16)],
        compiler_params=pltpu.CompilerParams(dimension_semantics=("arbitrary", "arbitrary"),
                                             vmem_limit_bytes=VMEM_LIMIT),
        name="gdn",
    )(qkv, za, ba, conv_w, alog_vec, dtb_vec, g_norm.reshape(1, DV_A))


def _bias_kernel(rb_ref, o_ref):
    d = pl.program_id(0)
    kj = lax.broadcasted_iota(I32, (TQ, TQ), 0)
    qi = lax.broadcasted_iota(I32, (TQ, TQ), 1)
    dist = d * TQ + qi - kj
    n = jnp.maximum(dist, 0)
    nf = jnp.maximum(n, 1).astype(F32)
    large = MAX_EXACT + (jnp.log(nf / MAX_EXACT) / math.log(MAX_DIST / MAX_EXACT)
                         * (N_BUCKETS - MAX_EXACT)).astype(I32)
    large = jnp.minimum(large, N_BUCKETS - 1)
    bucket = jnp.where(n < MAX_EXACT, n, large)
    for h in range(H_B):
        acc = jnp.zeros((TQ, TQ), F32)
        for kb in range(N_BUCKETS):
            acc = jnp.where(bucket == kb, rb_ref[kb, h], acc)
        o_ref[0, :, h * TQ:(h + 1) * TQ] = acc * LOG2E


def _bias_tiles(rel_bias, n_diag):
    return pl.pallas_call(
        _bias_kernel,
        grid=(n_diag,),
        in_specs=[pl.BlockSpec(memory_space=pltpu.SMEM)],
        out_specs=pl.BlockSpec((1, TQ, H_B * TQ), lambda d: (d, 0, 0)),
        out_shape=jax.ShapeDtypeStruct((n_diag, TQ, H_B * TQ), F32),
        compiler_params=pltpu.CompilerParams(dimension_semantics=("arbitrary",)),
        name="bias",
    )(rel_bias)


def _fold_keys(x, op):
    x = op(x.reshape(4, TQ // 32, 8, x.shape[-1]), axis=1)
    return op(x, axis=0)


def _dsa_kernel(iq_ref, ikwq_ref, qb_ref, zb_ref, ikw_ref, ckv_ref, gkv_ref, wuv_ref, bias_ref,
                o_ref,
                kvn_ref, kvt_ref, iklo_ref, ikhi_ref, key_ref, am_ref, lg_ref, acc_ref,
                *, rb, seq, k_top):
    qi = pl.program_id(1)
    n_kc = qi + 1
    hq = H_B * TQ
    rows = range(rb)

    @pl.when(qi == 0)
    def _():
        for r in rows:
            for c in range(seq // TQ):
                sl = slice(c * TQ, (c + 1) * TQ)
                ckv = ckv_ref[r, sl, :]
                ms = jnp.mean(ckv * ckv, axis=-1, keepdims=True)
                kvn = (ckv * lax.rsqrt(ms + EPS)) * gkv_ref[...]
                kvn_ref[r, sl, :] = kvn.astype(BF16)
                kvt_ref[r, c] = kvn.T.astype(BF16)
                ikw = ikw_ref[r, sl, :]
                lane = lax.broadcasted_iota(I32, ikw.shape, 1)
                lo = jnp.where(lane < D_IDX, ikw, 0.0)
                iklo_ref[r, sl, :] = lo.astype(BF16)
                ikhi_ref[r, sl, :] = pltpu.roll(lo, D_IDX, axis=1).astype(BF16)

    iq4s, iwts = [], []
    for r in rows:
        iq = iq_ref[r]
        iq4s.append(jnp.concatenate(
            [iq[:, p * LANES:(p + 1) * LANES] for p in range(H_IDX // 2)], axis=0))
        iwts.append((ikwq_ref[r] * (H_IDX ** -0.5 * D_IDX ** -0.5)).T)

    key_j = lax.broadcasted_iota(I32, (TQ, TQ), 0)
    qry_t = qi * TQ + lax.broadcasted_iota(I32, (TQ, TQ), 1)

    def causal_mask(c):
        return (c * TQ + key_j) <= qry_t

    q4s = []
    for r in rows:
        qb = qb_ref[r]
        q4s.append(jnp.concatenate([qb[:, h * R_KV:(h + 1) * R_KV] for h in range(H_B)], axis=0))
    scale = R_KV ** -0.5 * LOG2E

    def pair_loop(body, carry):
        carry = lax.fori_loop(0, n_kc >> 1, lambda i, cr: body((2 * i, 2 * i + 1), cr), carry)
        return lax.cond((n_kc & 1) == 1, lambda cr: body((n_kc - 1,), cr), lambda cr: cr, carry)

    def score_chunks(cs, m0s):
        m0s = list(m0s)
        nc = len(cs)
        ks = pl.ds(pl.multiple_of(cs[0] * TQ, TQ), nc * TQ)
        rels = [_dot_nt(jnp.concatenate([iklo_ref[r, ks, :], ikhi_ref[r, ks, :]], axis=0),
                        iq4s[r]) for r in rows]
        sts = [_dot_nt(kvn_ref[r, ks, :], q4s[r]) for r in rows]
        for j, c in enumerate(cs):
            for r in rows:
                re = rels[r][j * TQ:(j + 1) * TQ]
                ro = rels[r][(nc + j) * TQ:(nc + j + 1) * TQ]
                s = None
                for p in range(H_IDX // 2):
                    ps = slice(p * TQ, (p + 1) * TQ)
                    we = iwts[r][IW_LANE + 2 * p:IW_LANE + 2 * p + 1, :]
                    wo = iwts[r][IW_LANE + 2 * p + 1:IW_LANE + 2 * p + 2, :]
                    t = jnp.maximum(re[:, ps], 0.0) * we + jnp.maximum(ro[:, ps], 0.0) * wo
                    s = t if s is None else s + t
                s = jnp.where(s == 0.0, 0.0, s)
                s = jnp.where(causal_mask(c), s, -jnp.inf)
                bits = pltpu.bitcast(s, I32)
                key_ref[r, c] = jnp.where(bits < 0, bits ^ 0x7FFFFFFF, bits)
                lg = sts[r][j * TQ:(j + 1) * TQ] * scale + bias_ref[qi - c]
                lg_ref[r, c] = lg
                m0s[r] = jnp.maximum(m0s[r], _fold_keys(lg, jnp.max))
        return tuple(m0s)

    m0s = pair_loop(score_chunks, tuple(jnp.full((8, hq), -jnp.inf, F32) for _ in rows))

    kf = float(k_top)
    n_pairs = (n_kc + 1) >> 1
    sign = jnp.int32(-2 ** 31)

    @pl.when((n_kc & 1) == 1)
    def _():
        for r in rows:
            key_ref[r, n_kc] = jnp.full((TQ, TQ), -2 ** 31, I32)

    def count_ge(cands):
        def body(i, accs):
            out = list(accs)
            for c in (2 * i, 2 * i + 1):
                for r in rows:
                    hit = jnp.where(key_ref[r, c] >= cands[r], 1.0, 0.0)
                    out[r] = out[r] + _fold_keys(hit, jnp.sum)
            return tuple(out)
        accs = lax.fori_loop(0, n_pairs, body, tuple(jnp.zeros((8, TQ), F32) for _ in rows))
        return [jnp.sum(a, axis=0, keepdims=True) for a in accs]

    def bit_body(i, tus):
        bit = lax.shift_left(jnp.int32(1), 31 - i)
        cands = [tu | bit for tu in tus]
        cnts = count_ge([(cu ^ sign)[0:1, :] for cu in cands])
        return tuple(jnp.where(cnt >= kf, cu, tu) for cnt, cu, tu in zip(cnts, cands, tus))

    tus = lax.fori_loop(0, 32, bit_body, tuple(jnp.zeros((8, TQ), I32) for _ in rows))
    thrs = [(tu ^ sign)[0:1, :] for tu in tus]

    def count_both(c, carry):
        out = []
        for r in rows:
            cg, ce = carry[r]
            key = key_ref[r, c]
            out.append((cg + _fold_keys(jnp.where(key > thrs[r], 1.0, 0.0), jnp.sum),
                        ce + _fold_keys(jnp.where(key == thrs[r], 1.0, 0.0), jnp.sum)))
        return tuple(out)

    zero8 = jnp.zeros((8, TQ), F32)
    cges = lax.fori_loop(0, n_kc, count_both, tuple((zero8, zero8) for _ in rows))
    qrow = qi * TQ + lax.broadcasted_iota(I32, (1, TQ), 1)
    needs, simple_all = [], None
    for r in rows:
        cnt_gt = jnp.sum(cges[r][0], axis=0, keepdims=True)
        cnt_eq = jnp.sum(cges[r][1], axis=0, keepdims=True)
        need = kf - cnt_gt
        needs.append(need)
        simple = (cnt_eq <= need) | ((thrs[r] == NEG_INF_KEY) & (qrow < k_top))
        simple_all = simple if simple_all is None else (simple_all & simple)
    all_simple = jnp.min(jnp.where(simple_all, 1.0, 0.0)) > 0.5

    @pl.when(all_simple)
    def _():
        def body(c, carry):
            cm = causal_mask(c)
            for r in rows:
                sel = (key_ref[r, c] >= thrs[r]) & cm
                am_ref[r, c] = jnp.where(sel, 0.0, -jnp.inf)
            return carry
        lax.fori_loop(0, n_kc, body, 0)

    @pl.when(jnp.logical_not(all_simple))
    def _():
        lower = jnp.where(lax.broadcasted_iota(I32, (TQ, TQ), 1)
                          <= lax.broadcasted_iota(I32, (TQ, TQ), 0), 1.0, 0.0).astype(BF16)

        def body(c, seens):
            cm = causal_mask(c)
            out = []
            for r in rows:
                key = key_ref[r, c]
                eq = key == thrs[r]
                eqf = jnp.where(eq, 1.0, 0.0)
                rank = seens[r] + _dot(lower, eqf.astype(BF16))
                sel = (key > thrs[r]) | (eq & (rank <= needs[r]))
                am_ref[r, c] = jnp.where(sel & cm, 0.0, -jnp.inf)
                out.append(seens[r] + jnp.sum(eqf, axis=0, keepdims=True))
            return tuple(out)
        lax.fori_loop(0, n_kc, body, tuple(jnp.zeros((1, TQ), F32) for _ in rows))

    def masked_logits(r, c):
        return lg_ref[r, c] + jnp.concatenate([am_ref[r, c]] * H_B, axis=1)

    def softmax_pv(shifts):
        acc_ref[...] = jnp.zeros_like(acc_ref)

        def body(cs, ls):
            out = list(ls)
            for r in rows:
                ps = [jnp.exp2(masked_logits(r, c) - shifts[r]) for c in cs]
                kvt = jnp.concatenate([kvt_ref[r, c] for c in cs], axis=1)
                acc_ref[r] += _dot(kvt, jnp.concatenate(ps, axis=0).astype(BF16))
                for p in ps:
                    out[r] = out[r] + _fold_keys(p, jnp.sum)
            return tuple(out)

        ls = pair_loop(body, tuple(jnp.zeros((8, hq), F32) for _ in rows))
        return tuple(jnp.sum(l, axis=0, keepdims=True) for l in ls)

    lsums = softmax_pv([jnp.max(m, axis=0, keepdims=True) for m in m0s])
    lmin = jnp.min(jnp.concatenate(lsums, axis=1))

    def exact_shift(_):
        def body(cs, ms):
            out = list(ms)
            for c in cs:
                for r in rows:
                    out[r] = jnp.maximum(out[r], _fold_keys(masked_logits(r, c), jnp.max))
            return tuple(out)
        ms = pair_loop(body, tuple(jnp.full((8, hq), -jnp.inf, F32) for _ in rows))
        return softmax_pv([jnp.max(m, axis=0, keepdims=True) for m in ms])

    lsums = lax.cond(lmin >= 2.0 ** -40, lambda _: lsums, exact_shift, 0)
    for r in rows:
        ot = (acc_ref[r] / lsums[r]).astype(BF16)
        for h in range(H_B):
            hs = slice(h * DV_B, (h + 1) * DV_B)
            y = _dot_tn(ot[:, h * TQ:(h + 1) * TQ], wuv_ref[h])
            o_ref[r, :, hs] = y * _silu(zb_ref[r, :, hs])


def _dsa_attention(iq, ikw, qb, zb, ckv, g_kv, w_uv_bf16, bias_tiles, k_top, rb):
    bsz, seq, _ = iq.shape
    n_q = seq // TQ
    hq = H_B * TQ
    kern = functools.partial(_dsa_kernel, rb=rb, seq=seq, k_top=k_top)
    return pl.pallas_call(
        kern,
        grid=(bsz // rb, n_q),
        in_specs=[pl.BlockSpec((rb, TQ, H_IDX * D_IDX), lambda b, i: (b, i, 0)),
                  pl.BlockSpec((rb, TQ, LANES), lambda b, i: (b, i, 0)),
                  pl.BlockSpec((rb, TQ, H_B * R_KV), lambda b, i: (b, i, 0)),
                  pl.BlockSpec((rb, TQ, W_B), lambda b, i: (b, i, 0)),
                  pl.BlockSpec((rb, seq, LANES), lambda b, i: (b, 0, 0)),
                  pl.BlockSpec((rb, seq, R_KV), lambda b, i: (b, 0, 0)),
                  pl.BlockSpec((1, R_KV), lambda b, i: (0, 0)),
                  pl.BlockSpec((H_B, R_KV, DV_B), lambda b, i: (0, 0, 0)),
                  pl.BlockSpec((n_q, TQ, hq), lambda b, i: (0, 0, 0))],
        out_specs=pl.BlockSpec((rb, TQ, W_B), lambda b, i: (b, i, 0)),
        out_shape=jax.ShapeDtypeStruct((bsz, seq, W_B), F32),
        scratch_shapes=[pltpu.VMEM((rb, seq, R_KV), BF16),
                        pltpu.VMEM((rb, n_q, R_KV, TQ), BF16),
                        pltpu.VMEM((rb, seq, LANES), BF16),
                        pltpu.VMEM((rb, seq, LANES), BF16),
                        pltpu.VMEM((rb, n_q + 1, TQ, TQ), I32),
                        pltpu.VMEM((rb, n_q, TQ, TQ), F32),
                        pltpu.VMEM((rb, n_q, TQ, hq), F32),
                        pltpu.VMEM((rb, R_KV, hq), F32)],
        compiler_params=pltpu.CompilerParams(dimension_semantics=("arbitrary", "arbitrary"),
                                             vmem_limit_bytes=VMEM_LIMIT),
        name="dsa",
    )(iq, ikw, qb, zb, ikw, ckv, g_kv.reshape(1, R_KV), w_uv_bf16, bias_tiles)


def _out_kernel(x_ref, oa_ref, ob_ref, mod_ref, g_ref, w_ref, o_ref):
    mix_in = jnp.concatenate([oa_ref[0], ob_ref[0]], axis=-1).astype(BF16)
    mix = _dot(mix_in, w_ref[...])
    ms = jnp.mean(mix * mix, axis=-1, keepdims=True)
    normed = (mix * lax.rsqrt(ms + EPS)) * g_ref[...]
    o_ref[0] = x_ref[0] + mod_ref[0, 2:3, :] * normed


def _output(x, o_a, o_b, mod3, g_post, w_out_bf16, tm):
    bsz, seq, d = x.shape
    return pl.pallas_call(
        _out_kernel,
        grid=(bsz, seq // tm),
        in_specs=[pl.BlockSpec((1, tm, d), lambda b, i: (b, i, 0)),
                  pl.BlockSpec((1, tm, W_A), lambda b, i: (b, i, 0)),
                  pl.BlockSpec((1, tm, W_B), lambda b, i: (b, i, 0)),
                  pl.BlockSpec((1, 3, d), lambda b, i: (b, 0, 0)),
                  pl.BlockSpec((1, d), lambda b, i: (0, 0)),
                  pl.BlockSpec((W_A + W_B, d), lambda b, i: (0, 0))],
        out_specs=pl.BlockSpec((1, tm, d), lambda b, i: (b, i, 0)),
        out_shape=jax.ShapeDtypeStruct((bsz, seq, d), F32),
        compiler_params=pltpu.CompilerParams(dimension_semantics=("arbitrary", "arbitrary"),
                                             vmem_limit_bytes=VMEM_LIMIT),
        name="outproj",
    )(x, o_a, o_b, mod3, g_post.reshape(1, d), w_out_bf16)


def _pad_lanes(w):
    return jnp.pad(w, ((0, 0), (0, LANES - w.shape[1])))


def _pad_input_projection(w_in):
    o_ba = 3 * W_A + W_A
    o_qb = o_ba + 2 * H_A
    o_ik = o_qb + H_B * R_KV + R_KV + W_B + H_IDX * D_IDX
    return jnp.concatenate([w_in[:, :o_ba], _pad_lanes(w_in[:, o_ba:o_qb]),
                            w_in[:, o_qb:o_ik], _pad_lanes(w_in[:, o_ik:])], axis=1)


def _head_lanes(v):
    return jnp.zeros((1, LANES), F32).at[0, H_A:2 * H_A].set(v.astype(F32))


def kernel(x, c, w_ada, b_ada, g_pre, w_in, conv_w, a_log, dt_bias, g_gdn, g_kv, w_uv, rel_bias, w_out, g_post):
    bsz, seq, d = x.shape
    depth = w_ada.shape[0]
    assert seq % TQ == 0 and seq % CHUNK == 0
    k_top = min(TOPK_MAX, seq // 4)
    tm = min(512, seq)
    tl = min(256, seq)
    rb = 4 if bsz % 4 == 0 else (2 if bsz % 2 == 0 else 1)
    bias_tiles = _bias_tiles(rel_bias, seq // TQ)
    for layer in range(depth):
        mod3 = _modulation(c, w_ada[layer], b_ada[layer]).reshape(bsz, 3, d)
        w_pad = _pad_input_projection(w_in[layer]).astype(BF16)
        qkv, za, ba, qb, ckv, zb, iq, ikw = _projection(x, mod3, g_pre[layer], w_pad, tm)
        o_a = _gated_deltanet(qkv, za, ba, conv_w[layer], _head_lanes(a_log[layer]),
                              _head_lanes(dt_bias[layer]), g_gdn[layer], rb, tl, 4)
        o_b = _dsa_attention(iq, ikw, qb, zb, ckv, g_kv[layer], w_uv[layer].astype(BF16),
                             bias_tiles, k_top, 2 if bsz % 2 == 0 else 1)
        x = _output(x, o_a, o_b, mod3, g_post[layer], w_out[layer].astype(BF16), tm)
    return x
```

```python
import functools
import math

import jax
import jax.numpy as jnp
from jax import lax
from jax.experimental import pallas as pl
from jax.experimental.pallas import tpu as pltpu

F32 = jnp.float32
BF16 = jnp.bfloat16
I32 = jnp.int32
HIGHEST = lax.Precision.HIGHEST

H_A, DK_A, DV_A = 4, 128, 128
W_A = H_A * DV_A
CONV_W = 4
CHUNK = 64
H_B, R_KV, DV_B = 4, 128, 128
W_B = H_B * DV_B
H_IDX, D_IDX = 8, 64
TOPK_MAX = 256
N_BUCKETS, MAX_EXACT, MAX_DIST = 32, 16, 128
EPS = 1e-6
LOG2E = math.log2(math.e)

LANES = 128
TQ = 128
VMEM_LIMIT = 58 * 1024 * 1024

C_QKV = (0, 3 * W_A)
C_ZA = (C_QKV[1], C_QKV[1] + W_A)
C_BA = (C_ZA[1], C_ZA[1] + LANES)
C_QB = (C_BA[1], C_BA[1] + H_B * R_KV)
C_CKV = (C_QB[1], C_QB[1] + R_KV)
C_ZB = (C_CKV[1], C_CKV[1] + W_B)
C_IQ = (C_ZB[1], C_ZB[1] + H_IDX * D_IDX)
C_IKW = (C_IQ[1], C_IQ[1] + LANES)
D_IN_PAD = C_IKW[1]
IW_LANE = D_IDX

NEG_INF_KEY = -2139095041


def _sigmoid(x):
    return 1.0 / (1.0 + jnp.exp(-x))


def _silu(x):
    return x * _sigmoid(x)


def _softplus(x):
    return jnp.maximum(x, 0.0) + jnp.log1p(jnp.exp(-jnp.abs(x)))


def _dot(a, b, precision=None):
    return jnp.dot(a, b, precision=precision, preferred_element_type=F32)


def _dot_nt(a, b, precision=None):
    return lax.dot_general(a, b, (((1,), (1,)), ((), ())), precision=precision,
                           preferred_element_type=F32)


def _dot_tn(a, b, precision=None):
    return lax.dot_general(a, b, (((0,), (0,)), ((), ())), precision=precision,
                           preferred_element_type=F32)


def _bdot(a, b, precision=None):
    return _dot(a.astype(BF16), b.astype(BF16))


def _bdot_nt(a, b, precision=None):
    return _dot_nt(a.astype(BF16), b.astype(BF16))


def _bdot_tn(a, b, precision=None):
    return _dot_tn(a.astype(BF16), b.astype(BF16))


def _mod_kernel(c_ref, w_ref, b_ref, o_ref):
    c = c_ref[...]
    o_ref[...] = _dot(_silu(c), w_ref[...], HIGHEST) + b_ref[...]


def _modulation(c, w_ada, b_ada):
    bsz, d = c.shape
    n = w_ada.shape[1]
    tn = 512
    return pl.pallas_call(
        _mod_kernel,
        grid=(n // tn,),
        in_specs=[pl.BlockSpec((bsz, d), lambda j: (0, 0)),
                  pl.BlockSpec((d, tn), lambda j: (0, j)),
                  pl.BlockSpec((1, tn), lambda j: (0, j))],
        out_specs=pl.BlockSpec((bsz, tn), lambda j: (0, j)),
        out_shape=jax.ShapeDtypeStruct((bsz, n), F32),
        compiler_params=pltpu.CompilerParams(dimension_semantics=("arbitrary",),
                                             vmem_limit_bytes=VMEM_LIMIT),
        name="mod",
    )(c, w_ada, b_ada.reshape(1, n))


def _proj_kernel(x_ref, mod_ref, g_ref, w_ref,
                 qkv_ref, za_ref, ba_ref, qb_ref, ckv_ref, zb_ref, iq_ref, ikw_ref):
    x = x_ref[0]
    ms = jnp.mean(x * x, axis=-1, keepdims=True)
    xn = x * lax.rsqrt(ms + EPS)
    shift = mod_ref[0, 0:1, :]
    scale = mod_ref[0, 1:2, :]
    h = (xn * g_ref[...]) * (1.0 + scale) + shift
    hb = h.astype(BF16)

    def mm(cols):
        return _dot(hb, w_ref[:, cols[0]:cols[1]])

    qkv_ref[0] = mm(C_QKV)
    za_ref[0] = mm(C_ZA)
    ba_ref[0] = mm(C_BA)
    qb_ref[0] = mm(C_QB).astype(BF16)
    ckv_ref[0] = mm(C_CKV)
    zb_ref[0] = mm(C_ZB)
    iq_ref[0] = mm(C_IQ).astype(BF16)
    ikw_ref[0] = mm(C_IKW)


def _projection(x, mod3, g_pre, w_pad, tm):
    bsz, seq, d = x.shape
    widths = [(C_QKV, F32), (C_ZA, F32), (C_BA, F32), (C_QB, BF16),
              (C_CKV, F32), (C_ZB, F32), (C_IQ, BF16), (C_IKW, F32)]
    out_shape = [jax.ShapeDtypeStruct((bsz, seq, c[1] - c[0]), dt) for c, dt in widths]
    out_specs = [pl.BlockSpec((1, tm, c[1] - c[0]), lambda b, i: (b, i, 0)) for c, _ in widths]
    return pl.pallas_call(
        _proj_kernel,
        grid=(bsz, seq // tm),
        in_specs=[pl.BlockSpec((1, tm, d), lambda b, i: (b, i, 0)),
                  pl.BlockSpec((1, 3, d), lambda b, i: (b, 0, 0)),
                  pl.BlockSpec((1, d), lambda b, i: (0, 0)),
                  pl.BlockSpec((d, D_IN_PAD), lambda b, i: (0, 0))],
        out_specs=out_specs,
        out_shape=out_shape,
        compiler_params=pltpu.CompilerParams(dimension_semantics=("arbitrary", "arbitrary"),
                                             vmem_limit_bytes=VMEM_LIMIT),
        name="proj",
    )(x, mod3, g_pre.reshape(1, d), w_pad)


def _tri_inverse_many(lmats, level_masks, eye):
    ts = [eye - jnp.where(level_masks[0], lm, 0.0) for lm in lmats]
    for m in level_masks[1:]:
        xs = [_bdot(jnp.where(m, lm, 0.0), t) for lm, t in zip(lmats, ts)]
        ts = [t - _bdot(t, x) for t, x in zip(ts, xs)]
    return ts


def _gdn_kernel(qkv_ref, za_ref, ba_ref, cw_ref, alog_ref, dtb_ref, gn_ref, o_ref,
                ext_ref, s_ref, gc_ref, beta_ref, gct_ref, u_ref, wq_ref, kg_ref, at_ref,
                *, rb, tl, group_a):
    n_ch = tl // CHUNK
    li = pl.program_id(1)

    @pl.when(li == 0)
    def _():
        ext_ref[:, 0:8, :] = jnp.zeros((rb, 8, 3 * W_A), F32)
        s_ref[...] = jnp.zeros_like(s_ref)

    @pl.when(li > 0)
    def _():
        ext_ref[:, 0:8, :] = ext_ref[:, tl:tl + 8, :]

    ext_ref[:, 8:8 + tl, :] = qkv_ref[...]

    r_i = lax.broadcasted_iota(I32, (tl, tl), 0)
    c_i = lax.broadcasted_iota(I32, (tl, tl), 1)
    shift = CHUNK.bit_length() - 1
    tri = jnp.where((c_i <= r_i) & ((r_i >> shift) == (c_i >> shift)), 1.0, 0.0)
    for r in range(rb):
        ba = ba_ref[r]
        beta_ref[r] = _sigmoid(ba)
        g = -jnp.exp(alog_ref[...]) * _softplus(ba + dtb_ref[...])
        gc = _dot(tri, g, HIGHEST)
        gc_ref[r] = gc
        gct = gc.T
        for c in range(n_ch):
            gct_ref[r * n_ch + c] = gct[:, c * CHUNK:(c + 1) * CHUNK]

    row = lax.broadcasted_iota(I32, (CHUNK, CHUNK), 0)
    col = lax.broadcasted_iota(I32, (CHUNK, CHUNK), 1)
    causal = row >= col
    strict = row > col
    eye = jnp.where(row == col, 1.0, 0.0)
    level_masks = []
    s = 1
    while s < CHUNK:
        ls = s.bit_length() - 1
        level_masks.append(((row >> (ls + 1)) == (col >> (ls + 1)))
                           & (((row >> ls) & 1) == 1) & (((col >> ls) & 1) == 0))
        s *= 2

    def phase_a(ig, carry):
        probs = []
        for j in range(group_a):
            it = ig * group_a + j
            r = it // n_ch
            c = it - r * n_ch
            base = pl.multiple_of(c * CHUNK, CHUNK)
            gc_c = gc_ref[r, pl.ds(base, CHUNK), :]
            beta_c = beta_ref[r, pl.ds(base, CHUNK), :]
            gct_c = gct_ref[it]
            for h in range(H_A):
                def conv_silu(sec):
                    c0 = sec * W_A + h * DK_A
                    win = ext_ref[r, pl.ds(base, CHUNK + 8), c0:c0 + DK_A]
                    acc = None
                    for t in range(CONV_W):
                        sh = CONV_W - 1 - t
                        u = win if sh == 0 else pltpu.roll(win, sh, axis=0)
                        term = u[8:8 + CHUNK] * cw_ref[t:t + 1, c0:c0 + DK_A]
                        acc = term if acc is None else acc + term
                    return _silu(acc)

                q = conv_silu(0)
                k = conv_silu(1)
                v = conv_silu(2)
                q = q * lax.rsqrt(jnp.sum(q * q, axis=-1, keepdims=True) + EPS) * (DK_A ** -0.5)
                k = k * lax.rsqrt(jnp.sum(k * k, axis=-1, keepdims=True) + EPS)
                beta = beta_c[:, h:h + 1]
                gcol = gc_c[:, H_A + h:H_A + h + 1]
                grow = gct_c[H_A + h:H_A + h + 1, :]
                glast = gc_c[CHUNK - 1:CHUNK, H_A + h:H_A + h + 1]
                decay = jnp.exp(jnp.where(causal, gcol - grow, -jnp.inf))
                ecol = jnp.exp(gcol)
                kb = k * beta
                hc = slice(h * DV_A, (h + 1) * DV_A)
                wq_ref[it, h, CHUNK:2 * CHUNK, :] = (q * ecol).astype(BF16)
                kg_ref[r, pl.ds(base, CHUNK), hc] = (k * jnp.exp(glast - gcol)).astype(BF16)
                a2 = _bdot_nt(jnp.concatenate([kb, q], axis=0), k)
                at_ref[it, h] = (a2[CHUNK:] * decay).astype(BF16)
                probs.append(dict(
                    r=r, it=it, h=h, base=base, hc=hc,
                    lmat=jnp.where(strict, a2[:CHUNK] * decay, 0.0),
                    rhs=jnp.concatenate([v * beta, kb * ecol], axis=1).astype(BF16)))
        tmats = _tri_inverse_many([p["lmat"] for p in probs], level_masks, eye)
        for p, tmat in zip(probs, tmats):
            uw = _dot(tmat.astype(BF16), p["rhs"])
            u_ref[p["r"], pl.ds(p["base"], CHUNK), p["hc"]] = uw[:, :DV_A]
            wq_ref[p["it"], p["h"], 0:CHUNK, :] = uw[:, DV_A:].astype(BF16)
        return carry

    lax.fori_loop(0, rb * n_ch // group_a, phase_a, 0)

    def phase_b(c, carry):
        base = pl.multiple_of(c * CHUNK, CHUNK)
        chains = [(r, h) for r in range(rb) for h in range(H_A)]
        hcs = [slice(h * DV_A, (h + 1) * DV_A) for _, h in chains]
        sts = [s_ref[r, h] for r, h in chains]
        wss = [_dot(wq_ref[r * n_ch + c, h], st.astype(BF16)) for (r, h), st in zip(chains, sts)]
        vnbs = [(u_ref[r, pl.ds(base, CHUNK), hc] - ws[:CHUNK]).astype(BF16)
                for (r, h), hc, ws in zip(chains, hcs, wss)]
        upds = [_dot_tn(kg_ref[r, pl.ds(base, CHUNK), hc], vnb)
                for (r, h), hc, vnb in zip(chains, hcs, vnbs)]
        for (r, h), st, upd in zip(chains, sts, upds):
            glast = gc_ref[r, pl.ds(base + CHUNK - 1, 1), :][:, H_A + h:H_A + h + 1]
            s_ref[r, h] = st * jnp.exp(glast) + upd
        for (r, h), hc, ws, vnb in zip(chains, hcs, wss, vnbs):
            o = ws[CHUNK:] + _dot(at_ref[r * n_ch + c, h], vnb)
            on = o * lax.rsqrt(jnp.mean(o * o, axis=-1, keepdims=True) + EPS) * gn_ref[...]
            z = za_ref[r, pl.ds(base, CHUNK), hc]
            o_ref[r, pl.ds(base, CHUNK), hc] = on * _silu(z)
        return carry

    lax.fori_loop(0, n_ch, phase_b, 0)


def _gated_deltanet(qkv, za, ba, conv_w, alog_vec, dtb_vec, g_norm, rb, tl, group_a):
    bsz, seq, _ = qkv.shape
    n_ch = tl // CHUNK
    kern = functools.partial(_gdn_kernel, rb=rb, tl=tl, group_a=group_a)
    return pl.pallas_call(
        kern,
        grid=(bsz // rb, seq // tl),
        in_specs=[pl.BlockSpec((rb, tl, 3 * W_A), lambda b, i: (b, i, 0)),
                  pl.BlockSpec((rb, tl, W_A), lambda b, i: (b, i, 0)),
                  pl.BlockSpec((rb, tl, LANES), lambda b, i: (b, i, 0)),
                  pl.BlockSpec((CONV_W, 3 * W_A), lambda b, i: (0, 0)),
                  pl.BlockSpec((1, LANES), lambda b, i: (0, 0)),
                  pl.BlockSpec((1, LANES), lambda b, i: (0, 0)),
                  pl.BlockSpec((1, DV_A), lambda b, i: (0, 0))],
        out_specs=pl.BlockSpec((rb, tl, W_A), lambda b, i: (b, i, 0)),
        out_shape=jax.ShapeDtypeStruct((bsz, seq, W_A), F32),
        scratch_shapes=[pltpu.VMEM((rb, tl + 8, 3 * W_A), F32),
                        pltpu.VMEM((rb, H_A, DK_A, DV_A), F32),
                        pltpu.VMEM((rb, tl, LANES), F32),
                        pltpu.VMEM((rb, tl, LANES), F32),
                        pltpu.VMEM((rb * n_ch, LANES, CHUNK), F32),
                        pltpu.VMEM((rb, tl, W_A), F32),
                        pltpu.VMEM((rb * n_ch, H_A, 2 * CHUNK, DK_A), BF16),
                        pltpu.VMEM((rb, tl, W_A), BF16),
                        pltpu.VMEM((rb * n_ch, H_A, CHUNK, CHUNK), BF16)],
        compiler_params=pltpu.CompilerParams(dimension_semantics=("arbitrary", "arbitrary"),
                                             vmem_limit_bytes=VMEM_LIMIT),
        name="gdn",
    )(qkv, za, ba, conv_w, alog_vec, dtb_vec, g_norm.reshape(1, DV_A))


def _bias_kernel(rb_ref, o_ref):
    d = pl.program_id(0)
    kj = lax.broadcasted_iota(I32, (TQ, TQ), 0)
    qi = lax.broadcasted_iota(I32, (TQ, TQ), 1)
    dist = d * TQ + qi - kj
    n = jnp.maximum(dist, 0)
    nf = jnp.maximum(n, 1).astype(F32)
    large = MAX_EXACT + (jnp.log(nf / MAX_EXACT) / math.log(MAX_DIST / MAX_EXACT)
                         * (N_BUCKETS - MAX_EXACT)).astype(I32)
    large = jnp.minimum(large, N_BUCKETS - 1)
    bucket = jnp.where(n < MAX_EXACT, n, large)
    for h in range(H_B):
        acc = jnp.zeros((TQ, TQ), F32)
        for kb in range(N_BUCKETS):
            acc = jnp.where(bucket == kb, rb_ref[kb, h], acc)
        o_ref[0, :, h * TQ:(h + 1) * TQ] = acc * LOG2E


def _bias_tiles(rel_bias, n_diag):
    return pl.pallas_call(
        _bias_kernel,
        grid=(n_diag,),
        in_specs=[pl.BlockSpec(memory_space=pltpu.SMEM)],
        out_specs=pl.BlockSpec((1, TQ, H_B * TQ), lambda d: (d, 0, 0)),
        out_shape=jax.ShapeDtypeStruct((n_diag, TQ, H_B * TQ), F32),
        compiler_params=pltpu.CompilerParams(dimension_semantics=("arbitrary",)),
        name="bias",
    )(rel_bias)


def _fold_keys(x, op):
    x = op(x.reshape(4, TQ // 32, 8, x.shape[-1]), axis=1)
    return op(x, axis=0)


def _dsa_kernel(iq_ref, ikwq_ref, qb_ref, zb_ref, ikw_ref, ckv_ref, gkv_ref, wuv_ref, bias_ref,
                o_ref,
                kvn_ref, kvt_ref, iklo_ref, ikhi_ref, key_ref, am_ref, lg_ref, acc_ref,
                *, rb, seq, k_top):
    qi = pl.program_id(1)
    n_kc = qi + 1
    hq = H_B * TQ
    rows = range(rb)

    @pl.when(qi == 0)
    def _():
        for r in rows:
            for c in range(seq // TQ):
                sl = slice(c * TQ, (c + 1) * TQ)
                ckv = ckv_ref[r, sl, :]
                ms = jnp.mean(ckv * ckv, axis=-1, keepdims=True)
                kvn = (ckv * lax.rsqrt(ms + EPS)) * gkv_ref[...]
                kvn_ref[r, sl, :] = kvn.astype(BF16)
                kvt_ref[r, c] = kvn.T.astype(BF16)
                ikw = ikw_ref[r, sl, :]
                lane = lax.broadcasted_iota(I32, ikw.shape, 1)
                lo = jnp.where(lane < D_IDX, ikw, 0.0)
                iklo_ref[r, sl, :] = lo.astype(BF16)
                ikhi_ref[r, sl, :] = pltpu.roll(lo, D_IDX, axis=1).astype(BF16)

    iq4s, iwts = [], []
    for r in rows:
        iq = iq_ref[r]
        iq4s.append(jnp.concatenate(
            [iq[:, p * LANES:(p + 1) * LANES] for p in range(H_IDX // 2)], axis=0))
        iwts.append((ikwq_ref[r] * (H_IDX ** -0.5 * D_IDX ** -0.5)).T)

    key_j = lax.broadcasted_iota(I32, (TQ, TQ), 0)
    qry_t = qi * TQ + lax.broadcasted_iota(I32, (TQ, TQ), 1)

    def causal_mask(c):
        return (c * TQ + key_j) <= qry_t

    q4s = []
    for r in rows:
        qb = qb_ref[r]
        q4s.append(jnp.concatenate([qb[:, h * R_KV:(h + 1) * R_KV] for h in range(H_B)], axis=0))
    scale = R_KV ** -0.5 * LOG2E

    def pair_loop(body, carry):
        carry = lax.fori_loop(0, n_kc >> 1, lambda i, cr: body((2 * i, 2 * i + 1), cr), carry)
        return lax.cond((n_kc & 1) == 1, lambda cr: body((n_kc - 1,), cr), lambda cr: cr, carry)

    def score_chunks(cs, m0s):
        m0s = list(m0s)
        nc = len(cs)
        ks = pl.ds(pl.multiple_of(cs[0] * TQ, TQ), nc * TQ)
        rels = [_dot_nt(jnp.concatenate([iklo_ref[r, ks, :], ikhi_ref[r, ks, :]], axis=0),
                        iq4s[r]) for r in rows]
        sts = [_dot_nt(kvn_ref[r, ks, :], q4s[r]) for r in rows]
        for j, c in enumerate(cs):
            for r in rows:
                re = rels[r][j * TQ:(j + 1) * TQ]
                ro = rels[r][(nc + j) * TQ:(nc + j + 1) * TQ]
                s = None
                for p in range(H_IDX // 2):
                    ps = slice(p * TQ, (p + 1) * TQ)
                    we = iwts[r][IW_LANE + 2 * p:IW_LANE + 2 * p + 1, :]
                    wo = iwts[r][IW_LANE + 2 * p + 1:IW_LANE + 2 * p + 2, :]
                    t = jnp.maximum(re[:, ps], 0.0) * we + jnp.maximum(ro[:, ps], 0.0) * wo
                    s = t if s is None else s + t
                s = jnp.where(s == 0.0, 0.0, s)
                s = jnp.where(causal_mask(c), s, -jnp.inf)
                bits = pltpu.bitcast(s, I32)
                key_ref[r, c] = jnp.where(bits < 0, bits ^ 0x7FFFFFFF, bits)
                lg = sts[r][j * TQ:(j + 1) * TQ] * scale + bias_ref[qi - c]
                lg_ref[r, c] = lg
                m0s[r] = jnp.maximum(m0s[r], _fold_keys(lg, jnp.max))
        return tuple(m0s)

    m0s = pair_loop(score_chunks, tuple(jnp.full((8, hq), -jnp.inf, F32) for _ in rows))

    kf = float(k_top)
    n_pairs = (n_kc + 1) >> 1
    sign = jnp.int32(-2 ** 31)

    @pl.when((n_kc & 1) == 1)
    def _():
        for r in rows:
            key_ref[r, n_kc] = jnp.full((TQ, TQ), -2 ** 31, I32)

    def count_ge(cands):
        def body(i, accs):
            out = list(accs)
            for c in (2 * i, 2 * i + 1):
                for r in rows:
                    hit = jnp.where(key_ref[r, c] >= cands[r], 1.0, 0.0)
                    out[r] = out[r] + _fold_keys(hit, jnp.sum)
            return tuple(out)
        accs = lax.fori_loop(0, n_pairs, body, tuple(jnp.zeros((8, TQ), F32) for _ in rows))
        return [jnp.sum(a, axis=0, keepdims=True) for a in accs]

    def bit_body(i, tus):
        bit = lax.shift_left(jnp.int32(1), 31 - i)
        cands = [tu | bit for tu in tus]
        cnts = count_ge([(cu ^ sign)[0:1, :] for cu in cands])
        return tuple(jnp.where(cnt >= kf, cu, tu) for cnt, cu, tu in zip(cnts, cands, tus))

    tus = lax.fori_loop(0, 32, bit_body, tuple(jnp.zeros((8, TQ), I32) for _ in rows))
    thrs = [(tu ^ sign)[0:1, :] for tu in tus]

    def count_both(c, carry):
        out = []
        for r in rows:
            cg, ce = carry[r]
            key = key_ref[r, c]
            out.append((cg + _fold_keys(jnp.where(key > thrs[r], 1.0, 0.0), jnp.sum),
                        ce + _fold_keys(jnp.where(key == thrs[r], 1.0, 0.0), jnp.sum)))
        return tuple(out)

    zero8 = jnp.zeros((8, TQ), F32)
    cges = lax.fori_loop(0, n_kc, count_both, tuple((zero8, zero8) for _ in rows))
    qrow = qi * TQ + lax.broadcasted_iota(I32, (1, TQ), 1)
    needs, simple_all = [], None
    for r in rows:
        cnt_gt = jnp.sum(cges[r][0], axis=0, keepdims=True)
        cnt_eq = jnp.sum(cges[r][1], axis=0, keepdims=True)
        need = kf - cnt_gt
        needs.append(need)
        simple = (cnt_eq <= need) | ((thrs[r] == NEG_INF_KEY) & (qrow < k_top))
        simple_all = simple if simple_all is None else (simple_all & simple)
    all_simple = jnp.min(jnp.where(simple_all, 1.0, 0.0)) > 0.5

    @pl.when(all_simple)
    def _():
        def body(c, carry):
            cm = causal_mask(c)
            for r in rows:
                sel = (key_ref[r, c] >= thrs[r]) & cm
                am_ref[r, c] = jnp.where(sel, 0.0, -jnp.inf)
            return carry
        lax.fori_loop(0, n_kc, body, 0)

    @pl.when(jnp.logical_not(all_simple))
    def _():
        lower = jnp.where(lax.broadcasted_iota(I32, (TQ, TQ), 1)
                          <= lax.broadcasted_iota(I32, (TQ, TQ), 0), 1.0, 0.0).astype(BF16)

        def body(c, seens):
            cm = causal_mask(c)
            out = []
            for r in rows:
                key = key_ref[r, c]
                eq = key == thrs[r]
                eqf = jnp.where(eq, 1.0, 0.0)
                rank = seens[r] + _dot(lower, eqf.astype(BF16))
                sel = (key > thrs[r]) | (eq & (rank <= needs[r]))
                am_ref[r, c] = jnp.where(sel & cm, 0.0, -jnp.inf)
                out.append(seens[r] + jnp.sum(eqf, axis=0, keepdims=True))
            return tuple(out)
        lax.fori_loop(0, n_kc, body, tuple(jnp.zeros((1, TQ), F32) for _ in rows))

    def masked_logits(r, c):
        return lg_ref[r, c] + jnp.concatenate([am_ref[r, c]] * H_B, axis=1)

    def softmax_pv(shifts):
        acc_ref[...] = jnp.zeros_like(acc_ref)

        def body(cs, ls):
            out = list(ls)
            for r in rows:
                ps = [jnp.exp2(masked_logits(r, c) - shifts[r]) for c in cs]
                kvt = jnp.concatenate([kvt_ref[r, c] for c in cs], axis=1)
                acc_ref[r] += _dot(kvt, jnp.concatenate(ps, axis=0).astype(BF16))
                for p in ps:
                    out[r] = out[r] + _fold_keys(p, jnp.sum)
            return tuple(out)

        ls = pair_loop(body, tuple(jnp.zeros((8, hq), F32) for _ in rows))
        return tuple(jnp.sum(l, axis=0, keepdims=True) for l in ls)

    lsums = softmax_pv([jnp.max(m, axis=0, keepdims=True) for m in m0s])
    lmin = jnp.min(jnp.concatenate(lsums, axis=1))

    def exact_shift(_):
        def body(cs, ms):
            out = list(ms)
            for c in cs:
                for r in rows:
                    out[r] = jnp.maximum(out[r], _fold_keys(masked_logits(r, c), jnp.max))
            return tuple(out)
        ms = pair_loop(body, tuple(jnp.full((8, hq), -jnp.inf, F32) for _ in rows))
        return softmax_pv([jnp.max(m, axis=0, keepdims=True) for m in ms])

    lsums = lax.cond(lmin >= 2.0 ** -40, lambda _: lsums, exact_shift, 0)
    for r in rows:
        ot = (acc_ref[r] / lsums[r]).astype(BF16)
        for h in range(H_B):
            hs = slice(h * DV_B, (h + 1) * DV_B)
            y = _dot_tn(ot[:, h * TQ:(h + 1) * TQ], wuv_ref[h])
            o_ref[r, :, hs] = y * _silu(zb_ref[r, :, hs])


def _dsa_attention(iq, ikw, qb, zb, ckv, g_kv, w_uv_bf16, bias_tiles, k_top, rb):
    bsz, seq, _ = iq.shape
    n_q = seq // TQ
    hq = H_B * TQ
    kern = functools.partial(_dsa_kernel, rb=rb, seq=seq, k_top=k_top)
    return pl.pallas_call(
        kern,
        grid=(bsz // rb, n_q),
        in_specs=[pl.BlockSpec((rb, TQ, H_IDX * D_IDX), lambda b, i: (b, i, 0)),
                  pl.BlockSpec((rb, TQ, LANES), lambda b, i: (b, i, 0)),
                  pl.BlockSpec((rb, TQ, H_B * R_KV), lambda b, i: (b, i, 0)),
                  pl.BlockSpec((rb, TQ, W_B), lambda b, i: (b, i, 0)),
                  pl.BlockSpec((rb, seq, LANES), lambda b, i: (b, 0, 0),
                               pipeline_mode=pl.Buffered(1)),
                  pl.BlockSpec((rb, seq, R_KV), lambda b, i: (b, 0, 0),
                               pipeline_mode=pl.Buffered(1)),
                  pl.BlockSpec((1, R_KV), lambda b, i: (0, 0)),
                  pl.BlockSpec((H_B, R_KV, DV_B), lambda b, i: (0, 0, 0)),
                  pl.BlockSpec((n_q, TQ, hq), lambda b, i: (0, 0, 0),
                               pipeline_mode=pl.Buffered(1))],
        out_specs=pl.BlockSpec((rb, TQ, W_B), lambda b, i: (b, i, 0)),
        out_shape=jax.ShapeDtypeStruct((bsz, seq, W_B), F32),
        scratch_shapes=[pltpu.VMEM((rb, seq, R_KV), BF16),
                        pltpu.VMEM((rb, n_q, R_KV, TQ), BF16),
                        pltpu.VMEM((rb, seq, LANES), BF16),
                        pltpu.VMEM((rb, seq, LANES), BF16),
                        pltpu.VMEM((rb, n_q + 1, TQ, TQ), I32),
                        pltpu.VMEM((rb, n_q, TQ, TQ), F32),
                        pltpu.VMEM((rb, n_q, TQ, hq), F32),
                        pltpu.VMEM((rb, R_KV, hq), F32)],
        compiler_params=pltpu.CompilerParams(dimension_semantics=("arbitrary", "arbitrary"),
                                             vmem_limit_bytes=VMEM_LIMIT),
        name="dsa",
    )(iq, ikw, qb, zb, ikw, ckv, g_kv.reshape(1, R_KV), w_uv_bf16, bias_tiles)


def _out_kernel(x_ref, oa_ref, ob_ref, mod_ref, g_ref, w_ref, o_ref):
    mix_in = jnp.concatenate([oa_ref[0], ob_ref[0]], axis=-1).astype(BF16)
    mix = _dot(mix_in, w_ref[...])
    ms = jnp.mean(mix * mix, axis=-1, keepdims=True)
    normed = (mix * lax.rsqrt(ms + EPS)) * g_ref[...]
    o_ref[0] = x_ref[0] + mod_ref[0, 2:3, :] * normed


def _output(x, o_a, o_b, mod3, g_post, w_out_bf16, tm):
    bsz, seq, d = x.shape
    return pl.pallas_call(
        _out_kernel,
        grid=(bsz, seq // tm),
        in_specs=[pl.BlockSpec((1, tm, d), lambda b, i: (b, i, 0)),
                  pl.BlockSpec((1, tm, W_A), lambda b, i: (b, i, 0)),
                  pl.BlockSpec((1, tm, W_B), lambda b, i: (b, i, 0)),
                  pl.BlockSpec((1, 3, d), lambda b, i: (b, 0, 0)),
                  pl.BlockSpec((1, d), lambda b, i: (0, 0)),
                  pl.BlockSpec((W_A + W_B, d), lambda b, i: (0, 0))],
        out_specs=pl.BlockSpec((1, tm, d), lambda b, i: (b, i, 0)),
        out_shape=jax.ShapeDtypeStruct((bsz, seq, d), F32),
        compiler_params=pltpu.CompilerParams(dimension_semantics=("arbitrary", "arbitrary"),
                                             vmem_limit_bytes=VMEM_LIMIT),
        name="outproj",
    )(x, o_a, o_b, mod3, g_post.reshape(1, d), w_out_bf16)


def _pad_lanes(w):
    return jnp.pad(w, ((0, 0), (0, LANES - w.shape[1])))


def _pad_input_projection(w_in):
    o_ba = 3 * W_A + W_A
    o_qb = o_ba + 2 * H_A
    o_ik = o_qb + H_B * R_KV + R_KV + W_B + H_IDX * D_IDX
    return jnp.concatenate([w_in[:, :o_ba], _pad_lanes(w_in[:, o_ba:o_qb]),
                            w_in[:, o_qb:o_ik], _pad_lanes(w_in[:, o_ik:])], axis=1)


def _head_lanes(v):
    return jnp.zeros((1, LANES), F32).at[0, H_A:2 * H_A].set(v.astype(F32))


def kernel(x, c, w_ada, b_ada, g_pre, w_in, conv_w, a_log, dt_bias, g_gdn, g_kv, w_uv, rel_bias, w_out, g_post):
    bsz, seq, d = x.shape
    depth = w_ada.shape[0]
    assert seq % TQ == 0 and seq % CHUNK == 0
    k_top = min(TOPK_MAX, seq // 4)
    tm = min(512, seq)
    tl = min(256, seq)
    rb = 4 if bsz % 4 == 0 else (2 if bsz % 2 == 0 else 1)
    bias_tiles = _bias_tiles(rel_bias, seq // TQ)
    for layer in range(depth):
        mod3 = _modulation(c, w_ada[layer], b_ada[layer]).reshape(bsz, 3, d)
        w_pad = _pad_input_projection(w_in[layer]).astype(BF16)
        qkv, za, ba, qb, ckv, zb, iq, ikw = _projection(x, mod3, g_pre[layer], w_pad, tm)
        o_a = _gated_deltanet(qkv, za, ba, conv_w[layer], _head_lanes(a_log[layer]),
                              _head_lanes(dt_bias[layer]), g_gdn[layer], rb, tl, 4)
        o_b = _dsa_attention(iq, ikw, qb, zb, ckv, g_kv[layer], w_uv[layer].astype(BF16),
                             bias_tiles, k_top, 4 if bsz % 4 == 0 else 1)
        x = _output(x, o_a, o_b, mod3, g_post[layer], w_out[layer].astype(BF16), tm)
    return x
```

```python
import functools
import math

import jax
import jax.numpy as jnp
from jax import lax
from jax.experimental import pallas as pl
from jax.experimental.pallas import tpu as pltpu

F32 = jnp.float32
BF16 = jnp.bfloat16
I32 = jnp.int32
HIGHEST = lax.Precision.HIGHEST

H_A, DK_A, DV_A = 4, 128, 128
W_A = H_A * DV_A
CONV_W = 4
CHUNK = 64
H_B, R_KV, DV_B = 4, 128, 128
W_B = H_B * DV_B
H_IDX, D_IDX = 8, 64
TOPK_MAX = 256
N_BUCKETS, MAX_EXACT, MAX_DIST = 32, 16, 128
EPS = 1e-6
LOG2E = math.log2(math.e)

LANES = 128
TQ = 128
VMEM_LIMIT = 58 * 1024 * 1024

C_QKV = (0, 3 * W_A)
C_ZA = (C_QKV[1], C_QKV[1] + W_A)
C_BA = (C_ZA[1], C_ZA[1] + LANES)
C_QB = (C_BA[1], C_BA[1] + H_B * R_KV)
C_CKV = (C_QB[1], C_QB[1] + R_KV)
C_ZB = (C_CKV[1], C_CKV[1] + W_B)
C_IQ = (C_ZB[1], C_ZB[1] + H_IDX * D_IDX)
C_IKW = (C_IQ[1], C_IQ[1] + LANES)
D_IN_PAD = C_IKW[1]
IW_LANE = D_IDX

NEG_INF_KEY = -2139095041


def _sigmoid(x):
    return 1.0 / (1.0 + jnp.exp(-x))


def _silu(x):
    return x * _sigmoid(x)


def _softplus(x):
    return jnp.maximum(x, 0.0) + jnp.log1p(jnp.exp(-jnp.abs(x)))


def _dot(a, b, precision=None):
    return jnp.dot(a, b, precision=precision, preferred_element_type=F32)


def _dot_nt(a, b, precision=None):
    return lax.dot_general(a, b, (((1,), (1,)), ((), ())), precision=precision,
                           preferred_element_type=F32)


def _dot_tn(a, b, precision=None):
    return lax.dot_general(a, b, (((0,), (0,)), ((), ())), precision=precision,
                           preferred_element_type=F32)


def _bdot(a, b, precision=None):
    return _dot(a.astype(BF16), b.astype(BF16))


def _bdot_nt(a, b, precision=None):
    return _dot_nt(a.astype(BF16), b.astype(BF16))


def _bdot_tn(a, b, precision=None):
    return _dot_tn(a.astype(BF16), b.astype(BF16))


def _mod_kernel(c_ref, w_ref, b_ref, o_ref):
    c = c_ref[...]
    o_ref[...] = _dot(_silu(c), w_ref[...], HIGHEST) + b_ref[...]


def _modulation(c, w_ada, b_ada):
    bsz, d = c.shape
    n = w_ada.shape[1]
    tn = 512
    return pl.pallas_call(
        _mod_kernel,
        grid=(n // tn,),
        in_specs=[pl.BlockSpec((bsz, d), lambda j: (0, 0)),
                  pl.BlockSpec((d, tn), lambda j: (0, j)),
                  pl.BlockSpec((1, tn), lambda j: (0, j))],
        out_specs=pl.BlockSpec((bsz, tn), lambda j: (0, j)),
        out_shape=jax.ShapeDtypeStruct((bsz, n), F32),
        compiler_params=pltpu.CompilerParams(dimension_semantics=("arbitrary",),
                                             vmem_limit_bytes=VMEM_LIMIT),
        name="mod",
    )(c, w_ada, b_ada.reshape(1, n))


def _proj_kernel(x_ref, mod_ref, g_ref, w_ref,
                 qkv_ref, za_ref, ba_ref, qb_ref, ckv_ref, zb_ref, iq_ref, ikw_ref):
    x = x_ref[0]
    ms = jnp.mean(x * x, axis=-1, keepdims=True)
    xn = x * lax.rsqrt(ms + EPS)
    shift = mod_ref[0, 0:1, :]
    scale = mod_ref[0, 1:2, :]
    h = (xn * g_ref[...]) * (1.0 + scale) + shift
    hb = h.astype(BF16)

    def mm(cols):
        return _dot(hb, w_ref[:, cols[0]:cols[1]])

    qkv_ref[0] = mm(C_QKV)
    za_ref[0] = mm(C_ZA)
    ba_ref[0] = mm(C_BA)
    qb_ref[0] = mm(C_QB).astype(BF16)
    ckv_ref[0] = mm(C_CKV)
    zb_ref[0] = mm(C_ZB)
    iq_ref[0] = mm(C_IQ).astype(BF16)
    ikw_ref[0] = mm(C_IKW)


def _projection(x, mod3, g_pre, w_pad, tm):
    bsz, seq, d = x.shape
    widths = [(C_QKV, F32), (C_ZA, F32), (C_BA, F32), (C_QB, BF16),
              (C_CKV, F32), (C_ZB, F32), (C_IQ, BF16), (C_IKW, F32)]
    out_shape = [jax.ShapeDtypeStruct((bsz, seq, c[1] - c[0]), dt) for c, dt in widths]
    out_specs = [pl.BlockSpec((1, tm, c[1] - c[0]), lambda b, i: (b, i, 0)) for c, _ in widths]
    return pl.pallas_call(
        _proj_kernel,
        grid=(bsz, seq // tm),
        in_specs=[pl.BlockSpec((1, tm, d), lambda b, i: (b, i, 0)),
                  pl.BlockSpec((1, 3, d), lambda b, i: (b, 0, 0)),
                  pl.BlockSpec((1, d), lambda b, i: (0, 0)),
                  pl.BlockSpec((d, D_IN_PAD), lambda b, i: (0, 0))],
        out_specs=out_specs,
        out_shape=out_shape,
        compiler_params=pltpu.CompilerParams(dimension_semantics=("arbitrary", "arbitrary"),
                                             vmem_limit_bytes=VMEM_LIMIT),
        name="proj",
    )(x, mod3, g_pre.reshape(1, d), w_pad)


def _tri_inverse_many(lmats, level_masks, eye):
    ts = [eye - jnp.where(level_masks[0], lm, 0.0) for lm in lmats]
    for m in level_masks[1:]:
        xs = [_bdot(jnp.where(m, lm, 0.0), t) for lm, t in zip(lmats, ts)]
        ts = [t - _bdot(t, x) for t, x in zip(ts, xs)]
    return ts


def _gdn_kernel(qkv_ref, za_ref, ba_ref, cw_ref, alog_ref, dtb_ref, gn_ref, o_ref,
                ext_ref, s_ref, gc_ref, beta_ref, gct_ref, u_ref, wq_ref, kg_ref, at_ref,
                *, rb, tl, group_a):
    n_ch = tl // CHUNK
    li = pl.program_id(1)

    @pl.when(li == 0)
    def _():
        ext_ref[:, 0:8, :] = jnp.zeros((rb, 8, 3 * W_A), F32)
        s_ref[...] = jnp.zeros_like(s_ref)

    @pl.when(li > 0)
    def _():
        ext_ref[:, 0:8, :] = ext_ref[:, tl:tl + 8, :]

    ext_ref[:, 8:8 + tl, :] = qkv_ref[...]

    r_i = lax.broadcasted_iota(I32, (tl, tl), 0)
    c_i = lax.broadcasted_iota(I32, (tl, tl), 1)
    shift = CHUNK.bit_length() - 1
    tri = jnp.where((c_i <= r_i) & ((r_i >> shift) == (c_i >> shift)), 1.0, 0.0)
    for r in range(rb):
        ba = ba_ref[r]
        beta_ref[r] = _sigmoid(ba)
        g = -jnp.exp(alog_ref[...]) * _softplus(ba + dtb_ref[...])
        gc = _dot(tri, g, HIGHEST)
        gc_ref[r] = gc
        gct = gc.T
        for c in range(n_ch):
            gct_ref[r * n_ch + c] = gct[:, c * CHUNK:(c + 1) * CHUNK]

    row = lax.broadcasted_iota(I32, (CHUNK, CHUNK), 0)
    col = lax.broadcasted_iota(I32, (CHUNK, CHUNK), 1)
    causal = row >= col
    strict = row > col
    eye = jnp.where(row == col, 1.0, 0.0)
    level_masks = []
    s = 1
    while s < CHUNK:
        ls = s.bit_length() - 1
        level_masks.append(((row >> (ls + 1)) == (col >> (ls + 1)))
                           & (((row >> ls) & 1) == 1) & (((col >> ls) & 1) == 0))
        s *= 2

    def phase_a(ig, carry):
        probs = []
        for j in range(group_a):
            it = ig * group_a + j
            r = it // n_ch
            c = it - r * n_ch
            base = pl.multiple_of(c * CHUNK, CHUNK)
            gc_c = gc_ref[r, pl.ds(base, CHUNK), :]
            beta_c = beta_ref[r, pl.ds(base, CHUNK), :]
            gct_c = gct_ref[it]
            for h in range(H_A):
                def conv_silu(sec):
                    c0 = sec * W_A + h * DK_A
                    win = ext_ref[r, pl.ds(base, CHUNK + 8), c0:c0 + DK_A]
                    acc = None
                    for t in range(CONV_W):
                        sh = CONV_W - 1 - t
                        u = win if sh == 0 else pltpu.roll(win, sh, axis=0)
                        term = u[8:8 + CHUNK] * cw_ref[t:t + 1, c0:c0 + DK_A]
                        acc = term if acc is None else acc + term
                    return _silu(acc)

                q = conv_silu(0)
                k = conv_silu(1)
                v = conv_silu(2)
                q = q * lax.rsqrt(jnp.sum(q * q, axis=-1, keepdims=True) + EPS) * (DK_A ** -0.5)
                k = k * lax.rsqrt(jnp.sum(k * k, axis=-1, keepdims=True) + EPS)
                beta = beta_c[:, h:h + 1]
                gcol = gc_c[:, H_A + h:H_A + h + 1]
                grow = gct_c[H_A + h:H_A + h + 1, :]
                glast = gc_c[CHUNK - 1:CHUNK, H_A + h:H_A + h + 1]
                decay = jnp.exp(jnp.where(causal, gcol - grow, -jnp.inf))
                ecol = jnp.exp(gcol)
                kb = k * beta
                hc = slice(h * DV_A, (h + 1) * DV_A)
                wq_ref[it, h, CHUNK:2 * CHUNK, :] = (q * ecol).astype(BF16)
                kg_ref[r, pl.ds(base, CHUNK), hc] = (k * jnp.exp(glast - gcol)).astype(BF16)
                a2 = _bdot_nt(jnp.concatenate([kb, q], axis=0), k)
                at_ref[it, h] = (a2[CHUNK:] * decay).astype(BF16)
                probs.append(dict(
                    r=r, it=it, h=h, base=base, hc=hc,
                    lmat=jnp.where(strict, a2[:CHUNK] * decay, 0.0),
                    rhs=jnp.concatenate([v * beta, kb * ecol], axis=1).astype(BF16)))
        tmats = _tri_inverse_many([p["lmat"] for p in probs], level_masks, eye)
        for p, tmat in zip(probs, tmats):
            uw = _dot(tmat.astype(BF16), p["rhs"])
            u_ref[p["r"], pl.ds(p["base"], CHUNK), p["hc"]] = uw[:, :DV_A]
            wq_ref[p["it"], p["h"], 0:CHUNK, :] = uw[:, DV_A:].astype(BF16)
        return carry

    lax.fori_loop(0, rb * n_ch // group_a, phase_a, 0)

    def phase_b(c, carry):
        base = pl.multiple_of(c * CHUNK, CHUNK)
        chains = [(r, h) for r in range(rb) for h in range(H_A)]
        hcs = [slice(h * DV_A, (h + 1) * DV_A) for _, h in chains]
        sts = [s_ref[r, h] for r, h in chains]
        wss = [_dot(wq_ref[r * n_ch + c, h], st.astype(BF16)) for (r, h), st in zip(chains, sts)]
        vnbs = [(u_ref[r, pl.ds(base, CHUNK), hc] - ws[:CHUNK]).astype(BF16)
                for (r, h), hc, ws in zip(chains, hcs, wss)]
        upds = [_dot_tn(kg_ref[r, pl.ds(base, CHUNK), hc], vnb)
                for (r, h), hc, vnb in zip(chains, hcs, vnbs)]
        for (r, h), st, upd in zip(chains, sts, upds):
            glast = gc_ref[r, pl.ds(base + CHUNK - 1, 1), :][:, H_A + h:H_A + h + 1]
            s_ref[r, h] = st * jnp.exp(glast) + upd
        for (r, h), hc, ws, vnb in zip(chains, hcs, wss, vnbs):
            o = ws[CHUNK:] + _dot(at_ref[r * n_ch + c, h], vnb)
            on = o * lax.rsqrt(jnp.mean(o * o, axis=-1, keepdims=True) + EPS) * gn_ref[...]
            z = za_ref[r, pl.ds(base, CHUNK), hc]
            o_ref[r, pl.ds(base, CHUNK), hc] = on * _silu(z)
        return carry

    lax.fori_loop(0, n_ch, phase_b, 0)


def _gated_deltanet(qkv, za, ba, conv_w, alog_vec, dtb_vec, g_norm, rb, tl, group_a):
    bsz, seq, _ = qkv.shape
    n_ch = tl // CHUNK
    kern = functools.partial(_gdn_kernel, rb=rb, tl=tl, group_a=group_a)
    return pl.pallas_call(
        kern,
        grid=(bsz // rb, seq // tl),
        in_specs=[pl.BlockSpec((rb, tl, 3 * W_A), lambda b, i: (b, i, 0)),
                  pl.BlockSpec((rb, tl, W_A), lambda b, i: (b, i, 0)),
                  pl.BlockSpec((rb, tl, LANES), lambda b, i: (b, i, 0)),
                  pl.BlockSpec((CONV_W, 3 * W_A), lambda b, i: (0, 0)),
                  pl.BlockSpec((1, LANES), lambda b, i: (0, 0)),
                  pl.BlockSpec((1, LANES), lambda b, i: (0, 0)),
                  pl.BlockSpec((1, DV_A), lambda b, i: (0, 0))],
        out_specs=pl.BlockSpec((rb, tl, W_A), lambda b, i: (b, i, 0)),
        out_shape=jax.ShapeDtypeStruct((bsz, seq, W_A), F32),
        scratch_shapes=[pltpu.VMEM((rb, tl + 8, 3 * W_A), F32),
                        pltpu.VMEM((rb, H_A, DK_A, DV_A), F32),
                        pltpu.VMEM((rb, tl, LANES), F32),
                        pltpu.VMEM((rb, tl, LANES), F32),
                        pltpu.VMEM((rb * n_ch, LANES, CHUNK), F32),
                        pltpu.VMEM((rb, tl, W_A), F32),
                        pltpu.VMEM((rb * n_ch, H_A, 2 * CHUNK, DK_A), BF16),
                        pltpu.VMEM((rb, tl, W_A), BF16),
                        pltpu.VMEM((rb * n_ch, H_A, CHUNK, CHUNK), BF16)],
        compiler_params=pltpu.CompilerParams(dimension_semantics=("arbitrary", "arbitrary"),
                                             vmem_limit_bytes=VMEM_LIMIT),
        name="gdn",
    )(qkv, za, ba, conv_w, alog_vec, dtb_vec, g_norm.reshape(1, DV_A))


def _bias_kernel(rb_ref, o_ref):
    d = pl.program_id(0)
    kj = lax.broadcasted_iota(I32, (TQ, TQ), 0)
    qi = lax.broadcasted_iota(I32, (TQ, TQ), 1)
    dist = d * TQ + qi - kj
    n = jnp.maximum(dist, 0)
    nf = jnp.maximum(n, 1).astype(F32)
    large = MAX_EXACT + (jnp.log(nf / MAX_EXACT) / math.log(MAX_DIST / MAX_EXACT)
                         * (N_BUCKETS - MAX_EXACT)).astype(I32)
    large = jnp.minimum(large, N_BUCKETS - 1)
    bucket = jnp.where(n < MAX_EXACT, n, large)
    for h in range(H_B):
        acc = jnp.zeros((TQ, TQ), F32)
        for kb in range(N_BUCKETS):
            acc = jnp.where(bucket == kb, rb_ref[kb, h], acc)
        o_ref[0, :, h * TQ:(h + 1) * TQ] = acc * LOG2E


def _bias_tiles(rel_bias, n_diag):
    return pl.pallas_call(
        _bias_kernel,
        grid=(n_diag,),
        in_specs=[pl.BlockSpec(memory_space=pltpu.SMEM)],
        out_specs=pl.BlockSpec((1, TQ, H_B * TQ), lambda d: (d, 0, 0)),
        out_shape=jax.ShapeDtypeStruct((n_diag, TQ, H_B * TQ), F32),
        compiler_params=pltpu.CompilerParams(dimension_semantics=("arbitrary",)),
        name="bias",
    )(rel_bias)


def _fold_keys(x, op):
    x = op(x.reshape(4, TQ // 32, 8, x.shape[-1]), axis=1)
    return op(x, axis=0)


def _dsa_kernel(iq_ref, ikwq_ref, qb_ref, zb_ref, ikw_ref, ckv_ref, gkv_ref, wuv_ref, bias_ref,
                o_ref,
                kvn_ref, kvt_ref, iklo_ref, ikhi_ref, key_ref, am_ref, lg_ref, acc_ref,
                *, rb, seq, k_top):
    qi = pl.program_id(1)
    n_kc = qi + 1
    hq = H_B * TQ
    rows = range(rb)

    @pl.when(qi == 0)
    def _():
        for r in rows:
            for c in range(seq // TQ):
                sl = slice(c * TQ, (c + 1) * TQ)
                ckv = ckv_ref[r, sl, :]
                ms = jnp.mean(ckv * ckv, axis=-1, keepdims=True)
                kvn = (ckv * lax.rsqrt(ms + EPS)) * gkv_ref[...]
                kvn_ref[r, sl, :] = kvn.astype(BF16)
                kvt_ref[r, c] = kvn.T.astype(BF16)
                ikw = ikw_ref[r, sl, :]
                lane = lax.broadcasted_iota(I32, ikw.shape, 1)
                lo = jnp.where(lane < D_IDX, ikw, 0.0)
                iklo_ref[r, sl, :] = lo.astype(BF16)
                ikhi_ref[r, sl, :] = pltpu.roll(lo, D_IDX, axis=1).astype(BF16)

    iq4s, iwts = [], []
    for r in rows:
        iq = iq_ref[r]
        iq4s.append(jnp.concatenate(
            [iq[:, p * LANES:(p + 1) * LANES] for p in range(H_IDX // 2)], axis=0))
        iwts.append((ikwq_ref[r] * (H_IDX ** -0.5 * D_IDX ** -0.5)).T)

    key_j = lax.broadcasted_iota(I32, (TQ, TQ), 0)
    qry_t = qi * TQ + lax.broadcasted_iota(I32, (TQ, TQ), 1)

    def causal_mask(c):
        return (c * TQ + key_j) <= qry_t

    q4s = []
    for r in rows:
        qb = qb_ref[r]
        q4s.append(jnp.concatenate([qb[:, h * R_KV:(h + 1) * R_KV] for h in range(H_B)], axis=0))
    scale = R_KV ** -0.5 * LOG2E

    def pair_loop(body, carry):
        carry = lax.fori_loop(0, n_kc >> 1, lambda i, cr: body((2 * i, 2 * i + 1), cr), carry)
        return lax.cond((n_kc & 1) == 1, lambda cr: body((n_kc - 1,), cr), lambda cr: cr, carry)

    def score_chunks(cs, m0s):
        m0s = list(m0s)
        nc = len(cs)
        ks = pl.ds(pl.multiple_of(cs[0] * TQ, TQ), nc * TQ)
        rels = [_dot_nt(jnp.concatenate([iklo_ref[r, ks, :], ikhi_ref[r, ks, :]], axis=0),
                        iq4s[r]) for r in rows]
        sts = [_dot_nt(kvn_ref[r, ks, :], q4s[r]) for r in rows]
        for j, c in enumerate(cs):
            for r in rows:
                re = rels[r][j * TQ:(j + 1) * TQ]
                ro = rels[r][(nc + j) * TQ:(nc + j + 1) * TQ]
                s = None
                for p in range(H_IDX // 2):
                    ps = slice(p * TQ, (p + 1) * TQ)
                    we = iwts[r][IW_LANE + 2 * p:IW_LANE + 2 * p + 1, :]
                    wo = iwts[r][IW_LANE + 2 * p + 1:IW_LANE + 2 * p + 2, :]
                    t = jnp.maximum(re[:, ps], 0.0) * we + jnp.maximum(ro[:, ps], 0.0) * wo
                    s = t if s is None else s + t
                s = jnp.where(s == 0.0, 0.0, s)
                s = jnp.where(causal_mask(c), s, -jnp.inf)
                bits = pltpu.bitcast(s, I32)
                key_ref[r, c] = jnp.where(bits < 0, bits ^ 0x7FFFFFFF, bits)
                lg = sts[r][j * TQ:(j + 1) * TQ] * scale + bias_ref[qi - c]
                lg_ref[r, c] = lg
                m0s[r] = jnp.maximum(m0s[r], _fold_keys(lg, jnp.max))
        return tuple(m0s)

    m0s = pair_loop(score_chunks, tuple(jnp.full((8, hq), -jnp.inf, F32) for _ in rows))

    kf = float(k_top)
    n_pairs = (n_kc + 1) >> 1
    sign = jnp.int32(-2 ** 31)

    @pl.when((n_kc & 1) == 1)
    def _():
        for r in rows:
            key_ref[r, n_kc] = jnp.full((TQ, TQ), -2 ** 31, I32)

    def count_ge(cands):
        def body(i, accs):
            out = list(accs)
            for c in (2 * i, 2 * i + 1):
                for r in rows:
                    hit = jnp.where(key_ref[r, c] >= cands[r], 1.0, 0.0)
                    out[r] = out[r] + _fold_keys(hit, jnp.sum)
            return tuple(out)
        accs = lax.fori_loop(0, n_pairs, body, tuple(jnp.zeros((8, TQ), F32) for _ in rows))
        return [jnp.sum(a, axis=0, keepdims=True) for a in accs]

    def bit_body(i, tus):
        bit = lax.shift_left(jnp.int32(1), 31 - i)
        cands = [tu | bit for tu in tus]
        cnts = count_ge([(cu ^ sign)[0:1, :] for cu in cands])
        return tuple(jnp.where(cnt >= kf, cu, tu) for cnt, cu, tu in zip(cnts, cands, tus))

    tus = lax.fori_loop(0, 32, bit_body, tuple(jnp.zeros((8, TQ), I32) for _ in rows))
    thrs = [(tu ^ sign)[0:1, :] for tu in tus]

    def count_both(c, carry):
        out = []
        for r in rows:
            cg, ce = carry[r]
            key = key_ref[r, c]
            out.append((cg + _fold_keys(jnp.where(key > thrs[r], 1.0, 0.0), jnp.sum),
                        ce + _fold_keys(jnp.where(key == thrs[r], 1.0, 0.0), jnp.sum)))
        return tuple(out)

    zero8 = jnp.zeros((8, TQ), F32)
    cges = lax.fori_loop(0, n_kc, count_both, tuple((zero8, zero8) for _ in rows))
    qrow = qi * TQ + lax.broadcasted_iota(I32, (1, TQ), 1)
    needs, simple_all = [], None
    for r in rows:
        cnt_gt = jnp.sum(cges[r][0], axis=0, keepdims=True)
        cnt_eq = jnp.sum(cges[r][1], axis=0, keepdims=True)
        need = kf - cnt_gt
        needs.append(need)
        simple = (cnt_eq <= need) | ((thrs[r] == NEG_INF_KEY) & (qrow < k_top))
        simple_all = simple if simple_all is None else (simple_all & simple)
    all_simple = jnp.min(jnp.where(simple_all, 1.0, 0.0)) > 0.5

    @pl.when(all_simple)
    def _():
        def body(c, carry):
            cm = causal_mask(c)
            for r in rows:
                sel = (key_ref[r, c] >= thrs[r]) & cm
                am_ref[r, c] = jnp.where(sel, 0.0, -jnp.inf)
            return carry
        lax.fori_loop(0, n_kc, body, 0)

    @pl.when(jnp.logical_not(all_simple))
    def _():
        lower = jnp.where(lax.broadcasted_iota(I32, (TQ, TQ), 1)
                          <= lax.broadcasted_iota(I32, (TQ, TQ), 0), 1.0, 0.0).astype(BF16)

        def body(c, seens):
            cm = causal_mask(c)
            out = []
            for r in rows:
                key = key_ref[r, c]
                eq = key == thrs[r]
                eqf = jnp.where(eq, 1.0, 0.0)
                rank = seens[r] + _dot(lower, eqf.astype(BF16))
                sel = (key > thrs[r]) | (eq & (rank <= needs[r]))
                am_ref[r, c] = jnp.where(sel & cm, 0.0, -jnp.inf)
                out.append(seens[r] + jnp.sum(eqf, axis=0, keepdims=True))
            return tuple(out)
        lax.fori_loop(0, n_kc, body, tuple(jnp.zeros((1, TQ), F32) for _ in rows))

    def masked_logits(r, c):
        return lg_ref[r, c] + jnp.concatenate([am_ref[r, c]] * H_B, axis=1)

    def softmax_pv(shifts):
        acc_ref[...] = jnp.zeros_like(acc_ref)

        def body(cs, ls):
            out = list(ls)
            for r in rows:
                ps = [jnp.exp2(masked_logits(r, c) - shifts[r]) for c in cs]
                kvt = jnp.concatenate([kvt_ref[r, c] for c in cs], axis=1)
                acc_ref[r] += _dot(kvt, jnp.concatenate(ps, axis=0).astype(BF16))
                for p in ps:
                    out[r] = out[r] + _fold_keys(p, jnp.sum)
            return tuple(out)

        ls = pair_loop(body, tuple(jnp.zeros((8, hq), F32) for _ in rows))
        return tuple(jnp.sum(l, axis=0, keepdims=True) for l in ls)

    lsums = softmax_pv([jnp.max(m, axis=0, keepdims=True) for m in m0s])
    lmin = jnp.min(jnp.concatenate(lsums, axis=1))

    def exact_shift(_):
        def body(cs, ms):
            out = list(ms)
            for c in cs:
                for r in rows:
                    out[r] = jnp.maximum(out[r], _fold_keys(masked_logits(r, c), jnp.max))
            return tuple(out)
        ms = pair_loop(body, tuple(jnp.full((8, hq), -jnp.inf, F32) for _ in rows))
        return softmax_pv([jnp.max(m, axis=0, keepdims=True) for m in ms])

    lsums = lax.cond(lmin >= 2.0 ** -40, lambda _: lsums, exact_shift, 0)
    for r in rows:
        ot = (acc_ref[r] / lsums[r]).astype(BF16)
        for h in range(H_B):
            hs = slice(h * DV_B, (h + 1) * DV_B)
            y = _dot_tn(ot[:, h * TQ:(h + 1) * TQ], wuv_ref[h])
            o_ref[r, :, hs] = y * _silu(zb_ref[r, :, hs])


def _dsa_attention(iq, ikw, qb, zb, ckv, g_kv, w_uv_bf16, bias_tiles, k_top, rb):
    bsz, seq, _ = iq.shape
    n_q = seq // TQ
    hq = H_B * TQ
    kern = functools.partial(_dsa_kernel, rb=rb, seq=seq, k_top=k_top)
    return pl.pallas_call(
        kern,
        grid=(bsz // rb, n_q),
        in_specs=[pl.BlockSpec((rb, TQ, H_IDX * D_IDX), lambda b, i: (b, i, 0)),
                  pl.BlockSpec((rb, TQ, LANES), lambda b, i: (b, i, 0)),
                  pl.BlockSpec((rb, TQ, H_B * R_KV), lambda b, i: (b, i, 0)),
                  pl.BlockSpec((rb, TQ, W_B), lambda b, i: (b, i, 0)),
                  pl.BlockSpec((rb, seq, LANES), lambda b, i: (b, 0, 0),
                               pipeline_mode=pl.Buffered(1)),
                  pl.BlockSpec((rb, seq, R_KV), lambda b, i: (b, 0, 0),
                               pipeline_mode=pl.Buffered(1)),
                  pl.BlockSpec((1, R_KV), lambda b, i: (0, 0)),
                  pl.BlockSpec((H_B, R_KV, DV_B), lambda b, i: (0, 0, 0)),
                  pl.BlockSpec((n_q, TQ, hq), lambda b, i: (0, 0, 0),
                               pipeline_mode=pl.Buffered(1))],
        out_specs=pl.BlockSpec((rb, TQ, W_B), lambda b, i: (b, i, 0)),
        out_shape=jax.ShapeDtypeStruct((bsz, seq, W_B), F32),
        scratch_shapes=[pltpu.VMEM((rb, seq, R_KV), BF16),
                        pltpu.VMEM((rb, n_q, R_KV, TQ), BF16),
                        pltpu.VMEM((rb, seq, LANES), BF16),
                        pltpu.VMEM((rb, seq, LANES), BF16),
                        pltpu.VMEM((rb, n_q + 1, TQ, TQ), I32),
                        pltpu.VMEM((rb, n_q, TQ, TQ), F32),
                        pltpu.VMEM((rb, n_q, TQ, hq), F32),
                        pltpu.VMEM((rb, R_KV, hq), F32)],
        compiler_params=pltpu.CompilerParams(dimension_semantics=("arbitrary", "arbitrary"),
                                             vmem_limit_bytes=VMEM_LIMIT),
        name="dsa",
    )(iq, ikw, qb, zb, ikw, ckv, g_kv.reshape(1, R_KV), w_uv_bf16, bias_tiles)


def _out_kernel(x_ref, oa_ref, ob_ref, mod_ref, g_ref, w_ref, o_ref):
    mix_in = jnp.concatenate([oa_ref[0], ob_ref[0]], axis=-1).astype(BF16)
    mix = _dot(mix_in, w_ref[...])
    ms = jnp.mean(mix * mix, axis=-1, keepdims=True)
    normed = (mix * lax.rsqrt(ms + EPS)) * g_ref[...]
    o_ref[0] = x_ref[0] + mod_ref[0, 2:3, :] * normed


def _output(x, o_a, o_b, mod3, g_post, w_out_bf16, tm):
    bsz, seq, d = x.shape
    return pl.pallas_call(
        _out_kernel,
        grid=(bsz, seq // tm),
        in_specs=[pl.BlockSpec((1, tm, d), lambda b, i: (b, i, 0)),
                  pl.BlockSpec((1, tm, W_A), lambda b, i: (b, i, 0)),
                  pl.BlockSpec((1, tm, W_B), lambda b, i: (b, i, 0)),
                  pl.BlockSpec((1, 3, d), lambda b, i: (b, 0, 0)),
                  pl.BlockSpec((1, d), lambda b, i: (0, 0)),
                  pl.BlockSpec((W_A + W_B, d), lambda b, i: (0, 0))],
        out_specs=pl.BlockSpec((1, tm, d), lambda b, i: (b, i, 0)),
        out_shape=jax.ShapeDtypeStruct((bsz, seq, d), F32),
        compiler_params=pltpu.CompilerParams(dimension_semantics=("arbitrary", "arbitrary"),
                                             vmem_limit_bytes=VMEM_LIMIT),
        name="outproj",
    )(x, o_a, o_b, mod3, g_post.reshape(1, d), w_out_bf16)


def _pad_lanes(w):
    return jnp.pad(w, ((0, 0), (0, LANES - w.shape[1])))


def _pad_input_projection(w_in):
    o_ba = 3 * W_A + W_A
    o_qb = o_ba + 2 * H_A
    o_ik = o_qb + H_B * R_KV + R_KV + W_B + H_IDX * D_IDX
    return jnp.concatenate([w_in[:, :o_ba], _pad_lanes(w_in[:, o_ba:o_qb]),
                            w_in[:, o_qb:o_ik], _pad_lanes(w_in[:, o_ik:])], axis=1)


def _head_lanes(v):
    return jnp.zeros((1, LANES), F32).at[0, H_A:2 * H_A].set(v.astype(F32))


def kernel(x, c, w_ada, b_ada, g_pre, w_in, conv_w, a_log, dt_bias, g_gdn, g_kv, w_uv, rel_bias, w_out, g_post):
    bsz, seq, d = x.shape
    depth = w_ada.shape[0]
    assert seq % TQ == 0 and seq % CHUNK == 0
    k_top = min(TOPK_MAX, seq // 4)
    tm = min(512, seq)
    tl = min(128, seq)
    rb = 8 if bsz % 8 == 0 else (2 if bsz % 2 == 0 else 1)
    bias_tiles = _bias_tiles(rel_bias, seq // TQ)
    for layer in range(depth):
        mod3 = _modulation(c, w_ada[layer], b_ada[layer]).reshape(bsz, 3, d)
        w_pad = _pad_input_projection(w_in[layer]).astype(BF16)
        qkv, za, ba, qb, ckv, zb, iq, ikw = _projection(x, mod3, g_pre[layer], w_pad, tm)
        o_a = _gated_deltanet(qkv, za, ba, conv_w[layer], _head_lanes(a_log[layer]),
                              _head_lanes(dt_bias[layer]), g_gdn[layer], rb, tl, math.gcd(8, rb * (tl // CHUNK)))
        o_b = _dsa_attention(iq, ikw, qb, zb, ckv, g_kv[layer], w_uv[layer].astype(BF16),
                             bias_tiles, k_top, 4 if bsz % 4 == 0 else 1)
        x = _output(x, o_a, o_b, mod3, g_post[layer], w_out[layer].astype(BF16), tm)
    return x
```

```python
import functools
import math

import jax
import jax.numpy as jnp
from jax import lax
from jax.experimental import pallas as pl
from jax.experimental.pallas import tpu as pltpu

F32 = jnp.float32
BF16 = jnp.bfloat16
I32 = jnp.int32
HIGHEST = lax.Precision.HIGHEST

H_A, DK_A, DV_A = 4, 128, 128
W_A = H_A * DV_A
CONV_W = 4
CHUNK = 64
H_B, R_KV, DV_B = 4, 128, 128
W_B = H_B * DV_B
H_IDX, D_IDX = 8, 64
TOPK_MAX = 256
N_BUCKETS, MAX_EXACT, MAX_DIST = 32, 16, 128
EPS = 1e-6
LOG2E = math.log2(math.e)

LANES = 128
TQ = 128
VMEM_LIMIT = 58 * 1024 * 1024

C_QKV = (0, 3 * W_A)
C_ZA = (C_QKV[1], C_QKV[1] + W_A)
C_BA = (C_ZA[1], C_ZA[1] + LANES)
C_QB = (C_BA[1], C_BA[1] + H_B * R_KV)
C_CKV = (C_QB[1], C_QB[1] + R_KV)
C_ZB = (C_CKV[1], C_CKV[1] + W_B)
C_IQ = (C_ZB[1], C_ZB[1] + H_IDX * D_IDX)
C_IKW = (C_IQ[1], C_IQ[1] + LANES)
D_IN_PAD = C_IKW[1]
IW_LANE = D_IDX

NEG_INF_KEY = -2139095041


def _sigmoid(x):
    return 1.0 / (1.0 + jnp.exp(-x))


def _silu(x):
    return x * _sigmoid(x)


def _softplus(x):
    return jnp.maximum(x, 0.0) + jnp.log1p(jnp.exp(-jnp.abs(x)))


def _dot(a, b, precision=None):
    return jnp.dot(a, b, precision=precision, preferred_element_type=F32)


def _dot_nt(a, b, precision=None):
    return lax.dot_general(a, b, (((1,), (1,)), ((), ())), precision=precision,
                           preferred_element_type=F32)


def _dot_tn(a, b, precision=None):
    return lax.dot_general(a, b, (((0,), (0,)), ((), ())), precision=precision,
                           preferred_element_type=F32)


def _bdot(a, b, precision=None):
    return _dot(a.astype(BF16), b.astype(BF16))


def _bdot_nt(a, b, precision=None):
    return _dot_nt(a.astype(BF16), b.astype(BF16))


def _bdot_tn(a, b, precision=None):
    return _dot_tn(a.astype(BF16), b.astype(BF16))


def _mod_kernel(c_ref, w_ref, b_ref, o_ref):
    c = c_ref[...]
    o_ref[...] = _dot(_silu(c), w_ref[...], HIGHEST) + b_ref[...]


def _modulation(c, w_ada, b_ada):
    bsz, d = c.shape
    n = w_ada.shape[1]
    tn = 512
    return pl.pallas_call(
        _mod_kernel,
        grid=(n // tn,),
        in_specs=[pl.BlockSpec((bsz, d), lambda j: (0, 0)),
                  pl.BlockSpec((d, tn), lambda j: (0, j)),
                  pl.BlockSpec((1, tn), lambda j: (0, j))],
        out_specs=pl.BlockSpec((bsz, tn), lambda j: (0, j)),
        out_shape=jax.ShapeDtypeStruct((bsz, n), F32),
        compiler_params=pltpu.CompilerParams(dimension_semantics=("arbitrary",),
                                             vmem_limit_bytes=VMEM_LIMIT),
        name="mod",
    )(c, w_ada, b_ada.reshape(1, n))


def _proj_kernel(x_ref, mod_ref, g_ref, w_ref,
                 qkv_ref, za_ref, ba_ref, qb_ref, ckv_ref, zb_ref, iq_ref, ikw_ref):
    x = x_ref[0]
    ms = jnp.mean(x * x, axis=-1, keepdims=True)
    xn = x * lax.rsqrt(ms + EPS)
    shift = mod_ref[0, 0:1, :]
    scale = mod_ref[0, 1:2, :]
    h = (xn * g_ref[...]) * (1.0 + scale) + shift
    hb = h.astype(BF16)

    def mm(cols):
        return _dot(hb, w_ref[:, cols[0]:cols[1]])

    qkv_ref[0] = mm(C_QKV)
    za_ref[0] = mm(C_ZA)
    ba_ref[0] = mm(C_BA)
    qb_ref[0] = mm(C_QB).astype(BF16)
    ckv_ref[0] = mm(C_CKV)
    zb_ref[0] = mm(C_ZB)
    iq_ref[0] = mm(C_IQ).astype(BF16)
    ikw_ref[0] = mm(C_IKW)


def _projection(x, mod3, g_pre, w_pad, tm):
    bsz, seq, d = x.shape
    widths = [(C_QKV, F32), (C_ZA, F32), (C_BA, F32), (C_QB, BF16),
              (C_CKV, F32), (C_ZB, F32), (C_IQ, BF16), (C_IKW, F32)]
    out_shape = [jax.ShapeDtypeStruct((bsz, seq, c[1] - c[0]), dt) for c, dt in widths]
    out_specs = [pl.BlockSpec((1, tm, c[1] - c[0]), lambda b, i: (b, i, 0)) for c, _ in widths]
    return pl.pallas_call(
        _proj_kernel,
        grid=(bsz, seq // tm),
        in_specs=[pl.BlockSpec((1, tm, d), lambda b, i: (b, i, 0)),
                  pl.BlockSpec((1, 3, d), lambda b, i: (b, 0, 0)),
                  pl.BlockSpec((1, d), lambda b, i: (0, 0)),
                  pl.BlockSpec((d, D_IN_PAD), lambda b, i: (0, 0))],
        out_specs=out_specs,
        out_shape=out_shape,
        compiler_params=pltpu.CompilerParams(dimension_semantics=("arbitrary", "arbitrary"),
                                             vmem_limit_bytes=VMEM_LIMIT),
        name="proj",
    )(x, mod3, g_pre.reshape(1, d), w_pad)


def _tri_inverse_many(lmats, level_masks, eye):
    ts = [eye - jnp.where(level_masks[0], lm, 0.0) for lm in lmats]
    for m in level_masks[1:]:
        xs = [_bdot(jnp.where(m, lm, 0.0), t) for lm, t in zip(lmats, ts)]
        ts = [t - _bdot(t, x) for t, x in zip(ts, xs)]
    return ts


def _gdn_kernel(qkv_ref, za_ref, ba_ref, cw_ref, alog_ref, dtb_ref, gn_ref, o_ref,
                ext_ref, s_ref, gc_ref, beta_ref, gct_ref, u_ref, wq_ref, kg_ref, at_ref,
                *, rb, tl, group_a):
    n_ch = tl // CHUNK
    li = pl.program_id(1)

    @pl.when(li == 0)
    def _():
        ext_ref[:, 0:8, :] = jnp.zeros((rb, 8, 3 * W_A), F32)
        s_ref[...] = jnp.zeros_like(s_ref)

    @pl.when(li > 0)
    def _():
        ext_ref[:, 0:8, :] = ext_ref[:, tl:tl + 8, :]

    ext_ref[:, 8:8 + tl, :] = qkv_ref[...]

    r_i = lax.broadcasted_iota(I32, (tl, tl), 0)
    c_i = lax.broadcasted_iota(I32, (tl, tl), 1)
    shift = CHUNK.bit_length() - 1
    tri = jnp.where((c_i <= r_i) & ((r_i >> shift) == (c_i >> shift)), 1.0, 0.0)
    for r in range(rb):
        ba = ba_ref[r]
        beta_ref[r] = _sigmoid(ba)
        g = -jnp.exp(alog_ref[...]) * _softplus(ba + dtb_ref[...])
        gc = _dot(tri, g, HIGHEST)
        gc_ref[r] = gc
        gct = gc.T
        for c in range(n_ch):
            gct_ref[r * n_ch + c] = gct[:, c * CHUNK:(c + 1) * CHUNK]

    row = lax.broadcasted_iota(I32, (CHUNK, CHUNK), 0)
    col = lax.broadcasted_iota(I32, (CHUNK, CHUNK), 1)
    causal = row >= col
    strict = row > col
    eye = jnp.where(row == col, 1.0, 0.0)
    level_masks = []
    s = 1
    while s < CHUNK:
        ls = s.bit_length() - 1
        level_masks.append(((row >> (ls + 1)) == (col >> (ls + 1)))
                           & (((row >> ls) & 1) == 1) & (((col >> ls) & 1) == 0))
        s *= 2

    def phase_a(ig, carry):
        probs = []
        for j in range(group_a):
            it = ig * group_a + j
            r = it // n_ch
            c = it - r * n_ch
            base = pl.multiple_of(c * CHUNK, CHUNK)
            gc_c = gc_ref[r, pl.ds(base, CHUNK), :]
            beta_c = beta_ref[r, pl.ds(base, CHUNK), :]
            gct_c = gct_ref[it]
            for h in range(H_A):
                def conv_silu(sec):
                    c0 = sec * W_A + h * DK_A
                    win = ext_ref[r, pl.ds(base, CHUNK + 8), c0:c0 + DK_A]
                    acc = None
                    for t in range(CONV_W):
                        sh = CONV_W - 1 - t
                        u = win if sh == 0 else pltpu.roll(win, sh, axis=0)
                        term = u[8:8 + CHUNK] * cw_ref[t:t + 1, c0:c0 + DK_A]
                        acc = term if acc is None else acc + term
                    return _silu(acc)

                q = conv_silu(0)
                k = conv_silu(1)
                v = conv_silu(2)
                q = q * lax.rsqrt(jnp.sum(q * q, axis=-1, keepdims=True) + EPS) * (DK_A ** -0.5)
                k = k * lax.rsqrt(jnp.sum(k * k, axis=-1, keepdims=True) + EPS)
                beta = beta_c[:, h:h + 1]
                gcol = gc_c[:, H_A + h:H_A + h + 1]
                grow = gct_c[H_A + h:H_A + h + 1, :]
                glast = gc_c[CHUNK - 1:CHUNK, H_A + h:H_A + h + 1]
                decay = jnp.exp(jnp.where(causal, gcol - grow, -jnp.inf))
                ecol = jnp.exp(gcol)
                kb = k * beta
                hc = slice(h * DV_A, (h + 1) * DV_A)
                wq_ref[it, h, CHUNK:2 * CHUNK, :] = (q * ecol).astype(BF16)
                kg_ref[r, pl.ds(base, CHUNK), hc] = (k * jnp.exp(glast - gcol)).astype(BF16)
                a2 = _bdot_nt(jnp.concatenate([kb, q], axis=0), k)
                at_ref[it, h] = (a2[CHUNK:] * decay).astype(BF16)
                probs.append(dict(
                    r=r, it=it, h=h, base=base, hc=hc,
                    lmat=jnp.where(strict, a2[:CHUNK] * decay, 0.0),
                    rhs=jnp.concatenate([v * beta, kb * ecol], axis=1).astype(BF16)))
        tmats = _tri_inverse_many([p["lmat"] for p in probs], level_masks, eye)
        for p, tmat in zip(probs, tmats):
            uw = _dot(tmat.astype(BF16), p["rhs"])
            u_ref[p["r"], pl.ds(p["base"], CHUNK), p["hc"]] = uw[:, :DV_A]
            wq_ref[p["it"], p["h"], 0:CHUNK, :] = uw[:, DV_A:].astype(BF16)
        return carry

    lax.fori_loop(0, rb * n_ch // group_a, phase_a, 0)

    def phase_b(c, carry):
        base = pl.multiple_of(c * CHUNK, CHUNK)
        chains = [(r, h) for r in range(rb) for h in range(H_A)]
        hcs = [slice(h * DV_A, (h + 1) * DV_A) for _, h in chains]
        sts = [s_ref[r, h] for r, h in chains]
        wss = [_dot(wq_ref[r * n_ch + c, h], st.astype(BF16)) for (r, h), st in zip(chains, sts)]
        vnbs = [(u_ref[r, pl.ds(base, CHUNK), hc] - ws[:CHUNK]).astype(BF16)
                for (r, h), hc, ws in zip(chains, hcs, wss)]
        upds = [_dot_tn(kg_ref[r, pl.ds(base, CHUNK), hc], vnb)
                for (r, h), hc, vnb in zip(chains, hcs, vnbs)]
        for (r, h), st, upd in zip(chains, sts, upds):
            glast = gc_ref[r, pl.ds(base + CHUNK - 1, 1), :][:, H_A + h:H_A + h + 1]
            s_ref[r, h] = st * jnp.exp(glast) + upd
        for (r, h), hc, ws, vnb in zip(chains, hcs, wss, vnbs):
            o = ws[CHUNK:] + _dot(at_ref[r * n_ch + c, h], vnb)
            on = o * lax.rsqrt(jnp.mean(o * o, axis=-1, keepdims=True) + EPS) * gn_ref[...]
            z = za_ref[r, pl.ds(base, CHUNK), hc]
            o_ref[r, pl.ds(base, CHUNK), hc] = (on * _silu(z)).astype(BF16)
        return carry

    lax.fori_loop(0, n_ch, phase_b, 0)


def _gated_deltanet(qkv, za, ba, conv_w, alog_vec, dtb_vec, g_norm, rb, tl, group_a):
    bsz, seq, _ = qkv.shape
    n_ch = tl // CHUNK
    kern = functools.partial(_gdn_kernel, rb=rb, tl=tl, group_a=group_a)
    return pl.pallas_call(
        kern,
        grid=(bsz // rb, seq // tl),
        in_specs=[pl.BlockSpec((rb, tl, 3 * W_A), lambda b, i: (b, i, 0)),
                  pl.BlockSpec((rb, tl, W_A), lambda b, i: (b, i, 0)),
                  pl.BlockSpec((rb, tl, LANES), lambda b, i: (b, i, 0)),
                  pl.BlockSpec((CONV_W, 3 * W_A), lambda b, i: (0, 0)),
                  pl.BlockSpec((1, LANES), lambda b, i: (0, 0)),
                  pl.BlockSpec((1, LANES), lambda b, i: (0, 0)),
                  pl.BlockSpec((1, DV_A), lambda b, i: (0, 0))],
        out_specs=pl.BlockSpec((rb, tl, W_A), lambda b, i: (b, i, 0)),
        out_shape=jax.ShapeDtypeStruct((bsz, seq, W_A), BF16),
        scratch_shapes=[pltpu.VMEM((rb, tl + 8, 3 * W_A), F32),
                        pltpu.VMEM((rb, H_A, DK_A, DV_A), F32),
                        pltpu.VMEM((rb, tl, LANES), F32),
                        pltpu.VMEM((rb, tl, LANES), F32),
                        pltpu.VMEM((rb * n_ch, LANES, CHUNK), F32),
                        pltpu.VMEM((rb, tl, W_A), F32),
                        pltpu.VMEM((rb * n_ch, H_A, 2 * CHUNK, DK_A), BF16),
                        pltpu.VMEM((rb, tl, W_A), BF16),
                        pltpu.VMEM((rb * n_ch, H_A, CHUNK, CHUNK), BF16)],
        compiler_params=pltpu.CompilerParams(dimension_semantics=("arbitrary", "arbitrary"),
                                             vmem_limit_bytes=VMEM_LIMIT),
        name="gdn",
    )(qkv, za, ba, conv_w, alog_vec, dtb_vec, g_norm.reshape(1, DV_A))


def _bias_kernel(rb_ref, o_ref):
    d = pl.program_id(0)
    kj = lax.broadcasted_iota(I32, (TQ, TQ), 0)
    qi = lax.broadcasted_iota(I32, (TQ, TQ), 1)
    dist = d * TQ + qi - kj
    n = jnp.maximum(dist, 0)
    nf = jnp.maximum(n, 1).astype(F32)
    large = MAX_EXACT + (jnp.log(nf / MAX_EXACT) / math.log(MAX_DIST / MAX_EXACT)
                         * (N_BUCKETS - MAX_EXACT)).astype(I32)
    large = jnp.minimum(large, N_BUCKETS - 1)
    bucket = jnp.where(n < MAX_EXACT, n, large)
    for h in range(H_B):
        acc = jnp.zeros((TQ, TQ), F32)
        for kb in range(N_BUCKETS):
            acc = jnp.where(bucket == kb, rb_ref[kb, h], acc)
        o_ref[0, :, h * TQ:(h + 1) * TQ] = acc * LOG2E


def _bias_tiles(rel_bias, n_diag):
    return pl.pallas_call(
        _bias_kernel,
        grid=(n_diag,),
        in_specs=[pl.BlockSpec(memory_space=pltpu.SMEM)],
        out_specs=pl.BlockSpec((1, TQ, H_B * TQ), lambda d: (d, 0, 0)),
        out_shape=jax.ShapeDtypeStruct((n_diag, TQ, H_B * TQ), F32),
        compiler_params=pltpu.CompilerParams(dimension_semantics=("arbitrary",)),
        name="bias",
    )(rel_bias)


def _fold_keys(x, op):
    x = op(x.reshape(4, TQ // 32, 8, x.shape[-1]), axis=1)
    return op(x, axis=0)


def _dsa_kernel(iq_ref, ikwq_ref, qb_ref, zb_ref, ikw_ref, ckv_ref, gkv_ref, wuv_ref, bias_ref,
                o_ref,
                kvn_ref, kvt_ref, iklo_ref, ikhi_ref, key_ref, am_ref, lg_ref, acc_ref,
                *, rb, seq, k_top):
    qi = pl.program_id(1)
    n_kc = qi + 1
    hq = H_B * TQ
    rows = range(rb)

    @pl.when(qi == 0)
    def _():
        for r in rows:
            for c in range(seq // TQ):
                sl = slice(c * TQ, (c + 1) * TQ)
                ckv = ckv_ref[r, sl, :]
                ms = jnp.mean(ckv * ckv, axis=-1, keepdims=True)
                kvn = (ckv * lax.rsqrt(ms + EPS)) * gkv_ref[...]
                kvn_ref[r, sl, :] = kvn.astype(BF16)
                kvt_ref[r, c] = kvn.T.astype(BF16)
                ikw = ikw_ref[r, sl, :]
                lane = lax.broadcasted_iota(I32, ikw.shape, 1)
                lo = jnp.where(lane < D_IDX, ikw, 0.0)
                iklo_ref[r, sl, :] = lo.astype(BF16)
                ikhi_ref[r, sl, :] = pltpu.roll(lo, D_IDX, axis=1).astype(BF16)

    iq4s, iwts = [], []
    for r in rows:
        iq = iq_ref[r]
        iq4s.append(jnp.concatenate(
            [iq[:, p * LANES:(p + 1) * LANES] for p in range(H_IDX // 2)], axis=0))
        iwts.append((ikwq_ref[r] * (H_IDX ** -0.5 * D_IDX ** -0.5)).T)

    key_j = lax.broadcasted_iota(I32, (TQ, TQ), 0)
    qry_t = qi * TQ + lax.broadcasted_iota(I32, (TQ, TQ), 1)

    def causal_mask(c):
        return (c * TQ + key_j) <= qry_t

    q4s = []
    for r in rows:
        qb = qb_ref[r]
        q4s.append(jnp.concatenate([qb[:, h * R_KV:(h + 1) * R_KV] for h in range(H_B)], axis=0))
    scale = R_KV ** -0.5 * LOG2E

    def pair_loop(body, carry):
        carry = lax.fori_loop(0, n_kc >> 1, lambda i, cr: body((2 * i, 2 * i + 1), cr), carry)
        return lax.cond((n_kc & 1) == 1, lambda cr: body((n_kc - 1,), cr), lambda cr: cr, carry)

    def score_chunks(cs, m0s):
        m0s = list(m0s)
        nc = len(cs)
        ks = pl.ds(pl.multiple_of(cs[0] * TQ, TQ), nc * TQ)
        rels = [_dot_nt(jnp.concatenate([iklo_ref[r, ks, :], ikhi_ref[r, ks, :]], axis=0),
                        iq4s[r]) for r in rows]
        sts = [_dot_nt(kvn_ref[r, ks, :], q4s[r]) for r in rows]
        for j, c in enumerate(cs):
            for r in rows:
                re = rels[r][j * TQ:(j + 1) * TQ]
                ro = rels[r][(nc + j) * TQ:(nc + j + 1) * TQ]
                s = None
                for p in range(H_IDX // 2):
                    ps = slice(p * TQ, (p + 1) * TQ)
                    we = iwts[r][IW_LANE + 2 * p:IW_LANE + 2 * p + 1, :]
                    wo = iwts[r][IW_LANE + 2 * p + 1:IW_LANE + 2 * p + 2, :]
                    t = jnp.maximum(re[:, ps], 0.0) * we + jnp.maximum(ro[:, ps], 0.0) * wo
                    s = t if s is None else s + t
                s = jnp.where(s == 0.0, 0.0, s)
                s = jnp.where(causal_mask(c), s, -jnp.inf)
                bits = pltpu.bitcast(s, I32)
                key_ref[r, c] = jnp.where(bits < 0, bits ^ 0x7FFFFFFF, bits)
                lg = sts[r][j * TQ:(j + 1) * TQ] * scale + bias_ref[qi - c]
                lg_ref[r, c] = lg
                m0s[r] = jnp.maximum(m0s[r], _fold_keys(lg, jnp.max))
        return tuple(m0s)

    m0s = pair_loop(score_chunks, tuple(jnp.full((8, hq), -jnp.inf, F32) for _ in rows))

    kf = float(k_top)
    n_pairs = (n_kc + 1) >> 1
    sign = jnp.int32(-2 ** 31)

    @pl.when((n_kc & 1) == 1)
    def _():
        for r in rows:
            key_ref[r, n_kc] = jnp.full((TQ, TQ), -2 ** 31, I32)

    def count_ge(cands):
        def body(i, accs):
            out = list(accs)
            for c in (2 * i, 2 * i + 1):
                for r in rows:
                    hit = jnp.where(key_ref[r, c] >= cands[r], 1.0, 0.0)
                    out[r] = out[r] + _fold_keys(hit, jnp.sum)
            return tuple(out)
        accs = lax.fori_loop(0, n_pairs, body, tuple(jnp.zeros((8, TQ), F32) for _ in rows))
        return [jnp.sum(a, axis=0, keepdims=True) for a in accs]

    def bit_body(i, tus):
        bit = lax.shift_left(jnp.int32(1), 31 - i)
        cands = [tu | bit for tu in tus]
        cnts = count_ge([(cu ^ sign)[0:1, :] for cu in cands])
        return tuple(jnp.where(cnt >= kf, cu, tu) for cnt, cu, tu in zip(cnts, cands, tus))

    tus = lax.fori_loop(0, 32, bit_body, tuple(jnp.zeros((8, TQ), I32) for _ in rows))
    thrs = [(tu ^ sign)[0:1, :] for tu in tus]

    def count_both(c, carry):
        out = []
        for r in rows:
            cg, ce = carry[r]
            key = key_ref[r, c]
            out.append((cg + _fold_keys(jnp.where(key > thrs[r], 1.0, 0.0), jnp.sum),
                        ce + _fold_keys(jnp.where(key == thrs[r], 1.0, 0.0), jnp.sum)))
        return tuple(out)

    zero8 = jnp.zeros((8, TQ), F32)
    cges = lax.fori_loop(0, n_kc, count_both, tuple((zero8, zero8) for _ in rows))
    qrow = qi * TQ + lax.broadcasted_iota(I32, (1, TQ), 1)
    needs, simple_all = [], None
    for r in rows:
        cnt_gt = jnp.sum(cges[r][0], axis=0, keepdims=True)
        cnt_eq = jnp.sum(cges[r][1], axis=0, keepdims=True)
        need = kf - cnt_gt
        needs.append(need)
        simple = (cnt_eq <= need) | ((thrs[r] == NEG_INF_KEY) & (qrow < k_top))
        simple_all = simple if simple_all is None else (simple_all & simple)
    all_simple = jnp.min(jnp.where(simple_all, 1.0, 0.0)) > 0.5

    @pl.when(all_simple)
    def _():
        def body(c, carry):
            cm = causal_mask(c)
            for r in rows:
                sel = (key_ref[r, c] >= thrs[r]) & cm
                am_ref[r, c] = jnp.where(sel, 0.0, -jnp.inf)
            return carry
        lax.fori_loop(0, n_kc, body, 0)

    @pl.when(jnp.logical_not(all_simple))
    def _():
        lower = jnp.where(lax.broadcasted_iota(I32, (TQ, TQ), 1)
                          <= lax.broadcasted_iota(I32, (TQ, TQ), 0), 1.0, 0.0).astype(BF16)

        def body(c, seens):
            cm = causal_mask(c)
            out = []
            for r in rows:
                key = key_ref[r, c]
                eq = key == thrs[r]
                eqf = jnp.where(eq, 1.0, 0.0)
                rank = seens[r] + _dot(lower, eqf.astype(BF16))
                sel = (key > thrs[r]) | (eq & (rank <= needs[r]))
                am_ref[r, c] = jnp.where(sel & cm, 0.0, -jnp.inf)
                out.append(seens[r] + jnp.sum(eqf, axis=0, keepdims=True))
            return tuple(out)
        lax.fori_loop(0, n_kc, body, tuple(jnp.zeros((1, TQ), F32) for _ in rows))

    def masked_logits(r, c):
        return lg_ref[r, c] + jnp.concatenate([am_ref[r, c]] * H_B, axis=1)

    def softmax_pv(shifts):
        acc_ref[...] = jnp.zeros_like(acc_ref)

        def body(cs, ls):
            out = list(ls)
            for r in rows:
                ps = [jnp.exp2(masked_logits(r, c) - shifts[r]) for c in cs]
                kvt = jnp.concatenate([kvt_ref[r, c] for c in cs], axis=1)
                acc_ref[r] += _dot(kvt, jnp.concatenate(ps, axis=0).astype(BF16))
                for p in ps:
                    out[r] = out[r] + _fold_keys(p, jnp.sum)
            return tuple(out)

        ls = pair_loop(body, tuple(jnp.zeros((8, hq), F32) for _ in rows))
        return tuple(jnp.sum(l, axis=0, keepdims=True) for l in ls)

    lsums = softmax_pv([jnp.max(m, axis=0, keepdims=True) for m in m0s])
    lmin = jnp.min(jnp.concatenate(lsums, axis=1))

    def exact_shift(_):
        def body(cs, ms):
            out = list(ms)
            for c in cs:
                for r in rows:
                    out[r] = jnp.maximum(out[r], _fold_keys(masked_logits(r, c), jnp.max))
            return tuple(out)
        ms = pair_loop(body, tuple(jnp.full((8, hq), -jnp.inf, F32) for _ in rows))
        return softmax_pv([jnp.max(m, axis=0, keepdims=True) for m in ms])

    lsums = lax.cond(lmin >= 2.0 ** -40, lambda _: lsums, exact_shift, 0)
    for r in rows:
        ot = (acc_ref[r] / lsums[r]).astype(BF16)
        for h in range(H_B):
            hs = slice(h * DV_B, (h + 1) * DV_B)
            y = _dot_tn(ot[:, h * TQ:(h + 1) * TQ], wuv_ref[h])
            o_ref[r, :, hs] = (y * _silu(zb_ref[r, :, hs])).astype(BF16)


def _dsa_attention(iq, ikw, qb, zb, ckv, g_kv, w_uv_bf16, bias_tiles, k_top, rb):
    bsz, seq, _ = iq.shape
    n_q = seq // TQ
    hq = H_B * TQ
    kern = functools.partial(_dsa_kernel, rb=rb, seq=seq, k_top=k_top)
    return pl.pallas_call(
        kern,
        grid=(bsz // rb, n_q),
        in_specs=[pl.BlockSpec((rb, TQ, H_IDX * D_IDX), lambda b, i: (b, i, 0)),
                  pl.BlockSpec((rb, TQ, LANES), lambda b, i: (b, i, 0)),
                  pl.BlockSpec((rb, TQ, H_B * R_KV), lambda b, i: (b, i, 0)),
                  pl.BlockSpec((rb, TQ, W_B), lambda b, i: (b, i, 0)),
                  pl.BlockSpec((rb, seq, LANES), lambda b, i: (b, 0, 0),
                               pipeline_mode=pl.Buffered(1)),
                  pl.BlockSpec((rb, seq, R_KV), lambda b, i: (b, 0, 0),
                               pipeline_mode=pl.Buffered(1)),
                  pl.BlockSpec((1, R_KV), lambda b, i: (0, 0)),
                  pl.BlockSpec((H_B, R_KV, DV_B), lambda b, i: (0, 0, 0)),
                  pl.BlockSpec((n_q, TQ, hq), lambda b, i: (0, 0, 0),
                               pipeline_mode=pl.Buffered(1))],
        out_specs=pl.BlockSpec((rb, TQ, W_B), lambda b, i: (b, i, 0)),
        out_shape=jax.ShapeDtypeStruct((bsz, seq, W_B), BF16),
        scratch_shapes=[pltpu.VMEM((rb, seq, R_KV), BF16),
                        pltpu.VMEM((rb, n_q, R_KV, TQ), BF16),
                        pltpu.VMEM((rb, seq, LANES), BF16),
                        pltpu.VMEM((rb, seq, LANES), BF16),
                        pltpu.VMEM((rb, n_q + 1, TQ, TQ), I32),
                        pltpu.VMEM((rb, n_q, TQ, TQ), F32),
                        pltpu.VMEM((rb, n_q, TQ, hq), F32),
                        pltpu.VMEM((rb, R_KV, hq), F32)],
        compiler_params=pltpu.CompilerParams(dimension_semantics=("arbitrary", "arbitrary"),
                                             vmem_limit_bytes=VMEM_LIMIT),
        name="dsa",
    )(iq, ikw, qb, zb, ikw, ckv, g_kv.reshape(1, R_KV), w_uv_bf16, bias_tiles)


def _out_kernel(x_ref, oa_ref, ob_ref, mod_ref, g_ref, w_ref, o_ref):
    mix_in = jnp.concatenate([oa_ref[0], ob_ref[0]], axis=-1)
    mix = _dot(mix_in, w_ref[...])
    ms = jnp.mean(mix * mix, axis=-1, keepdims=True)
    normed = (mix * lax.rsqrt(ms + EPS)) * g_ref[...]
    o_ref[0] = x_ref[0] + mod_ref[0, 2:3, :] * normed


def _output(x, o_a, o_b, mod3, g_post, w_out_bf16, tm):
    bsz, seq, d = x.shape
    return pl.pallas_call(
        _out_kernel,
        grid=(bsz, seq // tm),
        in_specs=[pl.BlockSpec((1, tm, d), lambda b, i: (b, i, 0)),
                  pl.BlockSpec((1, tm, W_A), lambda b, i: (b, i, 0)),
                  pl.BlockSpec((1, tm, W_B), lambda b, i: (b, i, 0)),
                  pl.BlockSpec((1, 3, d), lambda b, i: (b, 0, 0)),
                  pl.BlockSpec((1, d), lambda b, i: (0, 0)),
                  pl.BlockSpec((W_A + W_B, d), lambda b, i: (0, 0))],
        out_specs=pl.BlockSpec((1, tm, d), lambda b, i: (b, i, 0)),
        out_shape=jax.ShapeDtypeStruct((bsz, seq, d), F32),
        compiler_params=pltpu.CompilerParams(dimension_semantics=("arbitrary", "arbitrary"),
                                             vmem_limit_bytes=VMEM_LIMIT),
        name="outproj",
    )(x, o_a, o_b, mod3, g_post.reshape(1, d), w_out_bf16)


def _pad_lanes(w):
    return jnp.pad(w, ((0, 0), (0, LANES - w.shape[1])))


def _pad_input_projection(w_in):
    o_ba = 3 * W_A + W_A
    o_qb = o_ba + 2 * H_A
    o_ik = o_qb + H_B * R_KV + R_KV + W_B + H_IDX * D_IDX
    return jnp.concatenate([w_in[:, :o_ba], _pad_lanes(w_in[:, o_ba:o_qb]),
                            w_in[:, o_qb:o_ik], _pad_lanes(w_in[:, o_ik:])], axis=1)


def _head_lanes(v):
    return jnp.zeros((1, LANES), F32).at[0, H_A:2 * H_A].set(v.astype(F32))


def kernel(x, c, w_ada, b_ada, g_pre, w_in, conv_w, a_log, dt_bias, g_gdn, g_kv, w_uv, rel_bias, w_out, g_post):
    bsz, seq, d = x.shape
    depth = w_ada.shape[0]
    assert seq % TQ == 0 and seq % CHUNK == 0
    k_top = min(TOPK_MAX, seq // 4)
    tm = min(512, seq)
    tl = min(128, seq)
    rb = 8 if bsz % 8 == 0 else (2 if bsz % 2 == 0 else 1)
    bias_tiles = _bias_tiles(rel_bias, seq // TQ)
    for layer in range(depth):
        mod3 = _modulation(c, w_ada[layer], b_ada[layer]).reshape(bsz, 3, d)
        w_pad = _pad_input_projection(w_in[layer]).astype(BF16)
        qkv, za, ba, qb, ckv, zb, iq, ikw = _projection(x, mod3, g_pre[layer], w_pad, tm)
        o_a = _gated_deltanet(qkv, za, ba, conv_w[layer], _head_lanes(a_log[layer]),
                              _head_lanes(dt_bias[layer]), g_gdn[layer], rb, tl, math.gcd(8, rb * (tl // CHUNK)))
        o_b = _dsa_attention(iq, ikw, qb, zb, ckv, g_kv[layer], w_uv[layer].astype(BF16),
                             bias_tiles, k_top, 4 if bsz % 4 == 0 else 1)
        x = _output(x, o_a, o_b, mod3, g_post[layer], w_out[layer].astype(BF16), tm)
    return x
```

```python
import functools
import math

import jax
import jax.numpy as jnp
from jax import lax
from jax.experimental import pallas as pl
from jax.experimental.pallas import tpu as pltpu

F32 = jnp.float32
BF16 = jnp.bfloat16
I32 = jnp.int32
HIGHEST = lax.Precision.HIGHEST

H_A, DK_A, DV_A = 4, 128, 128
W_A = H_A * DV_A
CONV_W = 4
CHUNK = 64
H_B, R_KV, DV_B = 4, 128, 128
W_B = H_B * DV_B
H_IDX, D_IDX = 8, 64
TOPK_MAX = 256
N_BUCKETS, MAX_EXACT, MAX_DIST = 32, 16, 128
EPS = 1e-6
LOG2E = math.log2(math.e)

LANES = 128
TQ = 128
VMEM_LIMIT = 58 * 1024 * 1024

C_QKV = (0, 3 * W_A)
C_ZA = (C_QKV[1], C_QKV[1] + W_A)
C_BA = (C_ZA[1], C_ZA[1] + LANES)
C_QB = (C_BA[1], C_BA[1] + H_B * R_KV)
C_CKV = (C_QB[1], C_QB[1] + R_KV)
C_ZB = (C_CKV[1], C_CKV[1] + W_B)
C_IQ = (C_ZB[1], C_ZB[1] + H_IDX * D_IDX)
C_IKW = (C_IQ[1], C_IQ[1] + LANES)
D_IN_PAD = C_IKW[1]
IW_LANE = D_IDX

NEG_INF_KEY = -2139095041


def _sigmoid(x):
    return 1.0 / (1.0 + jnp.exp(-x))


def _silu(x):
    return x * _sigmoid(x)


def _softplus(x):
    return jnp.maximum(x, 0.0) + jnp.log1p(jnp.exp(-jnp.abs(x)))


def _dot(a, b, precision=None):
    return jnp.dot(a, b, precision=precision, preferred_element_type=F32)


def _dot_nt(a, b, precision=None):
    return lax.dot_general(a, b, (((1,), (1,)), ((), ())), precision=precision,
                           preferred_element_type=F32)


def _dot_tn(a, b, precision=None):
    return lax.dot_general(a, b, (((0,), (0,)), ((), ())), precision=precision,
                           preferred_element_type=F32)


def _bdot(a, b, precision=None):
    return _dot(a.astype(BF16), b.astype(BF16))


def _bdot_nt(a, b, precision=None):
    return _dot_nt(a.astype(BF16), b.astype(BF16))


def _bdot_tn(a, b, precision=None):
    return _dot_tn(a.astype(BF16), b.astype(BF16))


def _mod_kernel(c_ref, w_ref, b_ref, o_ref):
    c = c_ref[...]
    o_ref[...] = _dot(_silu(c), w_ref[...], HIGHEST) + b_ref[...]


def _modulation(c, w_ada, b_ada):
    bsz, d = c.shape
    n = w_ada.shape[1]
    tn = 512
    return pl.pallas_call(
        _mod_kernel,
        grid=(n // tn,),
        in_specs=[pl.BlockSpec((bsz, d), lambda j: (0, 0)),
                  pl.BlockSpec((d, tn), lambda j: (0, j)),
                  pl.BlockSpec((1, tn), lambda j: (0, j))],
        out_specs=pl.BlockSpec((bsz, tn), lambda j: (0, j)),
        out_shape=jax.ShapeDtypeStruct((bsz, n), F32),
        compiler_params=pltpu.CompilerParams(dimension_semantics=("arbitrary",),
                                             vmem_limit_bytes=VMEM_LIMIT),
        name="mod",
    )(c, w_ada, b_ada.reshape(1, n))


def _proj_kernel(x_ref, mod_ref, g_ref, w_ref,
                 qkv_ref, za_ref, ba_ref, qb_ref, ckv_ref, zb_ref, iq_ref, ikw_ref):
    x = x_ref[0]
    ms = jnp.mean(x * x, axis=-1, keepdims=True)
    xn = x * lax.rsqrt(ms + EPS)
    shift = mod_ref[0, 0:1, :]
    scale = mod_ref[0, 1:2, :]
    h = (xn * g_ref[...]) * (1.0 + scale) + shift
    hb = h.astype(BF16)

    def mm(cols):
        return _dot(hb, w_ref[:, cols[0]:cols[1]])

    qkv_ref[0] = mm(C_QKV)
    za_ref[0] = mm(C_ZA)
    ba_ref[0] = mm(C_BA)
    qb_ref[0] = mm(C_QB).astype(BF16)
    ckv_ref[0] = mm(C_CKV)
    zb_ref[0] = mm(C_ZB)
    iq_ref[0] = mm(C_IQ).astype(BF16)
    ikw_ref[0] = mm(C_IKW)


def _projection(x, mod3, g_pre, w_pad, tm):
    bsz, seq, d = x.shape
    widths = [(C_QKV, F32), (C_ZA, F32), (C_BA, F32), (C_QB, BF16),
              (C_CKV, F32), (C_ZB, F32), (C_IQ, BF16), (C_IKW, F32)]
    out_shape = [jax.ShapeDtypeStruct((bsz, seq, c[1] - c[0]), dt) for c, dt in widths]
    out_specs = [pl.BlockSpec((1, tm, c[1] - c[0]), lambda b, i: (b, i, 0)) for c, _ in widths]
    return pl.pallas_call(
        _proj_kernel,
        grid=(bsz, seq // tm),
        in_specs=[pl.BlockSpec((1, tm, d), lambda b, i: (b, i, 0)),
                  pl.BlockSpec((1, 3, d), lambda b, i: (b, 0, 0)),
                  pl.BlockSpec((1, d), lambda b, i: (0, 0)),
                  pl.BlockSpec((d, D_IN_PAD), lambda b, i: (0, 0))],
        out_specs=out_specs,
        out_shape=out_shape,
        compiler_params=pltpu.CompilerParams(dimension_semantics=("arbitrary", "arbitrary"),
                                             vmem_limit_bytes=VMEM_LIMIT),
        name="proj",
    )(x, mod3, g_pre.reshape(1, d), w_pad)


def _tri_inverse_many(lmats, level_masks, eye):
    ts = [eye - jnp.where(level_masks[0], lm, 0.0) for lm in lmats]
    for m in level_masks[1:]:
        xs = [_bdot(jnp.where(m, lm, 0.0), t) for lm, t in zip(lmats, ts)]
        ts = [t - _bdot(t, x) for t, x in zip(ts, xs)]
    return ts


def _gdn_kernel(qkv_ref, za_ref, ba_ref, cw_ref, alog_ref, dtb_ref, gn_ref, o_ref,
                ext_ref, s_ref, gc_ref, beta_ref, gct_ref, u_ref, wq_ref, kg_ref, at_ref,
                *, rb, tl, group_a):
    n_ch = tl // CHUNK
    li = pl.program_id(1)

    @pl.when(li == 0)
    def _():
        ext_ref[:, 0:8, :] = jnp.zeros((rb, 8, 3 * W_A), F32)
        s_ref[...] = jnp.zeros_like(s_ref)

    @pl.when(li > 0)
    def _():
        ext_ref[:, 0:8, :] = ext_ref[:, tl:tl + 8, :]

    ext_ref[:, 8:8 + tl, :] = qkv_ref[...]

    r_i = lax.broadcasted_iota(I32, (tl, tl), 0)
    c_i = lax.broadcasted_iota(I32, (tl, tl), 1)
    shift = CHUNK.bit_length() - 1
    tri = jnp.where((c_i <= r_i) & ((r_i >> shift) == (c_i >> shift)), 1.0, 0.0)
    for r in range(rb):
        ba = ba_ref[r]
        beta_ref[r] = _sigmoid(ba)
        g = -jnp.exp(alog_ref[...]) * _softplus(ba + dtb_ref[...])
        gc = _dot(tri, g, HIGHEST)
        gc_ref[r] = gc
        gct = gc.T
        for c in range(n_ch):
            gct_ref[r * n_ch + c] = gct[:, c * CHUNK:(c + 1) * CHUNK]

    row = lax.broadcasted_iota(I32, (CHUNK, CHUNK), 0)
    col = lax.broadcasted_iota(I32, (CHUNK, CHUNK), 1)
    causal = row >= col
    strict = row > col
    eye = jnp.where(row == col, 1.0, 0.0)
    level_masks = []
    s = 1
    while s < CHUNK:
        ls = s.bit_length() - 1
        level_masks.append(((row >> (ls + 1)) == (col >> (ls + 1)))
                           & (((row >> ls) & 1) == 1) & (((col >> ls) & 1) == 0))
        s *= 2

    def phase_a(ig, carry):
        probs = []
        for j in range(group_a):
            it = ig * group_a + j
            r = it // n_ch
            c = it - r * n_ch
            base = pl.multiple_of(c * CHUNK, CHUNK)
            gc_c = gc_ref[r, pl.ds(base, CHUNK), :]
            beta_c = beta_ref[r, pl.ds(base, CHUNK), :]
            gct_c = gct_ref[it]
            for h in range(H_A):
                def conv_silu(sec):
                    c0 = sec * W_A + h * DK_A
                    win = ext_ref[r, pl.ds(base, CHUNK + 8), c0:c0 + DK_A]
                    acc = None
                    for t in range(CONV_W):
                        sh = CONV_W - 1 - t
                        u = win if sh == 0 else pltpu.roll(win, sh, axis=0)
                        term = u[8:8 + CHUNK] * cw_ref[t:t + 1, c0:c0 + DK_A]
                        acc = term if acc is None else acc + term
                    return _silu(acc)

                q = conv_silu(0)
                k = conv_silu(1)
                v = conv_silu(2)
                q = q * lax.rsqrt(jnp.sum(q * q, axis=-1, keepdims=True) + EPS) * (DK_A ** -0.5)
                k = k * lax.rsqrt(jnp.sum(k * k, axis=-1, keepdims=True) + EPS)
                beta = beta_c[:, h:h + 1]
                gcol = gc_c[:, H_A + h:H_A + h + 1]
                grow = gct_c[H_A + h:H_A + h + 1, :]
                glast = gc_c[CHUNK - 1:CHUNK, H_A + h:H_A + h + 1]
                decay = jnp.exp(jnp.where(causal, gcol - grow, -jnp.inf))
                ecol = jnp.exp(gcol)
                kb = k * beta
                hc = slice(h * DV_A, (h + 1) * DV_A)
                wq_ref[it, h, CHUNK:2 * CHUNK, :] = (q * ecol).astype(BF16)
                kg_ref[r, pl.ds(base, CHUNK), hc] = (k * jnp.exp(glast - gcol)).astype(BF16)
                a2 = _bdot_nt(jnp.concatenate([kb, q], axis=0), k)
                at_ref[it, h] = (a2[CHUNK:] * decay).astype(BF16)
                probs.append(dict(
                    r=r, it=it, h=h, base=base, hc=hc,
                    lmat=jnp.where(strict, a2[:CHUNK] * decay, 0.0),
                    rhs=jnp.concatenate([v * beta, kb * ecol], axis=1).astype(BF16)))
        tmats = _tri_inverse_many([p["lmat"] for p in probs], level_masks, eye)
        for p, tmat in zip(probs, tmats):
            uw = _dot(tmat.astype(BF16), p["rhs"])
            u_ref[p["r"], pl.ds(p["base"], CHUNK), p["hc"]] = uw[:, :DV_A]
            wq_ref[p["it"], p["h"], 0:CHUNK, :] = uw[:, DV_A:].astype(BF16)
        return carry

    lax.fori_loop(0, rb * n_ch // group_a, phase_a, 0)

    def phase_b(c, carry):
        base = pl.multiple_of(c * CHUNK, CHUNK)
        chains = [(r, h) for r in range(rb) for h in range(H_A)]
        hcs = [slice(h * DV_A, (h + 1) * DV_A) for _, h in chains]
        sts = [s_ref[r, h] for r, h in chains]
        wss = [_dot(wq_ref[r * n_ch + c, h], st.astype(BF16)) for (r, h), st in zip(chains, sts)]
        vnbs = [(u_ref[r, pl.ds(base, CHUNK), hc] - ws[:CHUNK]).astype(BF16)
                for (r, h), hc, ws in zip(chains, hcs, wss)]
        upds = [_dot_tn(kg_ref[r, pl.ds(base, CHUNK), hc], vnb)
                for (r, h), hc, vnb in zip(chains, hcs, vnbs)]
        for (r, h), st, upd in zip(chains, sts, upds):
            glast = gc_ref[r, pl.ds(base + CHUNK - 1, 1), :][:, H_A + h:H_A + h + 1]
            s_ref[r, h] = st * jnp.exp(glast) + upd
        for (r, h), hc, ws, vnb in zip(chains, hcs, wss, vnbs):
            o = ws[CHUNK:] + _dot(at_ref[r * n_ch + c, h], vnb)
            on = o * lax.rsqrt(jnp.mean(o * o, axis=-1, keepdims=True) + EPS) * gn_ref[...]
            z = za_ref[r, pl.ds(base, CHUNK), hc]
            o_ref[r, pl.ds(base, CHUNK), hc] = (on * _silu(z)).astype(BF16)
        return carry

    lax.fori_loop(0, n_ch, phase_b, 0)


def _gated_deltanet(qkv, za, ba, conv_w, alog_vec, dtb_vec, g_norm, rb, tl, group_a):
    bsz, seq, _ = qkv.shape
    n_ch = tl // CHUNK
    kern = functools.partial(_gdn_kernel, rb=rb, tl=tl, group_a=group_a)
    return pl.pallas_call(
        kern,
        grid=(bsz // rb, seq // tl),
        in_specs=[pl.BlockSpec((rb, tl, 3 * W_A), lambda b, i: (b, i, 0)),
                  pl.BlockSpec((rb, tl, W_A), lambda b, i: (b, i, 0)),
                  pl.BlockSpec((rb, tl, LANES), lambda b, i: (b, i, 0)),
                  pl.BlockSpec((CONV_W, 3 * W_A), lambda b, i: (0, 0)),
                  pl.BlockSpec((1, LANES), lambda b, i: (0, 0)),
                  pl.BlockSpec((1, LANES), lambda b, i: (0, 0)),
                  pl.BlockSpec((1, DV_A), lambda b, i: (0, 0))],
        out_specs=pl.BlockSpec((rb, tl, W_A), lambda b, i: (b, i, 0)),
        out_shape=jax.ShapeDtypeStruct((bsz, seq, W_A), BF16),
        scratch_shapes=[pltpu.VMEM((rb, tl + 8, 3 * W_A), F32),
                        pltpu.VMEM((rb, H_A, DK_A, DV_A), F32),
                        pltpu.VMEM((rb, tl, LANES), F32),
                        pltpu.VMEM((rb, tl, LANES), F32),
                        pltpu.VMEM((rb * n_ch, LANES, CHUNK), F32),
                        pltpu.VMEM((rb, tl, W_A), F32),
                        pltpu.VMEM((rb * n_ch, H_A, 2 * CHUNK, DK_A), BF16),
                        pltpu.VMEM((rb, tl, W_A), BF16),
                        pltpu.VMEM((rb * n_ch, H_A, CHUNK, CHUNK), BF16)],
        compiler_params=pltpu.CompilerParams(dimension_semantics=("arbitrary", "arbitrary"),
                                             vmem_limit_bytes=VMEM_LIMIT),
        name="gdn",
    )(qkv, za, ba, conv_w, alog_vec, dtb_vec, g_norm.reshape(1, DV_A))


def _bias_kernel(rb_ref, o_ref):
    d = pl.program_id(0)
    kj = lax.broadcasted_iota(I32, (TQ, TQ), 0)
    qi = lax.broadcasted_iota(I32, (TQ, TQ), 1)
    dist = d * TQ + qi - kj
    n = jnp.maximum(dist, 0)
    nf = jnp.maximum(n, 1).astype(F32)
    large = MAX_EXACT + (jnp.log(nf / MAX_EXACT) / math.log(MAX_DIST / MAX_EXACT)
                         * (N_BUCKETS - MAX_EXACT)).astype(I32)
    large = jnp.minimum(large, N_BUCKETS - 1)
    bucket = jnp.where(n < MAX_EXACT, n, large)
    for h in range(H_B):
        acc = jnp.zeros((TQ, TQ), F32)
        for kb in range(N_BUCKETS):
            acc = jnp.where(bucket == kb, rb_ref[kb, h], acc)
        o_ref[0, :, h * TQ:(h + 1) * TQ] = acc * LOG2E


def _bias_tiles(rel_bias, n_diag):
    return pl.pallas_call(
        _bias_kernel,
        grid=(n_diag,),
        in_specs=[pl.BlockSpec(memory_space=pltpu.SMEM)],
        out_specs=pl.BlockSpec((1, TQ, H_B * TQ), lambda d: (d, 0, 0)),
        out_shape=jax.ShapeDtypeStruct((n_diag, TQ, H_B * TQ), F32),
        compiler_params=pltpu.CompilerParams(dimension_semantics=("arbitrary",)),
        name="bias",
    )(rel_bias)


def _fold_keys(x, op):
    x = op(x.reshape(4, TQ // 32, 8, x.shape[-1]), axis=1)
    return op(x, axis=0)


def _dsa_kernel(iq_ref, ikwq_ref, qb_ref, zb_ref, ikw_ref, ckv_ref, gkv_ref, wuv_ref, bias_ref,
                o_ref,
                kvn_ref, kvt_ref, iklo_ref, ikhi_ref, key_ref, am_ref, lg_ref, acc_ref,
                *, rb, seq, k_top):
    qi = pl.program_id(1)
    n_kc = qi + 1
    hq = H_B * TQ
    rows = range(rb)

    @pl.when(qi == 0)
    def _():
        for r in rows:
            for c in range(seq // TQ):
                sl = slice(c * TQ, (c + 1) * TQ)
                ckv = ckv_ref[r, sl, :]
                ms = jnp.mean(ckv * ckv, axis=-1, keepdims=True)
                kvn = (ckv * lax.rsqrt(ms + EPS)) * gkv_ref[...]
                kvn_ref[r, sl, :] = kvn.astype(BF16)
                kvt_ref[r, c] = kvn.T.astype(BF16)
                ikw = ikw_ref[r, sl, :]
                lane = lax.broadcasted_iota(I32, ikw.shape, 1)
                lo = jnp.where(lane < D_IDX, ikw, 0.0)
                iklo_ref[r, sl, :] = lo.astype(BF16)
                ikhi_ref[r, sl, :] = pltpu.roll(lo, D_IDX, axis=1).astype(BF16)

    iq4s, iwts = [], []
    for r in rows:
        iq = iq_ref[r]
        iq4s.append(jnp.concatenate(
            [iq[:, p * LANES:(p + 1) * LANES] for p in range(H_IDX // 2)], axis=0))
        iwts.append((ikwq_ref[r] * (H_IDX ** -0.5 * D_IDX ** -0.5)).T)

    key_j = lax.broadcasted_iota(I32, (TQ, TQ), 0)
    qry_t = qi * TQ + lax.broadcasted_iota(I32, (TQ, TQ), 1)

    def causal_mask(c):
        return (c * TQ + key_j) <= qry_t

    q4s = []
    for r in rows:
        qb = qb_ref[r]
        q4s.append(jnp.concatenate([qb[:, h * R_KV:(h + 1) * R_KV] for h in range(H_B)], axis=0))
    scale = R_KV ** -0.5 * LOG2E

    def pair_loop(body, carry):
        carry = lax.fori_loop(0, n_kc >> 1, lambda i, cr: body((2 * i, 2 * i + 1), cr), carry)
        return lax.cond((n_kc & 1) == 1, lambda cr: body((n_kc - 1,), cr), lambda cr: cr, carry)

    def score_chunks(cs, m0s):
        m0s = list(m0s)
        nc = len(cs)
        ks = pl.ds(pl.multiple_of(cs[0] * TQ, TQ), nc * TQ)
        rels = [_dot_nt(jnp.concatenate([iklo_ref[r, ks, :], ikhi_ref[r, ks, :]], axis=0),
                        iq4s[r]) for r in rows]
        sts = [_dot_nt(kvn_ref[r, ks, :], q4s[r]) for r in rows]
        for j, c in enumerate(cs):
            for r in rows:
                re = rels[r][j * TQ:(j + 1) * TQ]
                ro = rels[r][(nc + j) * TQ:(nc + j + 1) * TQ]
                s = None
                for p in range(H_IDX // 2):
                    ps = slice(p * TQ, (p + 1) * TQ)
                    we = iwts[r][IW_LANE + 2 * p:IW_LANE + 2 * p + 1, :]
                    wo = iwts[r][IW_LANE + 2 * p + 1:IW_LANE + 2 * p + 2, :]
                    t = jnp.maximum(re[:, ps], 0.0) * we + jnp.maximum(ro[:, ps], 0.0) * wo
                    s = t if s is None else s + t
                s = jnp.where(s == 0.0, 0.0, s)
                s = jnp.where(causal_mask(c), s, -jnp.inf)
                bits = pltpu.bitcast(s, I32)
                key_ref[r, c] = jnp.where(bits < 0, bits ^ 0x7FFFFFFF, bits)
                lg = sts[r][j * TQ:(j + 1) * TQ] * scale + bias_ref[qi - c]
                lg_ref[r, c] = lg
                m0s[r] = jnp.maximum(m0s[r], _fold_keys(lg, jnp.max))
        return tuple(m0s)

    m0s = pair_loop(score_chunks, tuple(jnp.full((8, hq), -jnp.inf, F32) for _ in rows))

    kf = float(k_top)
    n_pairs = (n_kc + 1) >> 1
    sign = jnp.int32(-2 ** 31)

    @pl.when((n_kc & 1) == 1)
    def _():
        for r in rows:
            key_ref[r, n_kc] = jnp.full((TQ, TQ), -2 ** 31, I32)

    def count_ge(cands):
        def body(i, accs):
            out = list(accs)
            for c in (2 * i, 2 * i + 1):
                for r in rows:
                    hit = jnp.where(key_ref[r, c] >= cands[r], 1.0, 0.0)
                    out[r] = out[r] + _fold_keys(hit, jnp.sum)
            return tuple(out)
        accs = lax.fori_loop(0, n_pairs, body, tuple(jnp.zeros((8, TQ), F32) for _ in rows))
        return [jnp.sum(a, axis=0, keepdims=True) for a in accs]

    def bit_body(i, carry):
        tus, cges = carry
        bit = lax.shift_left(jnp.int32(1), 31 - i)
        cands = [tu | bit for tu in tus]
        cnts = count_ge([(cu ^ sign)[0:1, :] for cu in cands])
        take = [cnt >= kf for cnt in cnts]
        return (tuple(jnp.where(t, cu, tu) for t, cu, tu in zip(take, cands, tus)),
                tuple(jnp.where(t, cnt, cge) for t, cnt, cge in zip(take, cnts, cges)))

    tus, cges = lax.fori_loop(0, 32, bit_body,
                              (tuple(jnp.zeros((8, TQ), I32) for _ in rows),
                               tuple(jnp.full((1, TQ), kf, F32) for _ in rows)))
    thrs = [(tu ^ sign)[0:1, :] for tu in tus]
    qrow = qi * TQ + lax.broadcasted_iota(I32, (1, TQ), 1)
    simple_all = None
    for r in rows:
        simple = (cges[r] == kf) | ((thrs[r] == NEG_INF_KEY) & (qrow < k_top))
        simple_all = simple if simple_all is None else (simple_all & simple)
    all_simple = jnp.min(jnp.where(simple_all, 1.0, 0.0)) > 0.5

    @pl.when(all_simple)
    def _():
        def body(c, carry):
            cm = causal_mask(c)
            for r in rows:
                sel = (key_ref[r, c] >= thrs[r]) & cm
                am_ref[r, c] = jnp.where(sel, 0.0, -jnp.inf)
            return carry
        lax.fori_loop(0, n_kc, body, 0)

    @pl.when(jnp.logical_not(all_simple))
    def _():
        lower = jnp.where(lax.broadcasted_iota(I32, (TQ, TQ), 1)
                          <= lax.broadcasted_iota(I32, (TQ, TQ), 0), 1.0, 0.0).astype(BF16)

        def count_gt(c, cgs):
            return tuple(cg + _fold_keys(jnp.where(key_ref[r, c] > thrs[r], 1.0, 0.0), jnp.sum)
                         for r, cg in zip(rows, cgs))

        cgs = lax.fori_loop(0, n_kc, count_gt, tuple(jnp.zeros((8, TQ), F32) for _ in rows))
        needs = [kf - jnp.sum(cg, axis=0, keepdims=True) for cg in cgs]

        def body(c, seens):
            cm = causal_mask(c)
            out = []
            for r in rows:
                key = key_ref[r, c]
                eq = key == thrs[r]
                eqf = jnp.where(eq, 1.0, 0.0)
                rank = seens[r] + _dot(lower, eqf.astype(BF16))
                sel = (key > thrs[r]) | (eq & (rank <= needs[r]))
                am_ref[r, c] = jnp.where(sel & cm, 0.0, -jnp.inf)
                out.append(seens[r] + jnp.sum(eqf, axis=0, keepdims=True))
            return tuple(out)
        lax.fori_loop(0, n_kc, body, tuple(jnp.zeros((1, TQ), F32) for _ in rows))

    def masked_logits(r, c):
        return lg_ref[r, c] + jnp.concatenate([am_ref[r, c]] * H_B, axis=1)

    def softmax_pv(shifts):
        acc_ref[...] = jnp.zeros_like(acc_ref)

        def body(cs, ls):
            out = list(ls)
            for r in rows:
                ps = [jnp.exp2(masked_logits(r, c) - shifts[r]) for c in cs]
                kvt = jnp.concatenate([kvt_ref[r, c] for c in cs], axis=1)
                acc_ref[r] += _dot(kvt, jnp.concatenate(ps, axis=0).astype(BF16))
                for p in ps:
                    out[r] = out[r] + _fold_keys(p, jnp.sum)
            return tuple(out)

        ls = pair_loop(body, tuple(jnp.zeros((8, hq), F32) for _ in rows))
        return tuple(jnp.sum(l, axis=0, keepdims=True) for l in ls)

    lsums = softmax_pv([jnp.max(m, axis=0, keepdims=True) for m in m0s])
    lmin = jnp.min(jnp.concatenate(lsums, axis=1))

    def exact_shift(_):
        def body(cs, ms):
            out = list(ms)
            for c in cs:
                for r in rows:
                    out[r] = jnp.maximum(out[r], _fold_keys(masked_logits(r, c), jnp.max))
            return tuple(out)
        ms = pair_loop(body, tuple(jnp.full((8, hq), -jnp.inf, F32) for _ in rows))
        return softmax_pv([jnp.max(m, axis=0, keepdims=True) for m in ms])

    lsums = lax.cond(lmin >= 2.0 ** -40, lambda _: lsums, exact_shift, 0)
    for r in rows:
        ot = (acc_ref[r] / lsums[r]).astype(BF16)
        for h in range(H_B):
            hs = slice(h * DV_B, (h + 1) * DV_B)
            y = _dot_tn(ot[:, h * TQ:(h + 1) * TQ], wuv_ref[h])
            o_ref[r, :, hs] = (y * _silu(zb_ref[r, :, hs])).astype(BF16)


def _dsa_attention(iq, ikw, qb, zb, ckv, g_kv, w_uv_bf16, bias_tiles, k_top, rb):
    bsz, seq, _ = iq.shape
    n_q = seq // TQ
    hq = H_B * TQ
    kern = functools.partial(_dsa_kernel, rb=rb, seq=seq, k_top=k_top)
    return pl.pallas_call(
        kern,
        grid=(bsz // rb, n_q),
        in_specs=[pl.BlockSpec((rb, TQ, H_IDX * D_IDX), lambda b, i: (b, i, 0)),
                  pl.BlockSpec((rb, TQ, LANES), lambda b, i: (b, i, 0)),
                  pl.BlockSpec((rb, TQ, H_B * R_KV), lambda b, i: (b, i, 0)),
                  pl.BlockSpec((rb, TQ, W_B), lambda b, i: (b, i, 0)),
                  pl.BlockSpec((rb, seq, LANES), lambda b, i: (b, 0, 0),
                               pipeline_mode=pl.Buffered(1)),
                  pl.BlockSpec((rb, seq, R_KV), lambda b, i: (b, 0, 0),
                               pipeline_mode=pl.Buffered(1)),
                  pl.BlockSpec((1, R_KV), lambda b, i: (0, 0)),
                  pl.BlockSpec((H_B, R_KV, DV_B), lambda b, i: (0, 0, 0)),
                  pl.BlockSpec((n_q, TQ, hq), lambda b, i: (0, 0, 0),
                               pipeline_mode=pl.Buffered(1))],
        out_specs=pl.BlockSpec((rb, TQ, W_B), lambda b, i: (b, i, 0)),
        out_shape=jax.ShapeDtypeStruct((bsz, seq, W_B), BF16),
        scratch_shapes=[pltpu.VMEM((rb, seq, R_KV), BF16),
                        pltpu.VMEM((rb, n_q, R_KV, TQ), BF16),
                        pltpu.VMEM((rb, seq, LANES), BF16),
                        pltpu.VMEM((rb, seq, LANES), BF16),
                        pltpu.VMEM((rb, n_q + 1, TQ, TQ), I32),
                        pltpu.VMEM((rb, n_q, TQ, TQ), F32),
                        pltpu.VMEM((rb, n_q, TQ, hq), F32),
                        pltpu.VMEM((rb, R_KV, hq), F32)],
        compiler_params=pltpu.CompilerParams(dimension_semantics=("arbitrary", "arbitrary"),
                                             vmem_limit_bytes=VMEM_LIMIT),
        name="dsa",
    )(iq, ikw, qb, zb, ikw, ckv, g_kv.reshape(1, R_KV), w_uv_bf16, bias_tiles)


def _out_kernel(x_ref, oa_ref, ob_ref, mod_ref, g_ref, w_ref, o_ref):
    mix_in = jnp.concatenate([oa_ref[0], ob_ref[0]], axis=-1)
    mix = _dot(mix_in, w_ref[...])
    ms = jnp.mean(mix * mix, axis=-1, keepdims=True)
    normed = (mix * lax.rsqrt(ms + EPS)) * g_ref[...]
    o_ref[0] = x_ref[0] + mod_ref[0, 2:3, :] * normed


def _output(x, o_a, o_b, mod3, g_post, w_out_bf16, tm):
    bsz, seq, d = x.shape
    return pl.pallas_call(
        _out_kernel,
        grid=(bsz, seq // tm),
        in_specs=[pl.BlockSpec((1, tm, d), lambda b, i: (b, i, 0)),
                  pl.BlockSpec((1, tm, W_A), lambda b, i: (b, i, 0)),
                  pl.BlockSpec((1, tm, W_B), lambda b, i: (b, i, 0)),
                  pl.BlockSpec((1, 3, d), lambda b, i: (b, 0, 0)),
                  pl.BlockSpec((1, d), lambda b, i: (0, 0)),
                  pl.BlockSpec((W_A + W_B, d), lambda b, i: (0, 0))],
        out_specs=pl.BlockSpec((1, tm, d), lambda b, i: (b, i, 0)),
        out_shape=jax.ShapeDtypeStruct((bsz, seq, d), F32),
        compiler_params=pltpu.CompilerParams(dimension_semantics=("arbitrary", "arbitrary"),
                                             vmem_limit_bytes=VMEM_LIMIT),
        name="outproj",
    )(x, o_a, o_b, mod3, g_post.reshape(1, d), w_out_bf16)


def _pad_lanes(w):
    return jnp.pad(w, ((0, 0), (0, LANES - w.shape[1])))


def _pad_input_projection(w_in):
    o_ba = 3 * W_A + W_A
    o_qb = o_ba + 2 * H_A
    o_ik = o_qb + H_B * R_KV + R_KV + W_B + H_IDX * D_IDX
    return jnp.concatenate([w_in[:, :o_ba], _pad_lanes(w_in[:, o_ba:o_qb]),
                            w_in[:, o_qb:o_ik], _pad_lanes(w_in[:, o_ik:])], axis=1)


def _head_lanes(v):
    return jnp.zeros((1, LANES), F32).at[0, H_A:2 * H_A].set(v.astype(F32))


def kernel(x, c, w_ada, b_ada, g_pre, w_in, conv_w, a_log, dt_bias, g_gdn, g_kv, w_uv, rel_bias, w_out, g_post):
    bsz, seq, d = x.shape
    depth = w_ada.shape[0]
    assert seq % TQ == 0 and seq % CHUNK == 0
    k_top = min(TOPK_MAX, seq // 4)
    tm = min(512, seq)
    tl = min(128, seq)
    rb = 8 if bsz % 8 == 0 else (2 if bsz % 2 == 0 else 1)
    bias_tiles = _bias_tiles(rel_bias, seq // TQ)
    for layer in range(depth):
        mod3 = _modulation(c, w_ada[layer], b_ada[layer]).reshape(bsz, 3, d)
        w_pad = _pad_input_projection(w_in[layer]).astype(BF16)
        qkv, za, ba, qb, ckv, zb, iq, ikw = _projection(x, mod3, g_pre[layer], w_pad, tm)
        o_a = _gated_deltanet(qkv, za, ba, conv_w[layer], _head_lanes(a_log[layer]),
                              _head_lanes(dt_bias[layer]), g_gdn[layer], rb, tl, math.gcd(8, rb * (tl // CHUNK)))
        o_b = _dsa_attention(iq, ikw, qb, zb, ckv, g_kv[layer], w_uv[layer].astype(BF16),
                             bias_tiles, k_top, 4 if bsz % 4 == 0 else 1)
        x = _output(x, o_a, o_b, mod3, g_post[layer], w_out[layer].astype(BF16), tm)
    return x
```

```python
import functools
import math

import jax
import jax.numpy as jnp
from jax import lax
from jax.experimental import pallas as pl
from jax.experimental.pallas import tpu as pltpu

F32 = jnp.float32
BF16 = jnp.bfloat16
I32 = jnp.int32
HIGHEST = lax.Precision.HIGHEST

H_A, DK_A, DV_A = 4, 128, 128
W_A = H_A * DV_A
CONV_W = 4
CHUNK = 64
H_B, R_KV, DV_B = 4, 128, 128
W_B = H_B * DV_B
H_IDX, D_IDX = 8, 64
TOPK_MAX = 256
N_BUCKETS, MAX_EXACT, MAX_DIST = 32, 16, 128
EPS = 1e-6
LOG2E = math.log2(math.e)

LANES = 128
TQ = 128
FIELD = 15
VMEM_LIMIT = 58 * 1024 * 1024

C_QKV = (0, 3 * W_A)
C_ZA = (C_QKV[1], C_QKV[1] + W_A)
C_BA = (C_ZA[1], C_ZA[1] + LANES)
C_QB = (C_BA[1], C_BA[1] + H_B * R_KV)
C_CKV = (C_QB[1], C_QB[1] + R_KV)
C_ZB = (C_CKV[1], C_CKV[1] + W_B)
C_IQ = (C_ZB[1], C_ZB[1] + H_IDX * D_IDX)
C_IKW = (C_IQ[1], C_IQ[1] + LANES)
D_IN_PAD = C_IKW[1]
IW_LANE = D_IDX

NEG_INF_KEY = -2139095041


def _sigmoid(x):
    return 1.0 / (1.0 + jnp.exp(-x))


def _silu(x):
    return x * _sigmoid(x)


def _softplus(x):
    return jnp.maximum(x, 0.0) + jnp.log1p(jnp.exp(-jnp.abs(x)))


def _dot(a, b, precision=None):
    return jnp.dot(a, b, precision=precision, preferred_element_type=F32)


def _dot_nt(a, b, precision=None):
    return lax.dot_general(a, b, (((1,), (1,)), ((), ())), precision=precision,
                           preferred_element_type=F32)


def _dot_tn(a, b, precision=None):
    return lax.dot_general(a, b, (((0,), (0,)), ((), ())), precision=precision,
                           preferred_element_type=F32)


def _bdot(a, b, precision=None):
    return _dot(a.astype(BF16), b.astype(BF16))


def _bdot_nt(a, b, precision=None):
    return _dot_nt(a.astype(BF16), b.astype(BF16))


def _bdot_tn(a, b, precision=None):
    return _dot_tn(a.astype(BF16), b.astype(BF16))


def _mod_kernel(c_ref, w_ref, b_ref, o_ref):
    c = c_ref[...]
    o_ref[...] = _dot(_silu(c), w_ref[...], HIGHEST) + b_ref[...]


def _modulation(c, w_ada, b_ada):
    bsz, d = c.shape
    n = w_ada.shape[1]
    tn = 512
    return pl.pallas_call(
        _mod_kernel,
        grid=(n // tn,),
        in_specs=[pl.BlockSpec((bsz, d), lambda j: (0, 0)),
                  pl.BlockSpec((d, tn), lambda j: (0, j)),
                  pl.BlockSpec((1, tn), lambda j: (0, j))],
        out_specs=pl.BlockSpec((bsz, tn), lambda j: (0, j)),
        out_shape=jax.ShapeDtypeStruct((bsz, n), F32),
        compiler_params=pltpu.CompilerParams(dimension_semantics=("arbitrary",),
                                             vmem_limit_bytes=VMEM_LIMIT),
        name="mod",
    )(c, w_ada, b_ada.reshape(1, n))


def _proj_kernel(x_ref, mod_ref, g_ref, w_ref,
                 qkv_ref, za_ref, ba_ref, qb_ref, ckv_ref, zb_ref, iq_ref, ikw_ref):
    x = x_ref[0]
    ms = jnp.mean(x * x, axis=-1, keepdims=True)
    xn = x * lax.rsqrt(ms + EPS)
    shift = mod_ref[0, 0:1, :]
    scale = mod_ref[0, 1:2, :]
    h = (xn * g_ref[...]) * (1.0 + scale) + shift
    hb = h.astype(BF16)

    def mm(cols):
        return _dot(hb, w_ref[:, cols[0]:cols[1]])

    qkv_ref[0] = mm(C_QKV)
    za_ref[0] = mm(C_ZA)
    ba_ref[0] = mm(C_BA)
    qb_ref[0] = mm(C_QB).astype(BF16)
    ckv_ref[0] = mm(C_CKV)
    zb_ref[0] = mm(C_ZB)
    iq_ref[0] = mm(C_IQ).astype(BF16)
    ikw_ref[0] = mm(C_IKW)


def _projection(x, mod3, g_pre, w_pad, tm):
    bsz, seq, d = x.shape
    widths = [(C_QKV, F32), (C_ZA, F32), (C_BA, F32), (C_QB, BF16),
              (C_CKV, F32), (C_ZB, F32), (C_IQ, BF16), (C_IKW, F32)]
    out_shape = [jax.ShapeDtypeStruct((bsz, seq, c[1] - c[0]), dt) for c, dt in widths]
    out_specs = [pl.BlockSpec((1, tm, c[1] - c[0]), lambda b, i: (b, i, 0)) for c, _ in widths]
    return pl.pallas_call(
        _proj_kernel,
        grid=(bsz, seq // tm),
        in_specs=[pl.BlockSpec((1, tm, d), lambda b, i: (b, i, 0)),
                  pl.BlockSpec((1, 3, d), lambda b, i: (b, 0, 0)),
                  pl.BlockSpec((1, d), lambda b, i: (0, 0)),
                  pl.BlockSpec((d, D_IN_PAD), lambda b, i: (0, 0))],
        out_specs=out_specs,
        out_shape=out_shape,
        compiler_params=pltpu.CompilerParams(dimension_semantics=("arbitrary", "arbitrary"),
                                             vmem_limit_bytes=VMEM_LIMIT),
        name="proj",
    )(x, mod3, g_pre.reshape(1, d), w_pad)


def _tri_inverse_many(lmats, level_masks, eye):
    ts = [eye - jnp.where(level_masks[0], lm, 0.0) for lm in lmats]
    for m in level_masks[1:]:
        xs = [_bdot(jnp.where(m, lm, 0.0), t) for lm, t in zip(lmats, ts)]
        ts = [t - _bdot(t, x) for t, x in zip(ts, xs)]
    return ts


def _gdn_kernel(qkv_ref, za_ref, ba_ref, cw_ref, alog_ref, dtb_ref, gn_ref, o_ref,
                ext_ref, s_ref, gc_ref, beta_ref, gct_ref, u_ref, wq_ref, kg_ref, at_ref,
                *, rb, tl, group_a):
    n_ch = tl // CHUNK
    li = pl.program_id(1)

    @pl.when(li == 0)
    def _():
        ext_ref[:, 0:8, :] = jnp.zeros((rb, 8, 3 * W_A), F32)
        s_ref[...] = jnp.zeros_like(s_ref)

    @pl.when(li > 0)
    def _():
        ext_ref[:, 0:8, :] = ext_ref[:, tl:tl + 8, :]

    ext_ref[:, 8:8 + tl, :] = qkv_ref[...]

    r_i = lax.broadcasted_iota(I32, (tl, tl), 0)
    c_i = lax.broadcasted_iota(I32, (tl, tl), 1)
    shift = CHUNK.bit_length() - 1
    tri = jnp.where((c_i <= r_i) & ((r_i >> shift) == (c_i >> shift)), 1.0, 0.0)
    for r in range(rb):
        ba = ba_ref[r]
        beta_ref[r] = _sigmoid(ba)
        g = -jnp.exp(alog_ref[...]) * _softplus(ba + dtb_ref[...])
        gc = _dot(tri, g, HIGHEST)
        gc_ref[r] = gc
        gct = gc.T
        for c in range(n_ch):
            gct_ref[r * n_ch + c] = gct[:, c * CHUNK:(c + 1) * CHUNK]

    row = lax.broadcasted_iota(I32, (CHUNK, CHUNK), 0)
    col = lax.broadcasted_iota(I32, (CHUNK, CHUNK), 1)
    causal = row >= col
    strict = row > col
    eye = jnp.where(row == col, 1.0, 0.0)
    level_masks = []
    s = 1
    while s < CHUNK:
        ls = s.bit_length() - 1
        level_masks.append(((row >> (ls + 1)) == (col >> (ls + 1)))
                           & (((row >> ls) & 1) == 1) & (((col >> ls) & 1) == 0))
        s *= 2

    def phase_a(ig, carry):
        probs = []
        for j in range(group_a):
            it = ig * group_a + j
            r = it // n_ch
            c = it - r * n_ch
            base = pl.multiple_of(c * CHUNK, CHUNK)
            gc_c = gc_ref[r, pl.ds(base, CHUNK), :]
            beta_c = beta_ref[r, pl.ds(base, CHUNK), :]
            gct_c = gct_ref[it]
            for h in range(H_A):
                def conv_silu(sec):
                    c0 = sec * W_A + h * DK_A
                    win = ext_ref[r, pl.ds(base, CHUNK + 8), c0:c0 + DK_A]
                    acc = None
                    for t in range(CONV_W):
                        sh = CONV_W - 1 - t
                        u = win if sh == 0 else pltpu.roll(win, sh, axis=0)
                        term = u[8:8 + CHUNK] * cw_ref[t:t + 1, c0:c0 + DK_A]
                        acc = term if acc is None else acc + term
                    return _silu(acc)

                q = conv_silu(0)
                k = conv_silu(1)
                v = conv_silu(2)
                q = q * lax.rsqrt(jnp.sum(q * q, axis=-1, keepdims=True) + EPS) * (DK_A ** -0.5)
                k = k * lax.rsqrt(jnp.sum(k * k, axis=-1, keepdims=True) + EPS)
                beta = beta_c[:, h:h + 1]
                gcol = gc_c[:, H_A + h:H_A + h + 1]
                grow = gct_c[H_A + h:H_A + h + 1, :]
                glast = gc_c[CHUNK - 1:CHUNK, H_A + h:H_A + h + 1]
                decay = jnp.exp(jnp.where(causal, gcol - grow, -jnp.inf))
                ecol = jnp.exp(gcol)
                kb = k * beta
                hc = slice(h * DV_A, (h + 1) * DV_A)
                wq_ref[it, h, CHUNK:2 * CHUNK, :] = (q * ecol).astype(BF16)
                kg_ref[r, pl.ds(base, CHUNK), hc] = (k * jnp.exp(glast - gcol)).astype(BF16)
                a2 = _bdot_nt(jnp.concatenate([kb, q], axis=0), k)
                at_ref[it, h] = (a2[CHUNK:] * decay).astype(BF16)
                probs.append(dict(
                    r=r, it=it, h=h, base=base, hc=hc,
                    lmat=jnp.where(strict, a2[:CHUNK] * decay, 0.0),
                    rhs=jnp.concatenate([v * beta, kb * ecol], axis=1).astype(BF16)))
        tmats = _tri_inverse_many([p["lmat"] for p in probs], level_masks, eye)
        for p, tmat in zip(probs, tmats):
            uw = _dot(tmat.astype(BF16), p["rhs"])
            u_ref[p["r"], pl.ds(p["base"], CHUNK), p["hc"]] = uw[:, :DV_A]
            wq_ref[p["it"], p["h"], 0:CHUNK, :] = uw[:, DV_A:].astype(BF16)
        return carry

    lax.fori_loop(0, rb * n_ch // group_a, phase_a, 0)

    def phase_b(c, carry):
        base = pl.multiple_of(c * CHUNK, CHUNK)
        chains = [(r, h) for r in range(rb) for h in range(H_A)]
        hcs = [slice(h * DV_A, (h + 1) * DV_A) for _, h in chains]
        sts = [s_ref[r, h] for r, h in chains]
        wss = [_dot(wq_ref[r * n_ch + c, h], st.astype(BF16)) for (r, h), st in zip(chains, sts)]
        vnbs = [(u_ref[r, pl.ds(base, CHUNK), hc] - ws[:CHUNK]).astype(BF16)
                for (r, h), hc, ws in zip(chains, hcs, wss)]
        upds = [_dot_tn(kg_ref[r, pl.ds(base, CHUNK), hc], vnb)
                for (r, h), hc, vnb in zip(chains, hcs, vnbs)]
        for (r, h), st, upd in zip(chains, sts, upds):
            glast = gc_ref[r, pl.ds(base + CHUNK - 1, 1), :][:, H_A + h:H_A + h + 1]
            s_ref[r, h] = st * jnp.exp(glast) + upd
        for (r, h), hc, ws, vnb in zip(chains, hcs, wss, vnbs):
            o = ws[CHUNK:] + _dot(at_ref[r * n_ch + c, h], vnb)
            on = o * lax.rsqrt(jnp.mean(o * o, axis=-1, keepdims=True) + EPS) * gn_ref[...]
            z = za_ref[r, pl.ds(base, CHUNK), hc]
            o_ref[r, pl.ds(base, CHUNK), hc] = (on * _silu(z)).astype(BF16)
        return carry

    lax.fori_loop(0, n_ch, phase_b, 0)


def _gated_deltanet(qkv, za, ba, conv_w, alog_vec, dtb_vec, g_norm, rb, tl, group_a):
    bsz, seq, _ = qkv.shape
    n_ch = tl // CHUNK
    kern = functools.partial(_gdn_kernel, rb=rb, tl=tl, group_a=group_a)
    return pl.pallas_call(
        kern,
        grid=(bsz // rb, seq // tl),
        in_specs=[pl.BlockSpec((rb, tl, 3 * W_A), lambda b, i: (b, i, 0)),
                  pl.BlockSpec((rb, tl, W_A), lambda b, i: (b, i, 0)),
                  pl.BlockSpec((rb, tl, LANES), lambda b, i: (b, i, 0)),
                  pl.BlockSpec((CONV_W, 3 * W_A), lambda b, i: (0, 0)),
                  pl.BlockSpec((1, LANES), lambda b, i: (0, 0)),
                  pl.BlockSpec((1, LANES), lambda b, i: (0, 0)),
                  pl.BlockSpec((1, DV_A), lambda b, i: (0, 0))],
        out_specs=pl.BlockSpec((rb, tl, W_A), lambda b, i: (b, i, 0)),
        out_shape=jax.ShapeDtypeStruct((bsz, seq, W_A), BF16),
        scratch_shapes=[pltpu.VMEM((rb, tl + 8, 3 * W_A), F32),
                        pltpu.VMEM((rb, H_A, DK_A, DV_A), F32),
                        pltpu.VMEM((rb, tl, LANES), F32),
                        pltpu.VMEM((rb, tl, LANES), F32),
                        pltpu.VMEM((rb * n_ch, LANES, CHUNK), F32),
                        pltpu.VMEM((rb, tl, W_A), F32),
                        pltpu.VMEM((rb * n_ch, H_A, 2 * CHUNK, DK_A), BF16),
                        pltpu.VMEM((rb, tl, W_A), BF16),
                        pltpu.VMEM((rb * n_ch, H_A, CHUNK, CHUNK), BF16)],
        compiler_params=pltpu.CompilerParams(dimension_semantics=("arbitrary", "arbitrary"),
                                             vmem_limit_bytes=VMEM_LIMIT),
        name="gdn",
    )(qkv, za, ba, conv_w, alog_vec, dtb_vec, g_norm.reshape(1, DV_A))


def _bias_kernel(rb_ref, o_ref):
    d = pl.program_id(0)
    kj = lax.broadcasted_iota(I32, (TQ, TQ), 0)
    qi = lax.broadcasted_iota(I32, (TQ, TQ), 1)
    dist = d * TQ + qi - kj
    n = jnp.maximum(dist, 0)
    nf = jnp.maximum(n, 1).astype(F32)
    large = MAX_EXACT + (jnp.log(nf / MAX_EXACT) / math.log(MAX_DIST / MAX_EXACT)
                         * (N_BUCKETS - MAX_EXACT)).astype(I32)
    large = jnp.minimum(large, N_BUCKETS - 1)
    bucket = jnp.where(n < MAX_EXACT, n, large)
    for h in range(H_B):
        acc = jnp.zeros((TQ, TQ), F32)
        for kb in range(N_BUCKETS):
            acc = jnp.where(bucket == kb, rb_ref[kb, h], acc)
        o_ref[0, :, h * TQ:(h + 1) * TQ] = acc * LOG2E


def _bias_tiles(rel_bias, n_diag):
    return pl.pallas_call(
        _bias_kernel,
        grid=(n_diag,),
        in_specs=[pl.BlockSpec(memory_space=pltpu.SMEM)],
        out_specs=pl.BlockSpec((1, TQ, H_B * TQ), lambda d: (d, 0, 0)),
        out_shape=jax.ShapeDtypeStruct((n_diag, TQ, H_B * TQ), F32),
        compiler_params=pltpu.CompilerParams(dimension_semantics=("arbitrary",)),
        name="bias",
    )(rel_bias)


def _fold_keys(x, op):
    x = op(x.reshape(4, x.shape[0] // 32, 8, x.shape[-1]), axis=1)
    return op(x, axis=0)


def _dsa_kernel(iq_ref, ikwq_ref, qb_ref, zb_ref, ikw_ref, ckv_ref, gkv_ref, wuv_ref, bias_ref,
                o_ref,
                kvn_ref, kvt_ref, iklo_ref, ikhi_ref, key_ref, w_ref, am_ref, lg_ref, acc_ref,
                *, rb, seq, k_top):
    qi = pl.program_id(1)
    n_kc = qi + 1
    hq = H_B * TQ
    rows = range(rb)

    @pl.when(qi == 0)
    def _():
        for r in rows:
            for c in range(seq // TQ):
                sl = slice(c * TQ, (c + 1) * TQ)
                ckv = ckv_ref[r, sl, :]
                ms = jnp.mean(ckv * ckv, axis=-1, keepdims=True)
                kvn = (ckv * lax.rsqrt(ms + EPS)) * gkv_ref[...]
                kvn_ref[r, sl, :] = kvn.astype(BF16)
                kvt_ref[r, c] = kvn.T.astype(BF16)
                ikw = ikw_ref[r, sl, :]
                lane = lax.broadcasted_iota(I32, ikw.shape, 1)
                lo = jnp.where(lane < D_IDX, ikw, 0.0)
                iklo_ref[r, sl, :] = lo.astype(BF16)
                ikhi_ref[r, sl, :] = pltpu.roll(lo, D_IDX, axis=1).astype(BF16)

    iq4s, iwts = [], []
    for r in rows:
        iq = iq_ref[r]
        iq4s.append(jnp.concatenate(
            [iq[:, p * LANES:(p + 1) * LANES] for p in range(H_IDX // 2)], axis=0))
        iwts.append((ikwq_ref[r] * (H_IDX ** -0.5 * D_IDX ** -0.5)).T)

    key_j = lax.broadcasted_iota(I32, (TQ, TQ), 0)
    qry_t = qi * TQ + lax.broadcasted_iota(I32, (TQ, TQ), 1)

    def causal_mask(c):
        return (c * TQ + key_j) <= qry_t

    q4s = []
    for r in rows:
        qb = qb_ref[r]
        q4s.append(jnp.concatenate([qb[:, h * R_KV:(h + 1) * R_KV] for h in range(H_B)], axis=0))
    scale = R_KV ** -0.5 * LOG2E

    def pair_loop(body, carry):
        carry = lax.fori_loop(0, n_kc >> 1, lambda i, cr: body((2 * i, 2 * i + 1), cr), carry)
        return lax.cond((n_kc & 1) == 1, lambda cr: body((n_kc - 1,), cr), lambda cr: cr, carry)

    half = TQ // 2
    guards = jnp.int32(-(2 ** 31) + 2 ** 15)
    ones2 = jnp.int32(2 ** 16 + 1)
    fmax = 2 ** FIELD - 1

    def pack_fields(f):
        return (f[half:] << 16) | f[:half] | guards

    def field1(key):
        return lax.shift_right_logical(key, 32 - FIELD) ^ (1 << (FIELD - 1))

    def field2(key):
        return lax.shift_right_logical(key, 32 - 2 * FIELD) & fmax

    def score_chunks(cs, m0s):
        m0s = list(m0s)
        nc = len(cs)
        ks = pl.ds(pl.multiple_of(cs[0] * TQ, TQ), nc * TQ)
        rels = [_dot_nt(jnp.concatenate([iklo_ref[r, ks, :], ikhi_ref[r, ks, :]], axis=0),
                        iq4s[r]) for r in rows]
        sts = [_dot_nt(kvn_ref[r, ks, :], q4s[r]) for r in rows]
        for j, c in enumerate(cs):
            for r in rows:
                re = rels[r][j * TQ:(j + 1) * TQ]
                ro = rels[r][(nc + j) * TQ:(nc + j + 1) * TQ]
                s = None
                for p in range(H_IDX // 2):
                    ps = slice(p * TQ, (p + 1) * TQ)
                    we = iwts[r][IW_LANE + 2 * p:IW_LANE + 2 * p + 1, :]
                    wo = iwts[r][IW_LANE + 2 * p + 1:IW_LANE + 2 * p + 2, :]
                    t = jnp.maximum(re[:, ps], 0.0) * we + jnp.maximum(ro[:, ps], 0.0) * wo
                    s = t if s is None else s + t
                s = jnp.where(s == 0.0, 0.0, s)
                s = jnp.where(causal_mask(c), s, -jnp.inf)
                bits = pltpu.bitcast(s, I32)
                key = jnp.where(bits < 0, bits ^ 0x7FFFFFFF, bits)
                key_ref[r, c] = key
                w_ref[r, c] = pack_fields(field1(key))
                lg = sts[r][j * TQ:(j + 1) * TQ] * scale + bias_ref[qi - c]
                lg_ref[r, c] = lg
                m0s[r] = jnp.maximum(m0s[r], _fold_keys(lg, jnp.max))
        return tuple(m0s)

    m0s = pair_loop(score_chunks, tuple(jnp.full((8, hq), -jnp.inf, F32) for _ in rows))

    kf = float(k_top)
    n_pairs = (n_kc + 1) >> 1
    sign = jnp.int32(-2 ** 31)

    @pl.when((n_kc & 1) == 1)
    def _():
        for r in rows:
            key_ref[r, n_kc] = jnp.full((TQ, TQ), -2 ** 31, I32)

    def count_ge(cands):
        def body(i, accs):
            out = list(accs)
            for c in (2 * i, 2 * i + 1):
                for r in rows:
                    hit = jnp.where(key_ref[r, c] >= cands[r], 1.0, 0.0)
                    out[r] = out[r] + _fold_keys(hit, jnp.sum)
            return tuple(out)
        accs = lax.fori_loop(0, n_pairs, body, tuple(jnp.zeros((8, TQ), F32) for _ in rows))
        return [jnp.sum(a, axis=0, keepdims=True) for a in accs]

    def count_fields(cands):
        pairs = [(cu << 16) | cu for cu in cands]
        def body(i, accs):
            out = list(accs)
            for c in (2 * i, 2 * i + 1):
                for r in rows:
                    d = w_ref[r, c] - pairs[r]
                    hit = lax.shift_right_logical(d, 15) & ones2
                    out[r] = out[r] + _fold_keys(hit, jnp.sum)
            return tuple(out)
        accs = lax.fori_loop(0, n_pairs, body, tuple(jnp.zeros((8, TQ), I32) for _ in rows))
        both = [(a & 0xFFFF) + lax.shift_right_logical(a, 16) for a in accs]
        return [jnp.sum(b, axis=0, keepdims=True).astype(F32) for b in both]

    def field_level(bases, cges):
        def bit_body(i, carry):
            ts, cg = carry
            bit = lax.shift_left(jnp.int32(1), FIELD - 1 - i)
            cands = [t | bit for t in ts]
            cnts = [b + n for b, n in zip(bases, count_fields([cu[0:1, :] for cu in cands]))]
            take = [cnt >= kf for cnt in cnts]
            return (tuple(jnp.where(tk, cu, t) for tk, cu, t in zip(take, cands, ts)),
                    tuple(jnp.where(tk, cnt, g) for tk, cnt, g in zip(take, cnts, cg)))
        ts, cges = lax.fori_loop(0, FIELD, bit_body,
                                 (tuple(jnp.zeros((8, TQ), I32) for _ in rows), tuple(cges)))
        ps = [t[0:1, :] for t in ts]
        above = count_fields([jnp.minimum(p + 1, fmax) for p in ps])
        above = [b + jnp.where(p == fmax, 0.0, n) for b, p, n in zip(bases, ps, above)]
        return ps, above, cges

    @pl.when((n_kc & 1) == 1)
    def _():
        for r in rows:
            w_ref[r, n_kc] = jnp.full((half, TQ), guards, I32)

    cges = [jnp.full((1, TQ), kf, F32) for _ in rows]
    p1s, above1, cges = field_level([0.0 for _ in rows], cges)

    def build_level2(c, carry):
        for r in rows:
            key = key_ref[r, c]
            w_ref[r, c] = pack_fields(jnp.where(field1(key) == p1s[r], field2(key), 0))
        return carry

    lax.fori_loop(0, n_kc, build_level2, 0)
    p2s, _, cges = field_level(above1, cges)

    def bit_body(i, carry):
        tus, cges = carry
        bit = lax.shift_left(jnp.int32(1), 31 - i)
        cands = [tu | bit for tu in tus]
        cnts = count_ge([(cu ^ sign)[0:1, :] for cu in cands])
        take = [cnt >= kf for cnt in cnts]
        return (tuple(jnp.where(t, cu, tu) for t, cu, tu in zip(take, cands, tus)),
                tuple(jnp.where(t, cnt, cge) for t, cnt, cge in zip(take, cnts, cges)))

    tus = tuple(jnp.broadcast_to((p1 << (32 - FIELD)) | (p2 << (32 - 2 * FIELD)), (8, TQ))
                for p1, p2 in zip(p1s, p2s))
    tus, cges = lax.fori_loop(2 * FIELD, 32, bit_body, (tus, tuple(cges)))
    thrs = [(tu ^ sign)[0:1, :] for tu in tus]
    qrow = qi * TQ + lax.broadcasted_iota(I32, (1, TQ), 1)
    simple_all = None
    for r in rows:
        simple = (cges[r] == kf) | ((thrs[r] == NEG_INF_KEY) & (qrow < k_top))
        simple_all = simple if simple_all is None else (simple_all & simple)
    all_simple = jnp.min(jnp.where(simple_all, 1.0, 0.0)) > 0.5

    @pl.when(all_simple)
    def _():
        def body(c, carry):
            cm = causal_mask(c)
            for r in rows:
                sel = (key_ref[r, c] >= thrs[r]) & cm
                am_ref[r, c] = jnp.where(sel, 0.0, -jnp.inf)
            return carry
        lax.fori_loop(0, n_kc, body, 0)

    @pl.when(jnp.logical_not(all_simple))
    def _():
        lower = jnp.where(lax.broadcasted_iota(I32, (TQ, TQ), 1)
                          <= lax.broadcasted_iota(I32, (TQ, TQ), 0), 1.0, 0.0).astype(BF16)

        def count_gt(c, cgs):
            return tuple(cg + _fold_keys(jnp.where(key_ref[r, c] > thrs[r], 1.0, 0.0), jnp.sum)
                         for r, cg in zip(rows, cgs))

        cgs = lax.fori_loop(0, n_kc, count_gt, tuple(jnp.zeros((8, TQ), F32) for _ in rows))
        needs = [kf - jnp.sum(cg, axis=0, keepdims=True) for cg in cgs]

        def body(c, seens):
            cm = causal_mask(c)
            out = []
            for r in rows:
                key = key_ref[r, c]
                eq = key == thrs[r]
                eqf = jnp.where(eq, 1.0, 0.0)
                rank = seens[r] + _dot(lower, eqf.astype(BF16))
                sel = (key > thrs[r]) | (eq & (rank <= needs[r]))
                am_ref[r, c] = jnp.where(sel & cm, 0.0, -jnp.inf)
                out.append(seens[r] + jnp.sum(eqf, axis=0, keepdims=True))
            return tuple(out)
        lax.fori_loop(0, n_kc, body, tuple(jnp.zeros((1, TQ), F32) for _ in rows))

    def masked_logits(r, c):
        return lg_ref[r, c] + jnp.concatenate([am_ref[r, c]] * H_B, axis=1)

    def softmax_pv(shifts):
        acc_ref[...] = jnp.zeros_like(acc_ref)

        def body(cs, ls):
            out = list(ls)
            for r in rows:
                ps = [jnp.exp2(masked_logits(r, c) - shifts[r]) for c in cs]
                kvt = jnp.concatenate([kvt_ref[r, c] for c in cs], axis=1)
                acc_ref[r] += _dot(kvt, jnp.concatenate(ps, axis=0).astype(BF16))
                for p in ps:
                    out[r] = out[r] + _fold_keys(p, jnp.sum)
            return tuple(out)

        ls = pair_loop(body, tuple(jnp.zeros((8, hq), F32) for _ in rows))
        return tuple(jnp.sum(l, axis=0, keepdims=True) for l in ls)

    lsums = softmax_pv([jnp.max(m, axis=0, keepdims=True) for m in m0s])
    lmin = jnp.min(jnp.concatenate(lsums, axis=1))

    def exact_shift(_):
        def body(cs, ms):
            out = list(ms)
            for c in cs:
                for r in rows:
                    out[r] = jnp.maximum(out[r], _fold_keys(masked_logits(r, c), jnp.max))
            return tuple(out)
        ms = pair_loop(body, tuple(jnp.full((8, hq), -jnp.inf, F32) for _ in rows))
        return softmax_pv([jnp.max(m, axis=0, keepdims=True) for m in ms])

    lsums = lax.cond(lmin >= 2.0 ** -40, lambda _: lsums, exact_shift, 0)
    for r in rows:
        ot = (acc_ref[r] / lsums[r]).astype(BF16)
        for h in range(H_B):
            hs = slice(h * DV_B, (h + 1) * DV_B)
            y = _dot_tn(ot[:, h * TQ:(h + 1) * TQ], wuv_ref[h])
            o_ref[r, :, hs] = (y * _silu(zb_ref[r, :, hs])).astype(BF16)


def _dsa_attention(iq, ikw, qb, zb, ckv, g_kv, w_uv_bf16, bias_tiles, k_top, rb):
    bsz, seq, _ = iq.shape
    n_q = seq // TQ
    hq = H_B * TQ
    kern = functools.partial(_dsa_kernel, rb=rb, seq=seq, k_top=k_top)
    return pl.pallas_call(
        kern,
        grid=(bsz // rb, n_q),
        in_specs=[pl.BlockSpec((rb, TQ, H_IDX * D_IDX), lambda b, i: (b, i, 0)),
                  pl.BlockSpec((rb, TQ, LANES), lambda b, i: (b, i, 0)),
                  pl.BlockSpec((rb, TQ, H_B * R_KV), lambda b, i: (b, i, 0)),
                  pl.BlockSpec((rb, TQ, W_B), lambda b, i: (b, i, 0)),
                  pl.BlockSpec((rb, seq, LANES), lambda b, i: (b, 0, 0),
                               pipeline_mode=pl.Buffered(1)),
                  pl.BlockSpec((rb, seq, R_KV), lambda b, i: (b, 0, 0),
                               pipeline_mode=pl.Buffered(1)),
                  pl.BlockSpec((1, R_KV), lambda b, i: (0, 0)),
                  pl.BlockSpec((H_B, R_KV, DV_B), lambda b, i: (0, 0, 0)),
                  pl.BlockSpec((n_q, TQ, hq), lambda b, i: (0, 0, 0),
                               pipeline_mode=pl.Buffered(1))],
        out_specs=pl.BlockSpec((rb, TQ, W_B), lambda b, i: (b, i, 0)),
        out_shape=jax.ShapeDtypeStruct((bsz, seq, W_B), BF16),
        scratch_shapes=[pltpu.VMEM((rb, seq, R_KV), BF16),
                        pltpu.VMEM((rb, n_q, R_KV, TQ), BF16),
                        pltpu.VMEM((rb, seq, LANES), BF16),
                        pltpu.VMEM((rb, seq, LANES), BF16),
                        pltpu.VMEM((rb, n_q + 1, TQ, TQ), I32),
                        pltpu.VMEM((rb, n_q + 1, TQ // 2, TQ), I32),
                        pltpu.VMEM((rb, n_q, TQ, TQ), F32),
                        pltpu.VMEM((rb, n_q, TQ, hq), F32),
                        pltpu.VMEM((rb, R_KV, hq), F32)],
        compiler_params=pltpu.CompilerParams(dimension_semantics=("arbitrary", "arbitrary"),
                                             vmem_limit_bytes=VMEM_LIMIT),
        name="dsa",
    )(iq, ikw, qb, zb, ikw, ckv, g_kv.reshape(1, R_KV), w_uv_bf16, bias_tiles)


def _out_kernel(x_ref, oa_ref, ob_ref, mod_ref, g_ref, w_ref, o_ref):
    mix_in = jnp.concatenate([oa_ref[0], ob_ref[0]], axis=-1)
    mix = _dot(mix_in, w_ref[...])
    ms = jnp.mean(mix * mix, axis=-1, keepdims=True)
    normed = (mix * lax.rsqrt(ms + EPS)) * g_ref[...]
    o_ref[0] = x_ref[0] + mod_ref[0, 2:3, :] * normed


def _output(x, o_a, o_b, mod3, g_post, w_out_bf16, tm):
    bsz, seq, d = x.shape
    return pl.pallas_call(
        _out_kernel,
        grid=(bsz, seq // tm),
        in_specs=[pl.BlockSpec((1, tm, d), lambda b, i: (b, i, 0)),
                  pl.BlockSpec((1, tm, W_A), lambda b, i: (b, i, 0)),
                  pl.BlockSpec((1, tm, W_B), lambda b, i: (b, i, 0)),
                  pl.BlockSpec((1, 3, d), lambda b, i: (b, 0, 0)),
                  pl.BlockSpec((1, d), lambda b, i: (0, 0)),
                  pl.BlockSpec((W_A + W_B, d), lambda b, i: (0, 0))],
        out_specs=pl.BlockSpec((1, tm, d), lambda b, i: (b, i, 0)),
        out_shape=jax.ShapeDtypeStruct((bsz, seq, d), F32),
        compiler_params=pltpu.CompilerParams(dimension_semantics=("arbitrary", "arbitrary"),
                                             vmem_limit_bytes=VMEM_LIMIT),
        name="outproj",
    )(x, o_a, o_b, mod3, g_post.reshape(1, d), w_out_bf16)


def _pad_lanes(w):
    return jnp.pad(w, ((0, 0), (0, LANES - w.shape[1])))


def _pad_input_projection(w_in):
    o_ba = 3 * W_A + W_A
    o_qb = o_ba + 2 * H_A
    o_ik = o_qb + H_B * R_KV + R_KV + W_B + H_IDX * D_IDX
    return jnp.concatenate([w_in[:, :o_ba], _pad_lanes(w_in[:, o_ba:o_qb]),
                            w_in[:, o_qb:o_ik], _pad_lanes(w_in[:, o_ik:])], axis=1)


def _head_lanes(v):
    return jnp.zeros((1, LANES), F32).at[0, H_A:2 * H_A].set(v.astype(F32))


def kernel(x, c, w_ada, b_ada, g_pre, w_in, conv_w, a_log, dt_bias, g_gdn, g_kv, w_uv, rel_bias, w_out, g_post):
    bsz, seq, d = x.shape
    depth = w_ada.shape[0]
    assert seq % TQ == 0 and seq % CHUNK == 0
    k_top = min(TOPK_MAX, seq // 4)
    tm = min(512, seq)
    tl = min(128, seq)
    rb = 8 if bsz % 8 == 0 else (2 if bsz % 2 == 0 else 1)
    bias_tiles = _bias_tiles(rel_bias, seq // TQ)
    for layer in range(depth):
        mod3 = _modulation(c, w_ada[layer], b_ada[layer]).reshape(bsz, 3, d)
        w_pad = _pad_input_projection(w_in[layer]).astype(BF16)
        qkv, za, ba, qb, ckv, zb, iq, ikw = _projection(x, mod3, g_pre[layer], w_pad, tm)
        o_a = _gated_deltanet(qkv, za, ba, conv_w[layer], _head_lanes(a_log[layer]),
                              _head_lanes(dt_bias[layer]), g_gdn[layer], rb, tl, math.gcd(8, rb * (tl // CHUNK)))
        o_b = _dsa_attention(iq, ikw, qb, zb, ckv, g_kv[layer], w_uv[layer].astype(BF16),
                             bias_tiles, k_top, 4 if bsz % 4 == 0 else 1)
        x = _output(x, o_a, o_b, mod3, g_post[layer], w_out[layer].astype(BF16), tm)
    return x
```

```python
import functools
import math

import jax
import jax.numpy as jnp
from jax import lax
from jax.experimental import pallas as pl
from jax.experimental.pallas import tpu as pltpu

F32 = jnp.float32
BF16 = jnp.bfloat16
I32 = jnp.int32
HIGHEST = lax.Precision.HIGHEST

H_A, DK_A, DV_A = 4, 128, 128
W_A = H_A * DV_A
CONV_W = 4
CHUNK = 64
H_B, R_KV, DV_B = 4, 128, 128
W_B = H_B * DV_B
H_IDX, D_IDX = 8, 64
TOPK_MAX = 256
N_BUCKETS, MAX_EXACT, MAX_DIST = 32, 16, 128
EPS = 1e-6
LOG2E = math.log2(math.e)

LANES = 128
TQ = 128
FIELD = 15
VMEM_LIMIT = 58 * 1024 * 1024

C_QKV = (0, 3 * W_A)
C_ZA = (C_QKV[1], C_QKV[1] + W_A)
C_BA = (C_ZA[1], C_ZA[1] + LANES)
C_QB = (C_BA[1], C_BA[1] + H_B * R_KV)
C_CKV = (C_QB[1], C_QB[1] + R_KV)
C_ZB = (C_CKV[1], C_CKV[1] + W_B)
C_IQ = (C_ZB[1], C_ZB[1] + H_IDX * D_IDX)
C_IKW = (C_IQ[1], C_IQ[1] + LANES)
D_IN_PAD = C_IKW[1]
IW_LANE = D_IDX

NEG_INF_KEY = -2139095041


def _sigmoid(x):
    return 1.0 / (1.0 + jnp.exp(-x))


def _silu(x):
    return x * _sigmoid(x)


def _softplus(x):
    return jnp.maximum(x, 0.0) + jnp.log1p(jnp.exp(-jnp.abs(x)))


def _dot(a, b, precision=None):
    return jnp.dot(a, b, precision=precision, preferred_element_type=F32)


def _dot_nt(a, b, precision=None):
    return lax.dot_general(a, b, (((1,), (1,)), ((), ())), precision=precision,
                           preferred_element_type=F32)


def _dot_tn(a, b, precision=None):
    return lax.dot_general(a, b, (((0,), (0,)), ((), ())), precision=precision,
                           preferred_element_type=F32)


def _bdot(a, b, precision=None):
    return _dot(a.astype(BF16), b.astype(BF16))


def _bdot_nt(a, b, precision=None):
    return _dot_nt(a.astype(BF16), b.astype(BF16))


def _bdot_tn(a, b, precision=None):
    return _dot_tn(a.astype(BF16), b.astype(BF16))


def _mod_kernel(c_ref, w_ref, b_ref, o_ref):
    c = c_ref[...]
    o_ref[...] = _dot(_silu(c), w_ref[...], HIGHEST) + b_ref[...]


def _modulation(c, w_ada, b_ada):
    bsz, d = c.shape
    n = w_ada.shape[1]
    tn = 512
    return pl.pallas_call(
        _mod_kernel,
        grid=(n // tn,),
        in_specs=[pl.BlockSpec((bsz, d), lambda j: (0, 0)),
                  pl.BlockSpec((d, tn), lambda j: (0, j)),
                  pl.BlockSpec((1, tn), lambda j: (0, j))],
        out_specs=pl.BlockSpec((bsz, tn), lambda j: (0, j)),
        out_shape=jax.ShapeDtypeStruct((bsz, n), F32),
        compiler_params=pltpu.CompilerParams(dimension_semantics=("arbitrary",),
                                             vmem_limit_bytes=VMEM_LIMIT),
        name="mod",
    )(c, w_ada, b_ada.reshape(1, n))


def _proj_kernel(x_ref, mod_ref, g_ref, w_ref,
                 qkv_ref, za_ref, ba_ref, qb_ref, ckv_ref, zb_ref, iq_ref, ikw_ref):
    x = x_ref[0]
    ms = jnp.mean(x * x, axis=-1, keepdims=True)
    xn = x * lax.rsqrt(ms + EPS)
    shift = mod_ref[0, 0:1, :]
    scale = mod_ref[0, 1:2, :]
    h = (xn * g_ref[...]) * (1.0 + scale) + shift
    hb = h.astype(BF16)

    def mm(cols):
        return _dot(hb, w_ref[:, cols[0]:cols[1]])

    qkv_ref[0] = mm(C_QKV)
    za_ref[0] = mm(C_ZA)
    ba_ref[0] = mm(C_BA)
    qb_ref[0] = mm(C_QB).astype(BF16)
    ckv_ref[0] = mm(C_CKV)
    zb_ref[0] = mm(C_ZB)
    iq_ref[0] = mm(C_IQ).astype(BF16)
    ikw_ref[0] = mm(C_IKW)


def _projection(x, mod3, g_pre, w_pad, tm):
    bsz, seq, d = x.shape
    widths = [(C_QKV, F32), (C_ZA, F32), (C_BA, F32), (C_QB, BF16),
              (C_CKV, F32), (C_ZB, F32), (C_IQ, BF16), (C_IKW, F32)]
    out_shape = [jax.ShapeDtypeStruct((bsz, seq, c[1] - c[0]), dt) for c, dt in widths]
    out_specs = [pl.BlockSpec((1, tm, c[1] - c[0]), lambda b, i: (b, i, 0)) for c, _ in widths]
    return pl.pallas_call(
        _proj_kernel,
        grid=(bsz, seq // tm),
        in_specs=[pl.BlockSpec((1, tm, d), lambda b, i: (b, i, 0)),
                  pl.BlockSpec((1, 3, d), lambda b, i: (b, 0, 0)),
                  pl.BlockSpec((1, d), lambda b, i: (0, 0)),
                  pl.BlockSpec((d, D_IN_PAD), lambda b, i: (0, 0))],
        out_specs=out_specs,
        out_shape=out_shape,
        compiler_params=pltpu.CompilerParams(dimension_semantics=("arbitrary", "arbitrary"),
                                             vmem_limit_bytes=VMEM_LIMIT),
        name="proj",
    )(x, mod3, g_pre.reshape(1, d), w_pad)


def _tri_inverse_many(lmats, level_masks, eye):
    ts = [eye - jnp.where(level_masks[0], lm, 0.0) for lm in lmats]
    for m in level_masks[1:]:
        xs = [_bdot(jnp.where(m, lm, 0.0), t) for lm, t in zip(lmats, ts)]
        ts = [t - _bdot(t, x) for t, x in zip(ts, xs)]
    return ts


def _gdn_kernel(qkv_ref, za_ref, ba_ref, cw_ref, alog_ref, dtb_ref, gn_ref, o_ref,
                halo_ref, s_ref, gc_ref, beta_ref, gct_ref, u_ref, wq_ref, kg_ref, at_ref,
                *, rb, tl, group_a):
    n_ch = tl // CHUNK
    assert group_a % n_ch == 0
    li = pl.program_id(1)

    @pl.when(li == 0)
    def _():
        halo_ref[...] = jnp.zeros_like(halo_ref)
        s_ref[...] = jnp.zeros_like(s_ref)

    r_i = lax.broadcasted_iota(I32, (tl, tl), 0)
    c_i = lax.broadcasted_iota(I32, (tl, tl), 1)
    shift = CHUNK.bit_length() - 1
    tri = jnp.where((c_i <= r_i) & ((r_i >> shift) == (c_i >> shift)), 1.0, 0.0)
    for r in range(rb):
        ba = ba_ref[r]
        beta_ref[r] = _sigmoid(ba)
        g = -jnp.exp(alog_ref[...]) * _softplus(ba + dtb_ref[...])
        gc = _dot(tri, g, HIGHEST)
        gc_ref[r] = gc
        gct = gc.T
        for c in range(n_ch):
            gct_ref[r * n_ch + c] = gct[:, c * CHUNK:(c + 1) * CHUNK]

    row = lax.broadcasted_iota(I32, (CHUNK, CHUNK), 0)
    col = lax.broadcasted_iota(I32, (CHUNK, CHUNK), 1)
    causal = row >= col
    strict = row > col
    eye = jnp.where(row == col, 1.0, 0.0)
    level_masks = []
    s = 1
    while s < CHUNK:
        ls = s.bit_length() - 1
        level_masks.append(((row >> (ls + 1)) == (col >> (ls + 1)))
                           & (((row >> ls) & 1) == 1) & (((col >> ls) & 1) == 0))
        s *= 2

    def phase_a(ig, carry):
        probs = []
        for j in range(group_a):
            c = j % n_ch
            r = ig * (group_a // n_ch) + j // n_ch
            it = r * n_ch + c
            base = c * CHUNK
            gc_c = gc_ref[r, pl.ds(base, CHUNK), :]
            beta_c = beta_ref[r, pl.ds(base, CHUNK), :]
            gct_c = gct_ref[it]
            for h in range(H_A):
                def conv_silu(sec):
                    c0 = sec * W_A + h * DK_A
                    if c == 0:
                        win = jnp.concatenate([halo_ref[r, :, c0:c0 + DK_A],
                                               qkv_ref[r, 0:CHUNK, c0:c0 + DK_A]], axis=0)
                    else:
                        win = qkv_ref[r, base - 8:base + CHUNK, c0:c0 + DK_A]
                    w0, w1, w2, w3 = (cw_ref[t:t + 1, c0:c0 + DK_A] for t in range(CONV_W))
                    prev = pltpu.roll(win, 1, axis=0)
                    near = win * w3 + prev * w2
                    far = pltpu.roll(win * w1 + prev * w0, 2, axis=0)
                    return _silu((near + far)[8:8 + CHUNK])

                q = conv_silu(0)
                k = conv_silu(1)
                v = conv_silu(2)
                q = q * lax.rsqrt(jnp.sum(q * q, axis=-1, keepdims=True) + EPS) * (DK_A ** -0.5)
                k = k * lax.rsqrt(jnp.sum(k * k, axis=-1, keepdims=True) + EPS)
                beta = beta_c[:, h:h + 1]
                gcol = gc_c[:, H_A + h:H_A + h + 1]
                grow = gct_c[H_A + h:H_A + h + 1, :]
                glast = gc_c[CHUNK - 1:CHUNK, H_A + h:H_A + h + 1]
                decay = jnp.exp(jnp.where(causal, gcol - grow, -jnp.inf))
                ecol = jnp.exp(gcol)
                kb = k * beta
                hc = slice(h * DV_A, (h + 1) * DV_A)
                wq_ref[it, h, CHUNK:2 * CHUNK, :] = (q * ecol).astype(BF16)
                kg_ref[r, pl.ds(base, CHUNK), hc] = (k * jnp.exp(glast - gcol)).astype(BF16)
                a2 = _bdot_nt(jnp.concatenate([kb, q], axis=0), k)
                at_ref[it, h] = (a2[CHUNK:] * decay).astype(BF16)
                probs.append(dict(
                    r=r, it=it, h=h, base=base, hc=hc,
                    lmat=jnp.where(strict, a2[:CHUNK] * decay, 0.0),
                    rhs=jnp.concatenate([v * beta, kb * ecol], axis=1).astype(BF16)))
        tmats = _tri_inverse_many([p["lmat"] for p in probs], level_masks, eye)
        for p, tmat in zip(probs, tmats):
            uw = _dot(tmat.astype(BF16), p["rhs"])
            u_ref[p["r"], pl.ds(p["base"], CHUNK), p["hc"]] = uw[:, :DV_A]
            wq_ref[p["it"], p["h"], 0:CHUNK, :] = uw[:, DV_A:].astype(BF16)
        return carry

    lax.fori_loop(0, rb * n_ch // group_a, phase_a, 0)
    halo_ref[...] = qkv_ref[:, tl - 8:tl, :]

    def phase_b(c, carry):
        base = pl.multiple_of(c * CHUNK, CHUNK)
        chains = [(r, h) for r in range(rb) for h in range(H_A)]
        hcs = [slice(h * DV_A, (h + 1) * DV_A) for _, h in chains]
        sts = [s_ref[r, h] for r, h in chains]
        wss = [_dot(wq_ref[r * n_ch + c, h], st.astype(BF16)) for (r, h), st in zip(chains, sts)]
        vnbs = [(u_ref[r, pl.ds(base, CHUNK), hc] - ws[:CHUNK]).astype(BF16)
                for (r, h), hc, ws in zip(chains, hcs, wss)]
        upds = [_dot_tn(kg_ref[r, pl.ds(base, CHUNK), hc], vnb)
                for (r, h), hc, vnb in zip(chains, hcs, vnbs)]
        for (r, h), st, upd in zip(chains, sts, upds):
            glast = gc_ref[r, pl.ds(base + CHUNK - 1, 1), :][:, H_A + h:H_A + h + 1]
            s_ref[r, h] = st * jnp.exp(glast) + upd
        for (r, h), hc, ws, vnb in zip(chains, hcs, wss, vnbs):
            o = ws[CHUNK:] + _dot(at_ref[r * n_ch + c, h], vnb)
            on = o * lax.rsqrt(jnp.mean(o * o, axis=-1, keepdims=True) + EPS) * gn_ref[...]
            z = za_ref[r, pl.ds(base, CHUNK), hc]
            o_ref[r, pl.ds(base, CHUNK), hc] = (on * _silu(z)).astype(BF16)
        return carry

    lax.fori_loop(0, n_ch, phase_b, 0)


def _gated_deltanet(qkv, za, ba, conv_w, alog_vec, dtb_vec, g_norm, rb, tl, group_a):
    bsz, seq, _ = qkv.shape
    n_ch = tl // CHUNK
    kern = functools.partial(_gdn_kernel, rb=rb, tl=tl, group_a=group_a)
    return pl.pallas_call(
        kern,
        grid=(bsz // rb, seq // tl),
        in_specs=[pl.BlockSpec((rb, tl, 3 * W_A), lambda b, i: (b, i, 0)),
                  pl.BlockSpec((rb, tl, W_A), lambda b, i: (b, i, 0)),
                  pl.BlockSpec((rb, tl, LANES), lambda b, i: (b, i, 0)),
                  pl.BlockSpec((CONV_W, 3 * W_A), lambda b, i: (0, 0)),
                  pl.BlockSpec((1, LANES), lambda b, i: (0, 0)),
                  pl.BlockSpec((1, LANES), lambda b, i: (0, 0)),
                  pl.BlockSpec((1, DV_A), lambda b, i: (0, 0))],
        out_specs=pl.BlockSpec((rb, tl, W_A), lambda b, i: (b, i, 0)),
        out_shape=jax.ShapeDtypeStruct((bsz, seq, W_A), BF16),
        scratch_shapes=[pltpu.VMEM((rb, 8, 3 * W_A), F32),
                        pltpu.VMEM((rb, H_A, DK_A, DV_A), F32),
                        pltpu.VMEM((rb, tl, LANES), F32),
                        pltpu.VMEM((rb, tl, LANES), F32),
                        pltpu.VMEM((rb * n_ch, LANES, CHUNK), F32),
                        pltpu.VMEM((rb, tl, W_A), F32),
                        pltpu.VMEM((rb * n_ch, H_A, 2 * CHUNK, DK_A), BF16),
                        pltpu.VMEM((rb, tl, W_A), BF16),
                        pltpu.VMEM((rb * n_ch, H_A, CHUNK, CHUNK), BF16)],
        compiler_params=pltpu.CompilerParams(dimension_semantics=("arbitrary", "arbitrary"),
                                             vmem_limit_bytes=VMEM_LIMIT),
        name="gdn",
    )(qkv, za, ba, conv_w, alog_vec, dtb_vec, g_norm.reshape(1, DV_A))


def _bias_kernel(rb_ref, o_ref):
    d = pl.program_id(0)
    kj = lax.broadcasted_iota(I32, (TQ, TQ), 0)
    qi = lax.broadcasted_iota(I32, (TQ, TQ), 1)
    dist = d * TQ + qi - kj
    n = jnp.maximum(dist, 0)
    nf = jnp.maximum(n, 1).astype(F32)
    large = MAX_EXACT + (jnp.log(nf / MAX_EXACT) / math.log(MAX_DIST / MAX_EXACT)
                         * (N_BUCKETS - MAX_EXACT)).astype(I32)
    large = jnp.minimum(large, N_BUCKETS - 1)
    bucket = jnp.where(n < MAX_EXACT, n, large)
    for h in range(H_B):
        acc = jnp.zeros((TQ, TQ), F32)
        for kb in range(N_BUCKETS):
            acc = jnp.where(bucket == kb, rb_ref[kb, h], acc)
        o_ref[0, :, h * TQ:(h + 1) * TQ] = acc * LOG2E


def _bias_tiles(rel_bias, n_diag):
    return pl.pallas_call(
        _bias_kernel,
        grid=(n_diag,),
        in_specs=[pl.BlockSpec(memory_space=pltpu.SMEM)],
        out_specs=pl.BlockSpec((1, TQ, H_B * TQ), lambda d: (d, 0, 0)),
        out_shape=jax.ShapeDtypeStruct((n_diag, TQ, H_B * TQ), F32),
        compiler_params=pltpu.CompilerParams(dimension_semantics=("arbitrary",)),
        name="bias",
    )(rel_bias)


def _fold_keys(x, op):
    x = op(x.reshape(4, x.shape[0] // 32, 8, x.shape[-1]), axis=1)
    return op(x, axis=0)


def _dsa_kernel(iq_ref, ikwq_ref, qb_ref, zb_ref, ikw_ref, ckv_ref, gkv_ref, wuv_ref, bias_ref,
                o_ref,
                kvn_ref, kvt_ref, iklo_ref, ikhi_ref, key_ref, w_ref, am_ref, lg_ref, acc_ref,
                *, rb, seq, k_top):
    qi = pl.program_id(1)
    n_kc = qi + 1
    hq = H_B * TQ
    rows = range(rb)

    @pl.when(qi == 0)
    def _():
        for r in rows:
            for c in range(seq // TQ):
                sl = slice(c * TQ, (c + 1) * TQ)
                ckv = ckv_ref[r, sl, :]
                ms = jnp.mean(ckv * ckv, axis=-1, keepdims=True)
                kvn = (ckv * lax.rsqrt(ms + EPS)) * gkv_ref[...]
                kvn_ref[r, sl, :] = kvn.astype(BF16)
                kvt_ref[r, c] = kvn.T.astype(BF16)
                ikw = ikw_ref[r, sl, :]
                lane = lax.broadcasted_iota(I32, ikw.shape, 1)
                lo = jnp.where(lane < D_IDX, ikw, 0.0)
                iklo_ref[r, sl, :] = lo.astype(BF16)
                ikhi_ref[r, sl, :] = pltpu.roll(lo, D_IDX, axis=1).astype(BF16)

    iq4s, iwts = [], []
    for r in rows:
        iq = iq_ref[r]
        iq4s.append(jnp.concatenate(
            [iq[:, p * LANES:(p + 1) * LANES] for p in range(H_IDX // 2)], axis=0))
        iwts.append((ikwq_ref[r] * (H_IDX ** -0.5 * D_IDX ** -0.5)).T)

    key_j = lax.broadcasted_iota(I32, (TQ, TQ), 0)
    qry_t = qi * TQ + lax.broadcasted_iota(I32, (TQ, TQ), 1)

    def causal_mask(c):
        return (c * TQ + key_j) <= qry_t

    q4s = []
    for r in rows:
        qb = qb_ref[r]
        q4s.append(jnp.concatenate([qb[:, h * R_KV:(h + 1) * R_KV] for h in range(H_B)], axis=0))
    scale = R_KV ** -0.5 * LOG2E

    def pair_loop(body, carry):
        carry = lax.fori_loop(0, n_kc >> 1, lambda i, cr: body((2 * i, 2 * i + 1), cr), carry)
        return lax.cond((n_kc & 1) == 1, lambda cr: body((n_kc - 1,), cr), lambda cr: cr, carry)

    half = TQ // 2
    guards = jnp.int32(-(2 ** 31) + 2 ** 15)
    ones2 = jnp.int32(2 ** 16 + 1)
    fmax = 2 ** FIELD - 1

    def pack_fields(f):
        return (f[half:] << 16) | f[:half] | guards

    def field1(key):
        return lax.shift_right_logical(key, 32 - FIELD) ^ (1 << (FIELD - 1))

    def field2(key):
        return lax.shift_right_logical(key, 32 - 2 * FIELD) & fmax

    def score_chunks(cs, m0s):
        m0s = list(m0s)
        nc = len(cs)
        ks = pl.ds(pl.multiple_of(cs[0] * TQ, TQ), nc * TQ)
        rels = [_dot_nt(jnp.concatenate([iklo_ref[r, ks, :], ikhi_ref[r, ks, :]], axis=0),
                        iq4s[r]) for r in rows]
        sts = [_dot_nt(kvn_ref[r, ks, :], q4s[r]) for r in rows]
        for j, c in enumerate(cs):
            for r in rows:
                re = rels[r][j * TQ:(j + 1) * TQ]
                ro = rels[r][(nc + j) * TQ:(nc + j + 1) * TQ]
                s = None
                for p in range(H_IDX // 2):
                    ps = slice(p * TQ, (p + 1) * TQ)
                    we = iwts[r][IW_LANE + 2 * p:IW_LANE + 2 * p + 1, :]
                    wo = iwts[r][IW_LANE + 2 * p + 1:IW_LANE + 2 * p + 2, :]
                    t = jnp.maximum(re[:, ps], 0.0) * we + jnp.maximum(ro[:, ps], 0.0) * wo
                    s = t if s is None else s + t
                s = jnp.where(s == 0.0, 0.0, s)
                s = jnp.where(causal_mask(c), s, -jnp.inf)
                bits = pltpu.bitcast(s, I32)
                key = jnp.where(bits < 0, bits ^ 0x7FFFFFFF, bits)
                key_ref[r, c] = key
                w_ref[r, c] = pack_fields(field1(key))
                lg = sts[r][j * TQ:(j + 1) * TQ] * scale + bias_ref[qi - c]
                lg_ref[r, c] = lg
                m0s[r] = jnp.maximum(m0s[r], _fold_keys(lg, jnp.max))
        return tuple(m0s)

    m0s = pair_loop(score_chunks, tuple(jnp.full((8, hq), -jnp.inf, F32) for _ in rows))

    kf = float(k_top)
    n_pairs = (n_kc + 1) >> 1
    sign = jnp.int32(-2 ** 31)

    @pl.when((n_kc & 1) == 1)
    def _():
        for r in rows:
            key_ref[r, n_kc] = jnp.full((TQ, TQ), -2 ** 31, I32)

    def count_ge(cands):
        def body(i, accs):
            out = list(accs)
            for c in (2 * i, 2 * i + 1):
                for r in rows:
                    hit = jnp.where(key_ref[r, c] >= cands[r], 1.0, 0.0)
                    out[r] = out[r] + _fold_keys(hit, jnp.sum)
            return tuple(out)
        accs = lax.fori_loop(0, n_pairs, body, tuple(jnp.zeros((8, TQ), F32) for _ in rows))
        return [jnp.sum(a, axis=0, keepdims=True) for a in accs]

    def count_fields(cands):
        pairs = [(cu << 16) | cu for cu in cands]
        def body(i, accs):
            out = list(accs)
            for c in (2 * i, 2 * i + 1):
                for r in rows:
                    d = w_ref[r, c] - pairs[r]
                    hit = lax.shift_right_logical(d, 15) & ones2
                    out[r] = out[r] + _fold_keys(hit, jnp.sum)
            return tuple(out)
        accs = lax.fori_loop(0, n_pairs, body, tuple(jnp.zeros((8, TQ), I32) for _ in rows))
        both = [(a & 0xFFFF) + lax.shift_right_logical(a, 16) for a in accs]
        return [jnp.sum(b, axis=0, keepdims=True).astype(F32) for b in both]

    def field_level(bases, cges):
        def bit_body(i, carry):
            ts, cg = carry
            bit = lax.shift_left(jnp.int32(1), FIELD - 1 - i)
            cands = [t | bit for t in ts]
            cnts = [b + n for b, n in zip(bases, count_fields([cu[0:1, :] for cu in cands]))]
            take = [cnt >= kf for cnt in cnts]
            return (tuple(jnp.where(tk, cu, t) for tk, cu, t in zip(take, cands, ts)),
                    tuple(jnp.where(tk, cnt, g) for tk, cnt, g in zip(take, cnts, cg)))
        ts, cges = lax.fori_loop(0, FIELD, bit_body,
                                 (tuple(jnp.zeros((8, TQ), I32) for _ in rows), tuple(cges)))
        ps = [t[0:1, :] for t in ts]
        above = count_fields([jnp.minimum(p + 1, fmax) for p in ps])
        above = [b + jnp.where(p == fmax, 0.0, n) for b, p, n in zip(bases, ps, above)]
        return ps, above, cges

    @pl.when((n_kc & 1) == 1)
    def _():
        for r in rows:
            w_ref[r, n_kc] = jnp.full((half, TQ), guards, I32)

    cges = [jnp.full((1, TQ), kf, F32) for _ in rows]
    p1s, above1, cges = field_level([0.0 for _ in rows], cges)

    def build_level2(c, carry):
        for r in rows:
            key = key_ref[r, c]
            w_ref[r, c] = pack_fields(jnp.where(field1(key) == p1s[r], field2(key), 0))
        return carry

    lax.fori_loop(0, n_kc, build_level2, 0)
    p2s, _, cges = field_level(above1, cges)

    def bit_body(i, carry):
        tus, cges = carry
        bit = lax.shift_left(jnp.int32(1), 31 - i)
        cands = [tu | bit for tu in tus]
        cnts = count_ge([(cu ^ sign)[0:1, :] for cu in cands])
        take = [cnt >= kf for cnt in cnts]
        return (tuple(jnp.where(t, cu, tu) for t, cu, tu in zip(take, cands, tus)),
                tuple(jnp.where(t, cnt, cge) for t, cnt, cge in zip(take, cnts, cges)))

    tus = tuple(jnp.broadcast_to((p1 << (32 - FIELD)) | (p2 << (32 - 2 * FIELD)), (8, TQ))
                for p1, p2 in zip(p1s, p2s))
    tus, cges = lax.fori_loop(2 * FIELD, 32, bit_body, (tus, tuple(cges)))
    thrs = [(tu ^ sign)[0:1, :] for tu in tus]
    qrow = qi * TQ + lax.broadcasted_iota(I32, (1, TQ), 1)
    simple_all = None
    for r in rows:
        simple = (cges[r] == kf) | ((thrs[r] == NEG_INF_KEY) & (qrow < k_top))
        simple_all = simple if simple_all is None else (simple_all & simple)
    all_simple = jnp.min(jnp.where(simple_all, 1.0, 0.0)) > 0.5

    @pl.when(all_simple)
    def _():
        def body(c, carry):
            cm = causal_mask(c)
            for r in rows:
                sel = (key_ref[r, c] >= thrs[r]) & cm
                am_ref[r, c] = jnp.where(sel, 0.0, -jnp.inf)
            return carry
        lax.fori_loop(0, n_kc, body, 0)

    @pl.when(jnp.logical_not(all_simple))
    def _():
        lower = jnp.where(lax.broadcasted_iota(I32, (TQ, TQ), 1)
                          <= lax.broadcasted_iota(I32, (TQ, TQ), 0), 1.0, 0.0).astype(BF16)

        def count_gt(c, cgs):
            return tuple(cg + _fold_keys(jnp.where(key_ref[r, c] > thrs[r], 1.0, 0.0), jnp.sum)
                         for r, cg in zip(rows, cgs))

        cgs = lax.fori_loop(0, n_kc, count_gt, tuple(jnp.zeros((8, TQ), F32) for _ in rows))
        needs = [kf - jnp.sum(cg, axis=0, keepdims=True) for cg in cgs]

        def body(c, seens):
            cm = causal_mask(c)
            out = []
            for r in rows:
                key = key_ref[r, c]
                eq = key == thrs[r]
                eqf = jnp.where(eq, 1.0, 0.0)
                rank = seens[r] + _dot(lower, eqf.astype(BF16))
                sel = (key > thrs[r]) | (eq & (rank <= needs[r]))
                am_ref[r, c] = jnp.where(sel & cm, 0.0, -jnp.inf)
                out.append(seens[r] + jnp.sum(eqf, axis=0, keepdims=True))
            return tuple(out)
        lax.fori_loop(0, n_kc, body, tuple(jnp.zeros((1, TQ), F32) for _ in rows))

    def masked_logits(r, c):
        return lg_ref[r, c] + jnp.concatenate([am_ref[r, c]] * H_B, axis=1)

    def softmax_pv(shifts):
        acc_ref[...] = jnp.zeros_like(acc_ref)

        def body(cs, ls):
            out = list(ls)
            for r in rows:
                ps = [jnp.exp2(masked_logits(r, c) - shifts[r]) for c in cs]
                kvt = jnp.concatenate([kvt_ref[r, c] for c in cs], axis=1)
                acc_ref[r] += _dot(kvt, jnp.concatenate(ps, axis=0).astype(BF16))
                for p in ps:
                    out[r] = out[r] + _fold_keys(p, jnp.sum)
            return tuple(out)

        ls = pair_loop(body, tuple(jnp.zeros((8, hq), F32) for _ in rows))
        return tuple(jnp.sum(l, axis=0, keepdims=True) for l in ls)

    lsums = softmax_pv([jnp.max(m, axis=0, keepdims=True) for m in m0s])
    lmin = jnp.min(jnp.concatenate(lsums, axis=1))

    def exact_shift(_):
        def body(cs, ms):
            out = list(ms)
            for c in cs:
                for r in rows:
                    out[r] = jnp.maximum(out[r], _fold_keys(masked_logits(r, c), jnp.max))
            return tuple(out)
        ms = pair_loop(body, tuple(jnp.full((8, hq), -jnp.inf, F32) for _ in rows))
        return softmax_pv([jnp.max(m, axis=0, keepdims=True) for m in ms])

    lsums = lax.cond(lmin >= 2.0 ** -40, lambda _: lsums, exact_shift, 0)
    for r in rows:
        ot = (acc_ref[r] / lsums[r]).astype(BF16)
        for h in range(H_B):
            hs = slice(h * DV_B, (h + 1) * DV_B)
            y = _dot_tn(ot[:, h * TQ:(h + 1) * TQ], wuv_ref[h])
            o_ref[r, :, hs] = (y * _silu(zb_ref[r, :, hs])).astype(BF16)


def _dsa_attention(iq, ikw, qb, zb, ckv, g_kv, w_uv_bf16, bias_tiles, k_top, rb):
    bsz, seq, _ = iq.shape
    n_q = seq // TQ
    hq = H_B * TQ
    kern = functools.partial(_dsa_kernel, rb=rb, seq=seq, k_top=k_top)
    return pl.pallas_call(
        kern,
        grid=(bsz // rb, n_q),
        in_specs=[pl.BlockSpec((rb, TQ, H_IDX * D_IDX), lambda b, i: (b, i, 0)),
                  pl.BlockSpec((rb, TQ, LANES), lambda b, i: (b, i, 0)),
                  pl.BlockSpec((rb, TQ, H_B * R_KV), lambda b, i: (b, i, 0)),
                  pl.BlockSpec((rb, TQ, W_B), lambda b, i: (b, i, 0)),
                  pl.BlockSpec((rb, seq, LANES), lambda b, i: (b, 0, 0),
                               pipeline_mode=pl.Buffered(1)),
                  pl.BlockSpec((rb, seq, R_KV), lambda b, i: (b, 0, 0),
                               pipeline_mode=pl.Buffered(1)),
                  pl.BlockSpec((1, R_KV), lambda b, i: (0, 0)),
                  pl.BlockSpec((H_B, R_KV, DV_B), lambda b, i: (0, 0, 0)),
                  pl.BlockSpec((n_q, TQ, hq), lambda b, i: (0, 0, 0),
                               pipeline_mode=pl.Buffered(1))],
        out_specs=pl.BlockSpec((rb, TQ, W_B), lambda b, i: (b, i, 0)),
        out_shape=jax.ShapeDtypeStruct((bsz, seq, W_B), BF16),
        scratch_shapes=[pltpu.VMEM((rb, seq, R_KV), BF16),
                        pltpu.VMEM((rb, n_q, R_KV, TQ), BF16),
                        pltpu.VMEM((rb, seq, LANES), BF16),
                        pltpu.VMEM((rb, seq, LANES), BF16),
                        pltpu.VMEM((rb, n_q + 1, TQ, TQ), I32),
                        pltpu.VMEM((rb, n_q + 1, TQ // 2, TQ), I32),
                        pltpu.VMEM((rb, n_q, TQ, TQ), F32),
                        pltpu.VMEM((rb, n_q, TQ, hq), F32),
                        pltpu.VMEM((rb, R_KV, hq), F32)],
        compiler_params=pltpu.CompilerParams(dimension_semantics=("arbitrary", "arbitrary"),
                                             vmem_limit_bytes=VMEM_LIMIT),
        name="dsa",
    )(iq, ikw, qb, zb, ikw, ckv, g_kv.reshape(1, R_KV), w_uv_bf16, bias_tiles)


def _out_kernel(x_ref, oa_ref, ob_ref, mod_ref, g_ref, w_ref, o_ref):
    mix_in = jnp.concatenate([oa_ref[0], ob_ref[0]], axis=-1)
    mix = _dot(mix_in, w_ref[...])
    ms = jnp.mean(mix * mix, axis=-1, keepdims=True)
    normed = (mix * lax.rsqrt(ms + EPS)) * g_ref[...]
    o_ref[0] = x_ref[0] + mod_ref[0, 2:3, :] * normed


def _output(x, o_a, o_b, mod3, g_post, w_out_bf16, tm):
    bsz, seq, d = x.shape
    return pl.pallas_call(
        _out_kernel,
        grid=(bsz, seq // tm),
        in_specs=[pl.BlockSpec((1, tm, d), lambda b, i: (b, i, 0)),
                  pl.BlockSpec((1, tm, W_A), lambda b, i: (b, i, 0)),
                  pl.BlockSpec((1, tm, W_B), lambda b, i: (b, i, 0)),
                  pl.BlockSpec((1, 3, d), lambda b, i: (b, 0, 0)),
                  pl.BlockSpec((1, d), lambda b, i: (0, 0)),
                  pl.BlockSpec((W_A + W_B, d), lambda b, i: (0, 0))],
        out_specs=pl.BlockSpec((1, tm, d), lambda b, i: (b, i, 0)),
        out_shape=jax.ShapeDtypeStruct((bsz, seq, d), F32),
        compiler_params=pltpu.CompilerParams(dimension_semantics=("arbitrary", "arbitrary"),
                                             vmem_limit_bytes=VMEM_LIMIT),
        name="outproj",
    )(x, o_a, o_b, mod3, g_post.reshape(1, d), w_out_bf16)


def _pad_lanes(w):
    return jnp.pad(w, ((0, 0), (0, LANES - w.shape[1])))


def _pad_input_projection(w_in):
    o_ba = 3 * W_A + W_A
    o_qb = o_ba + 2 * H_A
    o_ik = o_qb + H_B * R_KV + R_KV + W_B + H_IDX * D_IDX
    return jnp.concatenate([w_in[:, :o_ba], _pad_lanes(w_in[:, o_ba:o_qb]),
                            w_in[:, o_qb:o_ik], _pad_lanes(w_in[:, o_ik:])], axis=1)


def _head_lanes(v):
    return jnp.zeros((1, LANES), F32).at[0, H_A:2 * H_A].set(v.astype(F32))


def kernel(x, c, w_ada, b_ada, g_pre, w_in, conv_w, a_log, dt_bias, g_gdn, g_kv, w_uv, rel_bias, w_out, g_post):
    bsz, seq, d = x.shape
    depth = w_ada.shape[0]
    assert seq % TQ == 0 and seq % CHUNK == 0
    k_top = min(TOPK_MAX, seq // 4)
    tm = min(512, seq)
    tl = min(128, seq)
    rb = 8 if bsz % 8 == 0 else (2 if bsz % 2 == 0 else 1)
    bias_tiles = _bias_tiles(rel_bias, seq // TQ)
    for layer in range(depth):
        mod3 = _modulation(c, w_ada[layer], b_ada[layer]).reshape(bsz, 3, d)
        w_pad = _pad_input_projection(w_in[layer]).astype(BF16)
        qkv, za, ba, qb, ckv, zb, iq, ikw = _projection(x, mod3, g_pre[layer], w_pad, tm)
        o_a = _gated_deltanet(qkv, za, ba, conv_w[layer], _head_lanes(a_log[layer]),
                              _head_lanes(dt_bias[layer]), g_gdn[layer], rb, tl,
                              (tl // CHUNK) * math.gcd(max(8 // (tl // CHUNK), 1), rb))
        o_b = _dsa_attention(iq, ikw, qb, zb, ckv, g_kv[layer], w_uv[layer].astype(BF16),
                             bias_tiles, k_top, 4 if bsz % 4 == 0 else 1)
        x = _output(x, o_a, o_b, mod3, g_post[layer], w_out[layer].astype(BF16), tm)
    return x
```

```python
import functools
import math

import jax
import jax.numpy as jnp
from jax import lax
from jax.experimental import pallas as pl
from jax.experimental.pallas import tpu as pltpu

F32 = jnp.float32
BF16 = jnp.bfloat16
I32 = jnp.int32
HIGHEST = lax.Precision.HIGHEST

H_A, DK_A, DV_A = 4, 128, 128
W_A = H_A * DV_A
CONV_W = 4
CHUNK = 64
H_B, R_KV, DV_B = 4, 128, 128
W_B = H_B * DV_B
H_IDX, D_IDX = 8, 64
TOPK_MAX = 256
N_BUCKETS, MAX_EXACT, MAX_DIST = 32, 16, 128
EPS = 1e-6
LOG2E = math.log2(math.e)

LANES = 128
TQ = 128
FIELD = 15
VMEM_LIMIT = 58 * 1024 * 1024

C_QKV = (0, 3 * W_A)
C_ZA = (C_QKV[1], C_QKV[1] + W_A)
C_BA = (C_ZA[1], C_ZA[1] + LANES)
C_QB = (C_BA[1], C_BA[1] + H_B * R_KV)
C_CKV = (C_QB[1], C_QB[1] + R_KV)
C_ZB = (C_CKV[1], C_CKV[1] + W_B)
C_IQ = (C_ZB[1], C_ZB[1] + H_IDX * D_IDX)
C_IKW = (C_IQ[1], C_IQ[1] + LANES)
D_IN_PAD = C_IKW[1]
IW_LANE = D_IDX

NEG_INF_KEY = -2139095041


def _sigmoid(x):
    return 1.0 / (1.0 + jnp.exp(-x))


def _silu(x):
    return x * _sigmoid(x)


def _softplus(x):
    return jnp.maximum(x, 0.0) + jnp.log1p(jnp.exp(-jnp.abs(x)))


def _dot(a, b, precision=None):
    return jnp.dot(a, b, precision=precision, preferred_element_type=F32)


def _dot_nt(a, b, precision=None):
    return lax.dot_general(a, b, (((1,), (1,)), ((), ())), precision=precision,
                           preferred_element_type=F32)


def _dot_tn(a, b, precision=None):
    return lax.dot_general(a, b, (((0,), (0,)), ((), ())), precision=precision,
                           preferred_element_type=F32)


def _bdot(a, b):
    return _dot(a.astype(BF16), b.astype(BF16))


def _bdot_nt(a, b):
    return _dot_nt(a.astype(BF16), b.astype(BF16))


def _mod_kernel(c_ref, w_ref, b_ref, o_ref):
    c = c_ref[...]
    o_ref[...] = _dot(_silu(c), w_ref[...], HIGHEST) + b_ref[...]


def _modulation(c, w_ada, b_ada):
    bsz, d = c.shape
    n = w_ada.shape[1]
    tn = 512
    return pl.pallas_call(
        _mod_kernel,
        grid=(n // tn,),
        in_specs=[pl.BlockSpec((bsz, d), lambda j: (0, 0)),
                  pl.BlockSpec((d, tn), lambda j: (0, j)),
                  pl.BlockSpec((1, tn), lambda j: (0, j))],
        out_specs=pl.BlockSpec((bsz, tn), lambda j: (0, j)),
        out_shape=jax.ShapeDtypeStruct((bsz, n), F32),
        compiler_params=pltpu.CompilerParams(dimension_semantics=("arbitrary",),
                                             vmem_limit_bytes=VMEM_LIMIT),
        name="mod",
    )(c, w_ada, b_ada.reshape(1, n))


def _proj_kernel(x_ref, mod_ref, g_ref, w_ref,
                 qkv_ref, za_ref, ba_ref, qb_ref, ckv_ref, zb_ref, iq_ref, ikw_ref):
    x = x_ref[0]
    ms = jnp.mean(x * x, axis=-1, keepdims=True)
    xn = x * lax.rsqrt(ms + EPS)
    shift = mod_ref[0, 0:1, :]
    scale = mod_ref[0, 1:2, :]
    h = (xn * g_ref[...]) * (1.0 + scale) + shift
    hb = h.astype(BF16)

    def mm(cols):
        return _dot(hb, w_ref[:, cols[0]:cols[1]])

    qkv_ref[0] = mm(C_QKV)
    za_ref[0] = mm(C_ZA)
    ba_ref[0] = mm(C_BA)
    qb_ref[0] = mm(C_QB).astype(BF16)
    ckv_ref[0] = mm(C_CKV)
    zb_ref[0] = mm(C_ZB)
    iq_ref[0] = mm(C_IQ).astype(BF16)
    ikw_ref[0] = mm(C_IKW)


def _projection(x, mod3, g_pre, w_pad, tm):
    bsz, seq, d = x.shape
    widths = [(C_QKV, F32), (C_ZA, F32), (C_BA, F32), (C_QB, BF16),
              (C_CKV, F32), (C_ZB, F32), (C_IQ, BF16), (C_IKW, F32)]
    out_shape = [jax.ShapeDtypeStruct((bsz, seq, c[1] - c[0]), dt) for c, dt in widths]
    out_specs = [pl.BlockSpec((1, tm, c[1] - c[0]), lambda b, i: (b, i, 0)) for c, _ in widths]
    return pl.pallas_call(
        _proj_kernel,
        grid=(bsz, seq // tm),
        in_specs=[pl.BlockSpec((1, tm, d), lambda b, i: (b, i, 0)),
                  pl.BlockSpec((1, 3, d), lambda b, i: (b, 0, 0)),
                  pl.BlockSpec((1, d), lambda b, i: (0, 0)),
                  pl.BlockSpec((d, D_IN_PAD), lambda b, i: (0, 0))],
        out_specs=out_specs,
        out_shape=out_shape,
        compiler_params=pltpu.CompilerParams(dimension_semantics=("arbitrary", "arbitrary"),
                                             vmem_limit_bytes=VMEM_LIMIT),
        name="proj",
    )(x, mod3, g_pre.reshape(1, d), w_pad)


def _tri_inverse_many(lmats, level_masks, eye):
    ts = [eye - jnp.where(level_masks[0], lm, 0.0) for lm in lmats]
    for m in level_masks[1:]:
        xs = [_bdot(jnp.where(m, lm, 0.0), t) for lm, t in zip(lmats, ts)]
        ts = [t - _bdot(t, x) for t, x in zip(ts, xs)]
    return ts


def _gdn_kernel(qkv_ref, za_ref, ba_ref, cw_ref, alog_ref, dtb_ref, gn_ref, o_ref,
                halo_ref, s_ref, gc_ref, beta_ref, gct_ref, u_ref, wq_ref, kg_ref, at_ref,
                *, rb, tl, group_a):
    n_ch = tl // CHUNK
    assert group_a % n_ch == 0
    li = pl.program_id(1)

    @pl.when(li == 0)
    def _():
        halo_ref[...] = jnp.zeros_like(halo_ref)
        s_ref[...] = jnp.zeros_like(s_ref)

    r_i = lax.broadcasted_iota(I32, (tl, tl), 0)
    c_i = lax.broadcasted_iota(I32, (tl, tl), 1)
    shift = CHUNK.bit_length() - 1
    tri = jnp.where((c_i <= r_i) & ((r_i >> shift) == (c_i >> shift)), 1.0, 0.0)
    for r in range(rb):
        ba = ba_ref[r]
        beta_ref[r] = _sigmoid(ba)
        g = -jnp.exp(alog_ref[...]) * _softplus(ba + dtb_ref[...])
        gc = _dot(tri, g, HIGHEST)
        gc_ref[r] = gc
        gct = gc.T
        for c in range(n_ch):
            gct_ref[r * n_ch + c] = gct[:, c * CHUNK:(c + 1) * CHUNK]

    row = lax.broadcasted_iota(I32, (CHUNK, CHUNK), 0)
    col = lax.broadcasted_iota(I32, (CHUNK, CHUNK), 1)
    causal = row >= col
    strict = row > col
    eye = jnp.where(row == col, 1.0, 0.0)
    level_masks = []
    s = 1
    while s < CHUNK:
        ls = s.bit_length() - 1
        level_masks.append(((row >> (ls + 1)) == (col >> (ls + 1)))
                           & (((row >> ls) & 1) == 1) & (((col >> ls) & 1) == 0))
        s *= 2

    def phase_a(ig, carry):
        probs = []
        for j in range(group_a):
            c = j % n_ch
            r = ig * (group_a // n_ch) + j // n_ch
            it = r * n_ch + c
            base = c * CHUNK
            gc_c = gc_ref[r, pl.ds(base, CHUNK), :]
            beta_c = beta_ref[r, pl.ds(base, CHUNK), :]
            gct_c = gct_ref[it]
            for h in range(H_A):
                def conv_silu(sec):
                    c0 = sec * W_A + h * DK_A
                    if c == 0:
                        win = jnp.concatenate([halo_ref[r, :, c0:c0 + DK_A],
                                               qkv_ref[r, 0:CHUNK, c0:c0 + DK_A]], axis=0)
                    else:
                        win = qkv_ref[r, base - 8:base + CHUNK, c0:c0 + DK_A]
                    w0, w1, w2, w3 = (cw_ref[t:t + 1, c0:c0 + DK_A] for t in range(CONV_W))
                    prev = pltpu.roll(win, 1, axis=0)
                    near = win * w3 + prev * w2
                    far = pltpu.roll(win * w1 + prev * w0, 2, axis=0)
                    return _silu((near + far)[8:8 + CHUNK])

                q = conv_silu(0)
                k = conv_silu(1)
                v = conv_silu(2)
                q = q * lax.rsqrt(jnp.sum(q * q, axis=-1, keepdims=True) + EPS) * (DK_A ** -0.5)
                k = k * lax.rsqrt(jnp.sum(k * k, axis=-1, keepdims=True) + EPS)
                beta = beta_c[:, h:h + 1]
                gcol = gc_c[:, H_A + h:H_A + h + 1]
                grow = gct_c[H_A + h:H_A + h + 1, :]
                glast = gc_c[CHUNK - 1:CHUNK, H_A + h:H_A + h + 1]
                decay = jnp.exp(jnp.where(causal, gcol - grow, -jnp.inf))
                ecol = jnp.exp(gcol)
                kb = k * beta
                hc = slice(h * DV_A, (h + 1) * DV_A)
                wq_ref[it, h, CHUNK:2 * CHUNK, :] = (q * ecol).astype(BF16)
                kg_ref[r, pl.ds(base, CHUNK), hc] = (k * jnp.exp(glast - gcol)).astype(BF16)
                a2 = _bdot_nt(jnp.concatenate([kb, q], axis=0), k)
                at_ref[it, h] = (a2[CHUNK:] * decay).astype(BF16)
                probs.append(dict(
                    r=r, it=it, h=h, base=base, hc=hc,
                    lmat=jnp.where(strict, a2[:CHUNK] * decay, 0.0),
                    rhs=jnp.concatenate([v * beta, kb * ecol], axis=1).astype(BF16)))
        tmats = _tri_inverse_many([p["lmat"] for p in probs], level_masks, eye)
        for p, tmat in zip(probs, tmats):
            uw = _dot(tmat.astype(BF16), p["rhs"])
            u_ref[p["r"], pl.ds(p["base"], CHUNK), p["hc"]] = uw[:, :DV_A]
            wq_ref[p["it"], p["h"], 0:CHUNK, :] = uw[:, DV_A:].astype(BF16)
        return carry

    lax.fori_loop(0, rb * n_ch // group_a, phase_a, 0)
    halo_ref[...] = qkv_ref[:, tl - 8:tl, :]

    def phase_b(c, carry):
        base = pl.multiple_of(c * CHUNK, CHUNK)
        chains = [(r, h) for r in range(rb) for h in range(H_A)]
        hcs = [slice(h * DV_A, (h + 1) * DV_A) for _, h in chains]
        sts = [s_ref[r, h] for r, h in chains]
        wss = [_dot(wq_ref[r * n_ch + c, h], st.astype(BF16)) for (r, h), st in zip(chains, sts)]
        vnbs = [(u_ref[r, pl.ds(base, CHUNK), hc] - ws[:CHUNK]).astype(BF16)
                for (r, h), hc, ws in zip(chains, hcs, wss)]
        upds = [_dot_tn(kg_ref[r, pl.ds(base, CHUNK), hc], vnb)
                for (r, h), hc, vnb in zip(chains, hcs, vnbs)]
        for (r, h), st, upd in zip(chains, sts, upds):
            glast = gc_ref[r, pl.ds(base + CHUNK - 1, 1), :][:, H_A + h:H_A + h + 1]
            s_ref[r, h] = st * jnp.exp(glast) + upd
        for (r, h), hc, ws, vnb in zip(chains, hcs, wss, vnbs):
            o = ws[CHUNK:] + _dot(at_ref[r * n_ch + c, h], vnb)
            on = o * lax.rsqrt(jnp.mean(o * o, axis=-1, keepdims=True) + EPS) * gn_ref[...]
            z = za_ref[r, pl.ds(base, CHUNK), hc]
            o_ref[r, pl.ds(base, CHUNK), hc] = (on * _silu(z)).astype(BF16)
        return carry

    lax.fori_loop(0, n_ch, phase_b, 0)


def _gated_deltanet(qkv, za, ba, conv_w, alog_vec, dtb_vec, g_norm, rb, tl, group_a):
    bsz, seq, _ = qkv.shape
    n_ch = tl // CHUNK
    kern = functools.partial(_gdn_kernel, rb=rb, tl=tl, group_a=group_a)
    return pl.pallas_call(
        kern,
        grid=(bsz // rb, seq // tl),
        in_specs=[pl.BlockSpec((rb, tl, 3 * W_A), lambda b, i: (b, i, 0)),
                  pl.BlockSpec((rb, tl, W_A), lambda b, i: (b, i, 0)),
                  pl.BlockSpec((rb, tl, LANES), lambda b, i: (b, i, 0)),
                  pl.BlockSpec((CONV_W, 3 * W_A), lambda b, i: (0, 0)),
                  pl.BlockSpec((1, LANES), lambda b, i: (0, 0)),
                  pl.BlockSpec((1, LANES), lambda b, i: (0, 0)),
                  pl.BlockSpec((1, DV_A), lambda b, i: (0, 0))],
        out_specs=pl.BlockSpec((rb, tl, W_A), lambda b, i: (b, i, 0)),
        out_shape=jax.ShapeDtypeStruct((bsz, seq, W_A), BF16),
        scratch_shapes=[pltpu.VMEM((rb, 8, 3 * W_A), F32),
                        pltpu.VMEM((rb, H_A, DK_A, DV_A), F32),
                        pltpu.VMEM((rb, tl, LANES), F32),
                        pltpu.VMEM((rb, tl, LANES), F32),
                        pltpu.VMEM((rb * n_ch, LANES, CHUNK), F32),
                        pltpu.VMEM((rb, tl, W_A), F32),
                        pltpu.VMEM((rb * n_ch, H_A, 2 * CHUNK, DK_A), BF16),
                        pltpu.VMEM((rb, tl, W_A), BF16),
                        pltpu.VMEM((rb * n_ch, H_A, CHUNK, CHUNK), BF16)],
        compiler_params=pltpu.CompilerParams(dimension_semantics=("arbitrary", "arbitrary"),
                                             vmem_limit_bytes=VMEM_LIMIT),
        name="gdn",
    )(qkv, za, ba, conv_w, alog_vec, dtb_vec, g_norm.reshape(1, DV_A))


def _bias_kernel(rb_ref, o_ref):
    d = pl.program_id(0)
    kj = lax.broadcasted_iota(I32, (TQ, TQ), 0)
    qi = lax.broadcasted_iota(I32, (TQ, TQ), 1)
    dist = d * TQ + qi - kj
    n = jnp.maximum(dist, 0)
    nf = jnp.maximum(n, 1).astype(F32)
    large = MAX_EXACT + (jnp.log(nf / MAX_EXACT) / math.log(MAX_DIST / MAX_EXACT)
                         * (N_BUCKETS - MAX_EXACT)).astype(I32)
    large = jnp.minimum(large, N_BUCKETS - 1)
    bucket = jnp.where(n < MAX_EXACT, n, large)
    for h in range(H_B):
        acc = jnp.zeros((TQ, TQ), F32)
        for kb in range(N_BUCKETS):
            acc = jnp.where(bucket == kb, rb_ref[kb, h], acc)
        o_ref[0, :, h * TQ:(h + 1) * TQ] = acc * LOG2E


def _bias_tiles(rel_bias, n_diag):
    return pl.pallas_call(
        _bias_kernel,
        grid=(n_diag,),
        in_specs=[pl.BlockSpec(memory_space=pltpu.SMEM)],
        out_specs=pl.BlockSpec((1, TQ, H_B * TQ), lambda d: (d, 0, 0)),
        out_shape=jax.ShapeDtypeStruct((n_diag, TQ, H_B * TQ), F32),
        compiler_params=pltpu.CompilerParams(dimension_semantics=("arbitrary",)),
        name="bias",
    )(rel_bias)


def _fold_keys(x, op):
    x = op(x.reshape(4, x.shape[0] // 32, 8, x.shape[-1]), axis=1)
    return op(x, axis=0)


def _dsa_kernel(iq_ref, ikwq_ref, qb_ref, zb_ref, ikw_ref, ckv_ref, gkv_ref, wuv_ref, bias_ref,
                o_ref,
                kvn_ref, kvt_ref, iklo_ref, ikhi_ref, key_ref, w_ref, am_ref, lg_ref, acc_ref,
                *, rb, seq, k_top):
    qi = pl.program_id(1)
    n_kc = qi + 1
    hq = H_B * TQ
    rows = range(rb)

    @pl.when(qi == 0)
    def _():
        for r in rows:
            for c in range(seq // TQ):
                sl = slice(c * TQ, (c + 1) * TQ)
                ckv = ckv_ref[r, sl, :]
                ms = jnp.mean(ckv * ckv, axis=-1, keepdims=True)
                kvn = (ckv * lax.rsqrt(ms + EPS)) * gkv_ref[...]
                kvn_ref[r, sl, :] = kvn.astype(BF16)
                kvt_ref[r, c] = kvn.T.astype(BF16)
                ikw = ikw_ref[r, sl, :]
                lane = lax.broadcasted_iota(I32, ikw.shape, 1)
                lo = jnp.where(lane < D_IDX, ikw, 0.0)
                iklo_ref[r, sl, :] = lo.astype(BF16)
                ikhi_ref[r, sl, :] = pltpu.roll(lo, D_IDX, axis=1).astype(BF16)

    iq4s, iwts = [], []
    for r in rows:
        iq = iq_ref[r]
        iq4s.append(jnp.concatenate(
            [iq[:, p * LANES:(p + 1) * LANES] for p in range(H_IDX // 2)], axis=0))
        iwts.append((ikwq_ref[r] * (H_IDX ** -0.5 * D_IDX ** -0.5)).T)

    key_j = lax.broadcasted_iota(I32, (TQ, TQ), 0)
    qry_t = qi * TQ + lax.broadcasted_iota(I32, (TQ, TQ), 1)

    def causal_mask(c):
        return (c * TQ + key_j) <= qry_t

    q4s = []
    for r in rows:
        qb = qb_ref[r]
        q4s.append(jnp.concatenate([qb[:, h * R_KV:(h + 1) * R_KV] for h in range(H_B)], axis=0))
    scale = R_KV ** -0.5 * LOG2E

    def pair_loop(body, carry):
        carry = lax.fori_loop(0, n_kc >> 1, lambda i, cr: body((2 * i, 2 * i + 1), cr), carry)
        return lax.cond((n_kc & 1) == 1, lambda cr: body((n_kc - 1,), cr), lambda cr: cr, carry)

    def quad_loop(body, carry):
        n4 = n_kc >> 2
        carry = lax.fori_loop(0, n4, lambda i, cr: body(tuple(4 * i + t for t in range(4)), cr),
                              carry)
        b2 = n4 << 2
        carry = lax.cond((n_kc & 2) != 0, lambda cr: body((b2, b2 + 1), cr), lambda cr: cr, carry)
        b1 = b2 + (n_kc & 2)
        return lax.cond((n_kc & 1) != 0, lambda cr: body((b1,), cr), lambda cr: cr, carry)

    half = TQ // 2
    guards = jnp.int32(-(2 ** 31) + 2 ** 15)
    ones2 = jnp.int32(2 ** 16 + 1)
    fmax = 2 ** FIELD - 1

    def pack_fields(f):
        return (f[half:] << 16) | f[:half] | guards

    def field1(key):
        return lax.shift_right_logical(key, 32 - FIELD) ^ (1 << (FIELD - 1))

    def field2(key):
        return lax.shift_right_logical(key, 32 - 2 * FIELD) & fmax

    def score_chunks(cs, m0s):
        m0s = list(m0s)
        nc = len(cs)
        ks = pl.ds(pl.multiple_of(cs[0] * TQ, TQ), nc * TQ)
        rels = [_dot_nt(jnp.concatenate([iklo_ref[r, ks, :], ikhi_ref[r, ks, :]], axis=0),
                        iq4s[r]) for r in rows]
        sts = [_dot_nt(kvn_ref[r, ks, :], q4s[r]) for r in rows]
        for j, c in enumerate(cs):
            for r in rows:
                re = rels[r][j * TQ:(j + 1) * TQ]
                ro = rels[r][(nc + j) * TQ:(nc + j + 1) * TQ]
                s = None
                for p in range(H_IDX // 2):
                    ps = slice(p * TQ, (p + 1) * TQ)
                    we = iwts[r][IW_LANE + 2 * p:IW_LANE + 2 * p + 1, :]
                    wo = iwts[r][IW_LANE + 2 * p + 1:IW_LANE + 2 * p + 2, :]
                    t = jnp.maximum(re[:, ps], 0.0) * we + jnp.maximum(ro[:, ps], 0.0) * wo
                    s = t if s is None else s + t
                s = jnp.where(s == 0.0, 0.0, s)
                s = jnp.where(causal_mask(c), s, -jnp.inf)
                bits = pltpu.bitcast(s, I32)
                key = jnp.where(bits < 0, bits ^ 0x7FFFFFFF, bits)
                key_ref[r, c] = key
                w_ref[r, c] = pack_fields(field1(key))
                lg = sts[r][j * TQ:(j + 1) * TQ] * scale + bias_ref[qi - c]
                lg_ref[r, c] = lg
                m0s[r] = jnp.maximum(m0s[r], _fold_keys(lg, jnp.max))
        return tuple(m0s)

    m0s = quad_loop(score_chunks, tuple(jnp.full((8, hq), -jnp.inf, F32) for _ in rows))

    kf = float(k_top)
    n_pairs = (n_kc + 1) >> 1
    sign = jnp.int32(-2 ** 31)

    @pl.when((n_kc & 1) == 1)
    def _():
        for r in rows:
            key_ref[r, n_kc] = jnp.full((TQ, TQ), -2 ** 31, I32)

    def count_ge(cands):
        def body(i, accs):
            out = list(accs)
            for c in (2 * i, 2 * i + 1):
                for r in rows:
                    hit = jnp.where(key_ref[r, c] >= cands[r], 1.0, 0.0)
                    out[r] = out[r] + _fold_keys(hit, jnp.sum)
            return tuple(out)
        accs = lax.fori_loop(0, n_pairs, body, tuple(jnp.zeros((8, TQ), F32) for _ in rows))
        return [jnp.sum(a, axis=0, keepdims=True) for a in accs]

    def count_fields(cands):
        pairs = [(cu << 16) | cu for cu in cands]
        def body(i, accs):
            out = list(accs)
            for c in (2 * i, 2 * i + 1):
                for r in rows:
                    d = w_ref[r, c] - pairs[r]
                    hit = lax.shift_right_logical(d, 15) & ones2
                    out[r] = out[r] + _fold_keys(hit, jnp.sum)
            return tuple(out)
        accs = lax.fori_loop(0, n_pairs, body, tuple(jnp.zeros((8, TQ), I32) for _ in rows))
        both = [(a & 0xFFFF) + lax.shift_right_logical(a, 16) for a in accs]
        return [jnp.sum(b, axis=0, keepdims=True).astype(F32) for b in both]

    def field_level(bases, cges, want_above):
        def bit_body(i, carry):
            ts, cg = carry
            bit = lax.shift_left(jnp.int32(1), FIELD - 1 - i)
            cands = [t | bit for t in ts]
            cnts = [b + n for b, n in zip(bases, count_fields([cu[0:1, :] for cu in cands]))]
            take = [cnt >= kf for cnt in cnts]
            return (tuple(jnp.where(tk, cu, t) for tk, cu, t in zip(take, cands, ts)),
                    tuple(jnp.where(tk, cnt, g) for tk, cnt, g in zip(take, cnts, cg)))
        ts, cges = lax.fori_loop(0, FIELD, bit_body,
                                 (tuple(jnp.zeros((8, TQ), I32) for _ in rows), tuple(cges)))
        ps = [t[0:1, :] for t in ts]
        if not want_above:
            return ps, None, cges
        above = count_fields([jnp.minimum(p + 1, fmax) for p in ps])
        above = [b + jnp.where(p == fmax, 0.0, n) for b, p, n in zip(bases, ps, above)]
        return ps, above, cges

    @pl.when((n_kc & 1) == 1)
    def _():
        for r in rows:
            w_ref[r, n_kc] = jnp.full((half, TQ), guards, I32)

    cges = [jnp.full((1, TQ), kf, F32) for _ in rows]
    p1s, above1, cges = field_level([0.0 for _ in rows], cges, True)

    def build_level2(c, carry):
        for r in rows:
            key = key_ref[r, c]
            w_ref[r, c] = pack_fields(jnp.where(field1(key) == p1s[r], field2(key), 0))
        return carry

    lax.fori_loop(0, n_kc, build_level2, 0)
    p2s, _, cges = field_level(above1, cges, False)

    def bit_body(i, carry):
        tus, cges = carry
        bit = lax.shift_left(jnp.int32(1), 31 - i)
        cands = [tu | bit for tu in tus]
        cnts = count_ge([(cu ^ sign)[0:1, :] for cu in cands])
        take = [cnt >= kf for cnt in cnts]
        return (tuple(jnp.where(t, cu, tu) for t, cu, tu in zip(take, cands, tus)),
                tuple(jnp.where(t, cnt, cge) for t, cnt, cge in zip(take, cnts, cges)))

    tus = tuple(jnp.broadcast_to((p1 << (32 - FIELD)) | (p2 << (32 - 2 * FIELD)), (8, TQ))
                for p1, p2 in zip(p1s, p2s))
    tus, cges = lax.fori_loop(2 * FIELD, 32, bit_body, (tus, tuple(cges)))
    thrs = [(tu ^ sign)[0:1, :] for tu in tus]
    qrow = qi * TQ + lax.broadcasted_iota(I32, (1, TQ), 1)
    simple_all = None
    for r in rows:
        simple = (cges[r] == kf) | ((thrs[r] == NEG_INF_KEY) & (qrow < k_top))
        simple_all = simple if simple_all is None else (simple_all & simple)
    all_simple = jnp.min(jnp.where(simple_all, 1.0, 0.0)) > 0.5

    @pl.when(all_simple)
    def _():
        def body(c, carry):
            cm = causal_mask(c)
            for r in rows:
                sel = (key_ref[r, c] >= thrs[r]) & cm
                am_ref[r, c] = jnp.where(sel, 0.0, -jnp.inf)
            return carry
        lax.fori_loop(0, n_kc, body, 0)

    @pl.when(jnp.logical_not(all_simple))
    def _():
        lower = jnp.where(lax.broadcasted_iota(I32, (TQ, TQ), 1)
                          <= lax.broadcasted_iota(I32, (TQ, TQ), 0), 1.0, 0.0).astype(BF16)

        def count_gt(c, cgs):
            return tuple(cg + _fold_keys(jnp.where(key_ref[r, c] > thrs[r], 1.0, 0.0), jnp.sum)
                         for r, cg in zip(rows, cgs))

        cgs = lax.fori_loop(0, n_kc, count_gt, tuple(jnp.zeros((8, TQ), F32) for _ in rows))
        needs = [kf - jnp.sum(cg, axis=0, keepdims=True) for cg in cgs]

        def body(c, seens):
            cm = causal_mask(c)
            out = []
            for r in rows:
                key = key_ref[r, c]
                eq = key == thrs[r]
                eqf = jnp.where(eq, 1.0, 0.0)
                rank = seens[r] + _dot(lower, eqf.astype(BF16))
                sel = (key > thrs[r]) | (eq & (rank <= needs[r]))
                am_ref[r, c] = jnp.where(sel & cm, 0.0, -jnp.inf)
                out.append(seens[r] + jnp.sum(eqf, axis=0, keepdims=True))
            return tuple(out)
        lax.fori_loop(0, n_kc, body, tuple(jnp.zeros((1, TQ), F32) for _ in rows))

    def masked_logits(r, c):
        return lg_ref[r, c] + jnp.concatenate([am_ref[r, c]] * H_B, axis=1)

    def softmax_pv(shifts):
        acc_ref[...] = jnp.zeros_like(acc_ref)

        def body(cs, ls):
            out = list(ls)
            for r in rows:
                ps = [jnp.exp2(masked_logits(r, c) - shifts[r]) for c in cs]
                kvt = jnp.concatenate([kvt_ref[r, c] for c in cs], axis=1)
                acc_ref[r] += _dot(kvt, jnp.concatenate(ps, axis=0).astype(BF16))
                for p in ps:
                    out[r] = out[r] + _fold_keys(p, jnp.sum)
            return tuple(out)

        ls = quad_loop(body, tuple(jnp.zeros((8, hq), F32) for _ in rows))
        return tuple(jnp.sum(l, axis=0, keepdims=True) for l in ls)

    lsums = softmax_pv([jnp.max(m, axis=0, keepdims=True) for m in m0s])
    lmin = jnp.min(jnp.concatenate(lsums, axis=1))

    def exact_shift(_):
        def body(cs, ms):
            out = list(ms)
            for c in cs:
                for r in rows:
                    out[r] = jnp.maximum(out[r], _fold_keys(masked_logits(r, c), jnp.max))
            return tuple(out)
        ms = pair_loop(body, tuple(jnp.full((8, hq), -jnp.inf, F32) for _ in rows))
        return softmax_pv([jnp.max(m, axis=0, keepdims=True) for m in ms])

    lsums = lax.cond(lmin >= 2.0 ** -40, lambda _: lsums, exact_shift, 0)
    for r in rows:
        ot = (acc_ref[r] / lsums[r]).astype(BF16)
        for h in range(H_B):
            hs = slice(h * DV_B, (h + 1) * DV_B)
            y = _dot_tn(ot[:, h * TQ:(h + 1) * TQ], wuv_ref[h])
            o_ref[r, :, hs] = (y * _silu(zb_ref[r, :, hs])).astype(BF16)


def _dsa_attention(iq, ikw, qb, zb, ckv, g_kv, w_uv_bf16, bias_tiles, k_top, rb):
    bsz, seq, _ = iq.shape
    n_q = seq // TQ
    hq = H_B * TQ
    kern = functools.partial(_dsa_kernel, rb=rb, seq=seq, k_top=k_top)
    return pl.pallas_call(
        kern,
        grid=(bsz // rb, n_q),
        in_specs=[pl.BlockSpec((rb, TQ, H_IDX * D_IDX), lambda b, i: (b, i, 0)),
                  pl.BlockSpec((rb, TQ, LANES), lambda b, i: (b, i, 0)),
                  pl.BlockSpec((rb, TQ, H_B * R_KV), lambda b, i: (b, i, 0)),
                  pl.BlockSpec((rb, TQ, W_B), lambda b, i: (b, i, 0)),
                  pl.BlockSpec((rb, seq, LANES), lambda b, i: (b, 0, 0),
                               pipeline_mode=pl.Buffered(1)),
                  pl.BlockSpec((rb, seq, R_KV), lambda b, i: (b, 0, 0),
                               pipeline_mode=pl.Buffered(1)),
                  pl.BlockSpec((1, R_KV), lambda b, i: (0, 0)),
                  pl.BlockSpec((H_B, R_KV, DV_B), lambda b, i: (0, 0, 0)),
                  pl.BlockSpec((n_q, TQ, hq), lambda b, i: (0, 0, 0),
                               pipeline_mode=pl.Buffered(1))],
        out_specs=pl.BlockSpec((rb, TQ, W_B), lambda b, i: (b, i, 0)),
        out_shape=jax.ShapeDtypeStruct((bsz, seq, W_B), BF16),
        scratch_shapes=[pltpu.VMEM((rb, seq, R_KV), BF16),
                        pltpu.VMEM((rb, n_q, R_KV, TQ), BF16),
                        pltpu.VMEM((rb, seq, LANES), BF16),
                        pltpu.VMEM((rb, seq, LANES), BF16),
                        pltpu.VMEM((rb, n_q + 1, TQ, TQ), I32),
                        pltpu.VMEM((rb, n_q + 1, TQ // 2, TQ), I32),
                        pltpu.VMEM((rb, n_q, TQ, TQ), F32),
                        pltpu.VMEM((rb, n_q, TQ, hq), F32),
                        pltpu.VMEM((rb, R_KV, hq), F32)],
        compiler_params=pltpu.CompilerParams(dimension_semantics=("arbitrary", "arbitrary"),
                                             vmem_limit_bytes=VMEM_LIMIT),
        name="dsa",
    )(iq, ikw, qb, zb, ikw, ckv, g_kv.reshape(1, R_KV), w_uv_bf16, bias_tiles)


def _out_kernel(x_ref, oa_ref, ob_ref, mod_ref, g_ref, w_ref, o_ref):
    mix_in = jnp.concatenate([oa_ref[0], ob_ref[0]], axis=-1)
    mix = _dot(mix_in, w_ref[...])
    ms = jnp.mean(mix * mix, axis=-1, keepdims=True)
    normed = (mix * lax.rsqrt(ms + EPS)) * g_ref[...]
    o_ref[0] = x_ref[0] + mod_ref[0, 2:3, :] * normed


def _output(x, o_a, o_b, mod3, g_post, w_out_bf16, tm):
    bsz, seq, d = x.shape
    return pl.pallas_call(
        _out_kernel,
        grid=(bsz, seq // tm),
        in_specs=[pl.BlockSpec((1, tm, d), lambda b, i: (b, i, 0)),
                  pl.BlockSpec((1, tm, W_A), lambda b, i: (b, i, 0)),
                  pl.BlockSpec((1, tm, W_B), lambda b, i: (b, i, 0)),
                  pl.BlockSpec((1, 3, d), lambda b, i: (b, 0, 0)),
                  pl.BlockSpec((1, d), lambda b, i: (0, 0)),
                  pl.BlockSpec((W_A + W_B, d), lambda b, i: (0, 0))],
        out_specs=pl.BlockSpec((1, tm, d), lambda b, i: (b, i, 0)),
        out_shape=jax.ShapeDtypeStruct((bsz, seq, d), F32),
        compiler_params=pltpu.CompilerParams(dimension_semantics=("arbitrary", "arbitrary"),
                                             vmem_limit_bytes=VMEM_LIMIT),
        name="outproj",
    )(x, o_a, o_b, mod3, g_post.reshape(1, d), w_out_bf16)


def _pad_lanes(w):
    return jnp.pad(w, ((0, 0), (0, LANES - w.shape[1])))


def _pad_input_projection(w_in):
    o_ba = 3 * W_A + W_A
    o_qb = o_ba + 2 * H_A
    o_ik = o_qb + H_B * R_KV + R_KV + W_B + H_IDX * D_IDX
    return jnp.concatenate([w_in[:, :o_ba], _pad_lanes(w_in[:, o_ba:o_qb]),
                            w_in[:, o_qb:o_ik], _pad_lanes(w_in[:, o_ik:])], axis=1)


def _head_lanes(v):
    return jnp.zeros((1, LANES), F32).at[0, H_A:2 * H_A].set(v.astype(F32))


def kernel(x, c, w_ada, b_ada, g_pre, w_in, conv_w, a_log, dt_bias, g_gdn, g_kv, w_uv, rel_bias, w_out, g_post):
    bsz, seq, d = x.shape
    depth = w_ada.shape[0]
    assert seq % TQ == 0 and seq % CHUNK == 0
    k_top = min(TOPK_MAX, seq // 4)
    tm = min(512, seq)
    tl = min(128, seq)
    rb = 8 if bsz % 8 == 0 else (2 if bsz % 2 == 0 else 1)
    bias_tiles = _bias_tiles(rel_bias, seq // TQ)
    for layer in range(depth):
        mod3 = _modulation(c, w_ada[layer], b_ada[layer]).reshape(bsz, 3, d)
        w_pad = _pad_input_projection(w_in[layer]).astype(BF16)
        qkv, za, ba, qb, ckv, zb, iq, ikw = _projection(x, mod3, g_pre[layer], w_pad, tm)
        o_a = _gated_deltanet(qkv, za, ba, conv_w[layer], _head_lanes(a_log[layer]),
                              _head_lanes(dt_bias[layer]), g_gdn[layer], rb, tl,
                              (tl // CHUNK) * math.gcd(max(8 // (tl // CHUNK), 1), rb))
        o_b = _dsa_attention(iq, ikw, qb, zb, ckv, g_kv[layer], w_uv[layer].astype(BF16),
                             bias_tiles, k_top, 4 if bsz % 4 == 0 else 1)
        x = _output(x, o_a, o_b, mod3, g_post[layer], w_out[layer].astype(BF16), tm)
    return x
```

```python
import functools
import math

import jax
import jax.numpy as jnp
from jax import lax
from jax.experimental import pallas as pl
from jax.experimental.pallas import tpu as pltpu

F32 = jnp.float32
BF16 = jnp.bfloat16
I32 = jnp.int32
HIGHEST = lax.Precision.HIGHEST

H_A, DK_A, DV_A = 4, 128, 128
W_A = H_A * DV_A
CONV_W = 4
CHUNK = 64
H_B, R_KV, DV_B = 4, 128, 128
W_B = H_B * DV_B
H_IDX, D_IDX = 8, 64
TOPK_MAX = 256
N_BUCKETS, MAX_EXACT, MAX_DIST = 32, 16, 128
EPS = 1e-6
LOG2E = math.log2(math.e)

LANES = 128
TQ = 128
FIELD = 15
V7X_VMEM_BYTES = 64 * 1024 * 1024
VMEM_LIMIT = V7X_VMEM_BYTES - 6 * 1024 * 1024

C_QKV = (0, 3 * W_A)
C_ZA = (C_QKV[1], C_QKV[1] + W_A)
C_BA = (C_ZA[1], C_ZA[1] + LANES)
C_QB = (C_BA[1], C_BA[1] + H_B * R_KV)
C_CKV = (C_QB[1], C_QB[1] + R_KV)
C_ZB = (C_CKV[1], C_CKV[1] + W_B)
C_IQ = (C_ZB[1], C_ZB[1] + H_IDX * D_IDX)
C_IKW = (C_IQ[1], C_IQ[1] + LANES)
D_IN_PAD = C_IKW[1]
IW_LANE = D_IDX

NEG_INF_KEY = -2139095041


def _sigmoid(x):
    return 1.0 / (1.0 + jnp.exp(-x))


def _silu(x):
    return x * _sigmoid(x)


def _softplus(x):
    return jnp.maximum(x, 0.0) + jnp.log1p(jnp.exp(-jnp.abs(x)))


def _dot(a, b, precision=None):
    return jnp.dot(a, b, precision=precision, preferred_element_type=F32)


def _dot_nt(a, b, precision=None):
    return lax.dot_general(a, b, (((1,), (1,)), ((), ())), precision=precision,
                           preferred_element_type=F32)


def _dot_tn(a, b, precision=None):
    return lax.dot_general(a, b, (((0,), (0,)), ((), ())), precision=precision,
                           preferred_element_type=F32)


def _bdot(a, b):
    return _dot(a.astype(BF16), b.astype(BF16))


def _bdot_nt(a, b):
    return _dot_nt(a.astype(BF16), b.astype(BF16))


def _mod_kernel(c_ref, w_ref, b_ref, o_ref):
    c = c_ref[...]
    o_ref[...] = _dot(_silu(c), w_ref[...], HIGHEST) + b_ref[...]


def _modulation(c, w_ada, b_ada):
    bsz, d = c.shape
    n = w_ada.shape[1]
    tn = 512
    return pl.pallas_call(
        _mod_kernel,
        grid=(n // tn,),
        in_specs=[pl.BlockSpec((bsz, d), lambda j: (0, 0)),
                  pl.BlockSpec((d, tn), lambda j: (0, j)),
                  pl.BlockSpec((1, tn), lambda j: (0, j))],
        out_specs=pl.BlockSpec((bsz, tn), lambda j: (0, j)),
        out_shape=jax.ShapeDtypeStruct((bsz, n), F32),
        compiler_params=pltpu.CompilerParams(dimension_semantics=("arbitrary",),
                                             vmem_limit_bytes=VMEM_LIMIT),
        name="mod",
    )(c, w_ada, b_ada.reshape(1, n))


def _proj_kernel(x_ref, mod_ref, g_ref, w_ref,
                 qkv_ref, za_ref, ba_ref, qb_ref, ckv_ref, zb_ref, iq_ref, ikw_ref):
    x = x_ref[0]
    ms = jnp.mean(x * x, axis=-1, keepdims=True)
    xn = x * lax.rsqrt(ms + EPS)
    shift = mod_ref[0, 0:1, :]
    scale = mod_ref[0, 1:2, :]
    h = (xn * g_ref[...]) * (1.0 + scale) + shift
    hb = h.astype(BF16)

    def mm(cols):
        return _dot(hb, w_ref[:, cols[0]:cols[1]])

    qkv_ref[0] = mm(C_QKV)
    za_ref[0] = mm(C_ZA)
    ba_ref[0] = mm(C_BA)
    qb_ref[0] = mm(C_QB).astype(BF16)
    ckv_ref[0] = mm(C_CKV)
    zb_ref[0] = mm(C_ZB)
    iq_ref[0] = mm(C_IQ).astype(BF16)
    ikw_ref[0] = mm(C_IKW)


def _projection(x, mod3, g_pre, w_pad, tm):
    bsz, seq, d = x.shape
    widths = [(C_QKV, F32), (C_ZA, F32), (C_BA, F32), (C_QB, BF16),
              (C_CKV, F32), (C_ZB, F32), (C_IQ, BF16), (C_IKW, F32)]
    out_shape = [jax.ShapeDtypeStruct((bsz, seq, c[1] - c[0]), dt) for c, dt in widths]
    out_specs = [pl.BlockSpec((1, tm, c[1] - c[0]), lambda b, i: (b, i, 0)) for c, _ in widths]
    return pl.pallas_call(
        _proj_kernel,
        grid=(bsz, seq // tm),
        in_specs=[pl.BlockSpec((1, tm, d), lambda b, i: (b, i, 0)),
                  pl.BlockSpec((1, 3, d), lambda b, i: (b, 0, 0)),
                  pl.BlockSpec((1, d), lambda b, i: (0, 0)),
                  pl.BlockSpec((d, D_IN_PAD), lambda b, i: (0, 0),
                               pipeline_mode=pl.Buffered(1))],
        out_specs=out_specs,
        out_shape=out_shape,
        compiler_params=pltpu.CompilerParams(dimension_semantics=("arbitrary", "arbitrary"),
                                             vmem_limit_bytes=VMEM_LIMIT),
        name="proj",
    )(x, mod3, g_pre.reshape(1, d), w_pad)


def _tri_inverse_many(lmats, level_masks, eye):
    ts = [eye - jnp.where(level_masks[0], lm, 0.0) for lm in lmats]
    for m in level_masks[1:]:
        xs = [_bdot(jnp.where(m, lm, 0.0), t) for lm, t in zip(lmats, ts)]
        ts = [t - _bdot(t, x) for t, x in zip(ts, xs)]
    return ts


def _gdn_kernel(qkv_ref, za_ref, ba_ref, cw_ref, alog_ref, dtb_ref, gn_ref, o_ref,
                halo_ref, s_ref, gc_ref, beta_ref, gct_ref, u_ref, wq_ref, kg_ref, at_ref,
                *, rb, tl, group_a):
    n_ch = tl // CHUNK
    assert group_a % n_ch == 0
    li = pl.program_id(1)

    @pl.when(li == 0)
    def _():
        halo_ref[...] = jnp.zeros_like(halo_ref)
        s_ref[...] = jnp.zeros_like(s_ref)

    r_i = lax.broadcasted_iota(I32, (tl, tl), 0)
    c_i = lax.broadcasted_iota(I32, (tl, tl), 1)
    shift = CHUNK.bit_length() - 1
    tri = jnp.where((c_i <= r_i) & ((r_i >> shift) == (c_i >> shift)), 1.0, 0.0)
    for r in range(rb):
        ba = ba_ref[r]
        beta_ref[r] = _sigmoid(ba)
        g = -jnp.exp(alog_ref[...]) * _softplus(ba + dtb_ref[...])
        gc = _dot(tri, g, HIGHEST)
        gc_ref[r] = gc
        gct = gc.T
        for c in range(n_ch):
            gct_ref[r * n_ch + c] = gct[:, c * CHUNK:(c + 1) * CHUNK]

    row = lax.broadcasted_iota(I32, (CHUNK, CHUNK), 0)
    col = lax.broadcasted_iota(I32, (CHUNK, CHUNK), 1)
    causal = row >= col
    strict = row > col
    eye = jnp.where(row == col, 1.0, 0.0)
    level_masks = []
    s = 1
    while s < CHUNK:
        ls = s.bit_length() - 1
        level_masks.append(((row >> (ls + 1)) == (col >> (ls + 1)))
                           & (((row >> ls) & 1) == 1) & (((col >> ls) & 1) == 0))
        s *= 2

    def phase_a(ig, carry):
        probs = []
        for j in range(group_a):
            c = j % n_ch
            r = ig * (group_a // n_ch) + j // n_ch
            it = r * n_ch + c
            base = c * CHUNK
            gc_c = gc_ref[r, pl.ds(base, CHUNK), :]
            beta_c = beta_ref[r, pl.ds(base, CHUNK), :]
            gct_c = gct_ref[it]
            for h in range(H_A):
                def conv_silu(sec):
                    c0 = sec * W_A + h * DK_A
                    if c == 0:
                        win = jnp.concatenate([halo_ref[r, :, c0:c0 + DK_A],
                                               qkv_ref[r, 0:CHUNK, c0:c0 + DK_A]], axis=0)
                    else:
                        win = qkv_ref[r, base - 8:base + CHUNK, c0:c0 + DK_A]
                    w0, w1, w2, w3 = (cw_ref[t:t + 1, c0:c0 + DK_A] for t in range(CONV_W))
                    prev = pltpu.roll(win, 1, axis=0)
                    near = win * w3 + prev * w2
                    far = pltpu.roll(win * w1 + prev * w0, 2, axis=0)
                    return _silu((near + far)[8:8 + CHUNK])

                q = conv_silu(0)
                k = conv_silu(1)
                v = conv_silu(2)
                q = q * lax.rsqrt(jnp.sum(q * q, axis=-1, keepdims=True) + EPS) * (DK_A ** -0.5)
                k = k * lax.rsqrt(jnp.sum(k * k, axis=-1, keepdims=True) + EPS)
                beta = beta_c[:, h:h + 1]
                gcol = gc_c[:, H_A + h:H_A + h + 1]
                grow = gct_c[H_A + h:H_A + h + 1, :]
                glast = gc_c[CHUNK - 1:CHUNK, H_A + h:H_A + h + 1]
                decay = jnp.exp(jnp.where(causal, gcol - grow, -jnp.inf))
                ecol = jnp.exp(gcol)
                kb = k * beta
                hc = slice(h * DV_A, (h + 1) * DV_A)
                wq_ref[it, h, CHUNK:2 * CHUNK, :] = (q * ecol).astype(BF16)
                kg_ref[r, pl.ds(base, CHUNK), hc] = (k * jnp.exp(glast - gcol)).astype(BF16)
                a2 = _bdot_nt(jnp.concatenate([kb, q], axis=0), k)
                at_ref[it, h] = (a2[CHUNK:] * decay).astype(BF16)
                probs.append(dict(
                    r=r, it=it, h=h, base=base, hc=hc,
                    lmat=jnp.where(strict, a2[:CHUNK] * decay, 0.0),
                    rhs=jnp.concatenate([v * beta, kb * ecol], axis=1).astype(BF16)))
        tmats = _tri_inverse_many([p["lmat"] for p in probs], level_masks, eye)
        for p, tmat in zip(probs, tmats):
            uw = _dot(tmat.astype(BF16), p["rhs"])
            u_ref[p["r"], pl.ds(p["base"], CHUNK), p["hc"]] = uw[:, :DV_A]
            wq_ref[p["it"], p["h"], 0:CHUNK, :] = uw[:, DV_A:].astype(BF16)
        return carry

    lax.fori_loop(0, rb * n_ch // group_a, phase_a, 0)
    halo_ref[...] = qkv_ref[:, tl - 8:tl, :]

    def phase_b(c, carry):
        base = pl.multiple_of(c * CHUNK, CHUNK)
        chains = [(r, h) for r in range(rb) for h in range(H_A)]
        hcs = [slice(h * DV_A, (h + 1) * DV_A) for _, h in chains]
        sts = [s_ref[r, h] for r, h in chains]
        wss = [_dot(wq_ref[r * n_ch + c, h], st.astype(BF16)) for (r, h), st in zip(chains, sts)]
        vnbs = [(u_ref[r, pl.ds(base, CHUNK), hc] - ws[:CHUNK]).astype(BF16)
                for (r, h), hc, ws in zip(chains, hcs, wss)]
        upds = [_dot_tn(kg_ref[r, pl.ds(base, CHUNK), hc], vnb)
                for (r, h), hc, vnb in zip(chains, hcs, vnbs)]
        for (r, h), st, upd in zip(chains, sts, upds):
            glast = gc_ref[r, pl.ds(base + CHUNK - 1, 1), :][:, H_A + h:H_A + h + 1]
            s_ref[r, h] = st * jnp.exp(glast) + upd
        for (r, h), hc, ws, vnb in zip(chains, hcs, wss, vnbs):
            o = ws[CHUNK:] + _dot(at_ref[r * n_ch + c, h], vnb)
            on = o * lax.rsqrt(jnp.mean(o * o, axis=-1, keepdims=True) + EPS) * gn_ref[...]
            z = za_ref[r, pl.ds(base, CHUNK), hc]
            o_ref[r, pl.ds(base, CHUNK), hc] = (on * _silu(z)).astype(BF16)
        return carry

    lax.fori_loop(0, n_ch, phase_b, 0, unroll=True)


def _gated_deltanet(qkv, za, ba, conv_w, alog_vec, dtb_vec, g_norm, rb, tl, group_a):
    bsz, seq, _ = qkv.shape
    n_ch = tl // CHUNK
    kern = functools.partial(_gdn_kernel, rb=rb, tl=tl, group_a=group_a)
    return pl.pallas_call(
        kern,
        grid=(bsz // rb, seq // tl),
        in_specs=[pl.BlockSpec((rb, tl, 3 * W_A), lambda b, i: (b, i, 0)),
                  pl.BlockSpec((rb, tl, W_A), lambda b, i: (b, i, 0)),
                  pl.BlockSpec((rb, tl, LANES), lambda b, i: (b, i, 0)),
                  pl.BlockSpec((CONV_W, 3 * W_A), lambda b, i: (0, 0)),
                  pl.BlockSpec((1, LANES), lambda b, i: (0, 0)),
                  pl.BlockSpec((1, LANES), lambda b, i: (0, 0)),
                  pl.BlockSpec((1, DV_A), lambda b, i: (0, 0))],
        out_specs=pl.BlockSpec((rb, tl, W_A), lambda b, i: (b, i, 0)),
        out_shape=jax.ShapeDtypeStruct((bsz, seq, W_A), BF16),
        scratch_shapes=[pltpu.VMEM((rb, 8, 3 * W_A), F32),
                        pltpu.VMEM((rb, H_A, DK_A, DV_A), F32),
                        pltpu.VMEM((rb, tl, LANES), F32),
                        pltpu.VMEM((rb, tl, LANES), F32),
                        pltpu.VMEM((rb * n_ch, LANES, CHUNK), F32),
                        pltpu.VMEM((rb, tl, W_A), F32),
                        pltpu.VMEM((rb * n_ch, H_A, 2 * CHUNK, DK_A), BF16),
                        pltpu.VMEM((rb, tl, W_A), BF16),
                        pltpu.VMEM((rb * n_ch, H_A, CHUNK, CHUNK), BF16)],
        compiler_params=pltpu.CompilerParams(dimension_semantics=("arbitrary", "arbitrary"),
                                             vmem_limit_bytes=VMEM_LIMIT),
        name="gdn",
    )(qkv, za, ba, conv_w, alog_vec, dtb_vec, g_norm.reshape(1, DV_A))


def _bias_kernel(rb_ref, o_ref):
    d = pl.program_id(0)
    kj = lax.broadcasted_iota(I32, (TQ, TQ), 0)
    qi = lax.broadcasted_iota(I32, (TQ, TQ), 1)
    dist = d * TQ + qi - kj
    n = jnp.maximum(dist, 0)
    nf = jnp.maximum(n, 1).astype(F32)
    large = MAX_EXACT + (jnp.log(nf / MAX_EXACT) / math.log(MAX_DIST / MAX_EXACT)
                         * (N_BUCKETS - MAX_EXACT)).astype(I32)
    large = jnp.minimum(large, N_BUCKETS - 1)
    bucket = jnp.where(n < MAX_EXACT, n, large)
    for h in range(H_B):
        acc = jnp.zeros((TQ, TQ), F32)
        for kb in range(N_BUCKETS):
            acc = jnp.where(bucket == kb, rb_ref[kb, h], acc)
        o_ref[0, :, h * TQ:(h + 1) * TQ] = acc * LOG2E


def _bias_tiles(rel_bias, n_diag):
    return pl.pallas_call(
        _bias_kernel,
        grid=(n_diag,),
        in_specs=[pl.BlockSpec(memory_space=pltpu.SMEM)],
        out_specs=pl.BlockSpec((1, TQ, H_B * TQ), lambda d: (d, 0, 0)),
        out_shape=jax.ShapeDtypeStruct((n_diag, TQ, H_B * TQ), F32),
        compiler_params=pltpu.CompilerParams(dimension_semantics=("arbitrary",)),
        name="bias",
    )(rel_bias)


def _fold_keys(x, op):
    x = op(x.reshape(4, x.shape[0] // 32, 8, x.shape[-1]), axis=1)
    return op(x, axis=0)


def _dsa_kernel(iq_ref, ikwq_ref, qb_ref, zb_ref, ikw_ref, ckv_ref, gkv_ref, wuv_ref, bias_ref,
                o_ref,
                kvn_ref, kvt_ref, iklo_ref, ikhi_ref, key_ref, w_ref, am_ref, lg_ref, acc_ref,
                *, rb, seq, k_top):
    qi = pl.program_id(1)
    n_kc = qi + 1
    hq = H_B * TQ
    rows = range(rb)

    @pl.when(qi == 0)
    def _():
        for r in rows:
            for c in range(seq // TQ):
                sl = slice(c * TQ, (c + 1) * TQ)
                ckv = ckv_ref[r, sl, :]
                ms = jnp.mean(ckv * ckv, axis=-1, keepdims=True)
                kvn = (ckv * lax.rsqrt(ms + EPS)) * gkv_ref[...]
                kvn_ref[r, sl, :] = kvn.astype(BF16)
                kvt_ref[r, c] = kvn.T.astype(BF16)
                ikw = ikw_ref[r, sl, :]
                lane = lax.broadcasted_iota(I32, ikw.shape, 1)
                lo = jnp.where(lane < D_IDX, ikw, 0.0)
                iklo_ref[r, sl, :] = lo.astype(BF16)
                ikhi_ref[r, sl, :] = pltpu.roll(lo, D_IDX, axis=1).astype(BF16)

    iq4s, iwts = [], []
    for r in rows:
        iq = iq_ref[r]
        iq4s.append(jnp.concatenate(
            [iq[:, p * LANES:(p + 1) * LANES] for p in range(H_IDX // 2)], axis=0))
        iwts.append((ikwq_ref[r] * (H_IDX ** -0.5 * D_IDX ** -0.5)).T)

    key_j = lax.broadcasted_iota(I32, (TQ, TQ), 0)
    qry_t = qi * TQ + lax.broadcasted_iota(I32, (TQ, TQ), 1)

    def causal_mask(c):
        return (c * TQ + key_j) <= qry_t

    q4s = []
    for r in rows:
        qb = qb_ref[r]
        q4s.append(jnp.concatenate([qb[:, h * R_KV:(h + 1) * R_KV] for h in range(H_B)], axis=0))
    scale = R_KV ** -0.5 * LOG2E

    def pair_loop(body, carry):
        carry = lax.fori_loop(0, n_kc >> 1, lambda i, cr: body((2 * i, 2 * i + 1), cr), carry)
        return lax.cond((n_kc & 1) == 1, lambda cr: body((n_kc - 1,), cr), lambda cr: cr, carry)

    def quad_loop(body, carry):
        n4 = n_kc >> 2
        carry = lax.fori_loop(0, n4, lambda i, cr: body(tuple(4 * i + t for t in range(4)), cr),
                              carry)
        b2 = n4 << 2
        carry = lax.cond((n_kc & 2) != 0, lambda cr: body((b2, b2 + 1), cr), lambda cr: cr, carry)
        b1 = b2 + (n_kc & 2)
        return lax.cond((n_kc & 1) != 0, lambda cr: body((b1,), cr), lambda cr: cr, carry)

    half = TQ // 2
    guards = jnp.int32(-(2 ** 31) + 2 ** 15)
    ones2 = jnp.int32(2 ** 16 + 1)
    fmax = 2 ** FIELD - 1

    def pack_fields(f):
        return (f[half:] << 16) | f[:half] | guards

    def field1(key):
        return lax.shift_right_logical(key, 32 - FIELD) ^ (1 << (FIELD - 1))

    def field2(key):
        return lax.shift_right_logical(key, 32 - 2 * FIELD) & fmax

    def score_chunks(cs, m0s):
        m0s = list(m0s)
        nc = len(cs)
        ks = pl.ds(pl.multiple_of(cs[0] * TQ, TQ), nc * TQ)
        rels = [_dot_nt(jnp.concatenate([iklo_ref[r, ks, :], ikhi_ref[r, ks, :]], axis=0),
                        iq4s[r]) for r in rows]
        sts = [_dot_nt(kvn_ref[r, ks, :], q4s[r]) for r in rows]
        for j, c in enumerate(cs):
            for r in rows:
                re = rels[r][j * TQ:(j + 1) * TQ]
                ro = rels[r][(nc + j) * TQ:(nc + j + 1) * TQ]
                s = None
                for p in range(H_IDX // 2):
                    ps = slice(p * TQ, (p + 1) * TQ)
                    we = iwts[r][IW_LANE + 2 * p:IW_LANE + 2 * p + 1, :]
                    wo = iwts[r][IW_LANE + 2 * p + 1:IW_LANE + 2 * p + 2, :]
                    t = jnp.maximum(re[:, ps], 0.0) * we + jnp.maximum(ro[:, ps], 0.0) * wo
                    s = t if s is None else s + t
                s = jnp.where(s == 0.0, 0.0, s)
                s = jnp.where(causal_mask(c), s, -jnp.inf)
                bits = pltpu.bitcast(s, I32)
                key = jnp.where(bits < 0, bits ^ 0x7FFFFFFF, bits)
                key_ref[r, c] = key
                w_ref[r, c] = pack_fields(field1(key))
                lg = sts[r][j * TQ:(j + 1) * TQ] * scale + bias_ref[qi - c]
                lg_ref[r, c] = lg
                m0s[r] = jnp.maximum(m0s[r], _fold_keys(lg, jnp.max))
        return tuple(m0s)

    m0s = quad_loop(score_chunks, tuple(jnp.full((8, hq), -jnp.inf, F32) for _ in rows))

    kf = float(k_top)
    n_pairs = (n_kc + 1) >> 1
    sign = jnp.int32(-2 ** 31)

    @pl.when((n_kc & 1) == 1)
    def _():
        for r in rows:
            key_ref[r, n_kc] = jnp.full((TQ, TQ), -2 ** 31, I32)

    def count_ge(cands):
        def body(i, accs):
            out = list(accs)
            for c in (2 * i, 2 * i + 1):
                for r in rows:
                    hit = jnp.where(key_ref[r, c] >= cands[r], 1.0, 0.0)
                    out[r] = out[r] + _fold_keys(hit, jnp.sum)
            return tuple(out)
        accs = lax.fori_loop(0, n_pairs, body, tuple(jnp.zeros((8, TQ), F32) for _ in rows))
        return [jnp.sum(a, axis=0, keepdims=True) for a in accs]

    def count_fields(cands):
        pairs = [(cu << 16) | cu for cu in cands]
        def body(i, accs):
            out = list(accs)
            for c in (2 * i, 2 * i + 1):
                for r in rows:
                    d = w_ref[r, c] - pairs[r]
                    hit = lax.shift_right_logical(d, 15) & ones2
                    out[r] = out[r] + _fold_keys(hit, jnp.sum)
            return tuple(out)
        accs = lax.fori_loop(0, n_pairs, body, tuple(jnp.zeros((8, TQ), I32) for _ in rows))
        both = [(a & 0xFFFF) + lax.shift_right_logical(a, 16) for a in accs]
        return [jnp.sum(b, axis=0, keepdims=True).astype(F32) for b in both]

    def field_level(bases, cges, want_above):
        def bit_body(i, carry):
            ts, cg = carry
            bit = lax.shift_left(jnp.int32(1), FIELD - 1 - i)
            cands = [t | bit for t in ts]
            cnts = [b + n for b, n in zip(bases, count_fields([cu[0:1, :] for cu in cands]))]
            take = [cnt >= kf for cnt in cnts]
            return (tuple(jnp.where(tk, cu, t) for tk, cu, t in zip(take, cands, ts)),
                    tuple(jnp.where(tk, cnt, g) for tk, cnt, g in zip(take, cnts, cg)))
        ts, cges = lax.fori_loop(0, FIELD, bit_body,
                                 (tuple(jnp.zeros((8, TQ), I32) for _ in rows), tuple(cges)))
        ps = [t[0:1, :] for t in ts]
        if not want_above:
            return ps, None, cges
        above = count_fields([jnp.minimum(p + 1, fmax) for p in ps])
        above = [b + jnp.where(p == fmax, 0.0, n) for b, p, n in zip(bases, ps, above)]
        return ps, above, cges

    @pl.when((n_kc & 1) == 1)
    def _():
        for r in rows:
            w_ref[r, n_kc] = jnp.full((half, TQ), guards, I32)

    cges = [jnp.full((1, TQ), kf, F32) for _ in rows]
    p1s, above1, cges = field_level([0.0 for _ in rows], cges, True)

    def build_level2(c, carry):
        for r in rows:
            key = key_ref[r, c]
            w_ref[r, c] = pack_fields(jnp.where(field1(key) == p1s[r], field2(key), 0))
        return carry

    lax.fori_loop(0, n_kc, build_level2, 0)
    p2s, _, cges = field_level(above1, cges, False)

    def bit_body(i, carry):
        tus, cges = carry
        bit = lax.shift_left(jnp.int32(1), 31 - i)
        cands = [tu | bit for tu in tus]
        cnts = count_ge([(cu ^ sign)[0:1, :] for cu in cands])
        take = [cnt >= kf for cnt in cnts]
        return (tuple(jnp.where(t, cu, tu) for t, cu, tu in zip(take, cands, tus)),
                tuple(jnp.where(t, cnt, cge) for t, cnt, cge in zip(take, cnts, cges)))

    tus = tuple(jnp.broadcast_to((p1 << (32 - FIELD)) | (p2 << (32 - 2 * FIELD)), (8, TQ))
                for p1, p2 in zip(p1s, p2s))
    tus, cges = lax.fori_loop(2 * FIELD, 32, bit_body, (tus, tuple(cges)))
    thrs = [(tu ^ sign)[0:1, :] for tu in tus]
    qrow = qi * TQ + lax.broadcasted_iota(I32, (1, TQ), 1)
    simple_all = None
    for r in rows:
        simple = (cges[r] == kf) | ((thrs[r] == NEG_INF_KEY) & (qrow < k_top))
        simple_all = simple if simple_all is None else (simple_all & simple)
    all_simple = jnp.min(jnp.where(simple_all, 1.0, 0.0)) > 0.5

    @pl.when(all_simple)
    def _():
        def body(c, carry):
            cm = causal_mask(c)
            for r in rows:
                sel = (key_ref[r, c] >= thrs[r]) & cm
                am_ref[r, c] = jnp.where(sel, 0.0, -jnp.inf)
            return carry
        lax.fori_loop(0, n_kc, body, 0)

    @pl.when(jnp.logical_not(all_simple))
    def _():
        lower = jnp.where(lax.broadcasted_iota(I32, (TQ, TQ), 1)
                          <= lax.broadcasted_iota(I32, (TQ, TQ), 0), 1.0, 0.0).astype(BF16)

        def count_gt(c, cgs):
            return tuple(cg + _fold_keys(jnp.where(key_ref[r, c] > thrs[r], 1.0, 0.0), jnp.sum)
                         for r, cg in zip(rows, cgs))

        cgs = lax.fori_loop(0, n_kc, count_gt, tuple(jnp.zeros((8, TQ), F32) for _ in rows))
        needs = [kf - jnp.sum(cg, axis=0, keepdims=True) for cg in cgs]

        def body(c, seens):
            cm = causal_mask(c)
            out = []
            for r in rows:
                key = key_ref[r, c]
                eq = key == thrs[r]
                eqf = jnp.where(eq, 1.0, 0.0)
                rank = seens[r] + _dot(lower, eqf.astype(BF16))
                sel = (key > thrs[r]) | (eq & (rank <= needs[r]))
                am_ref[r, c] = jnp.where(sel & cm, 0.0, -jnp.inf)
                out.append(seens[r] + jnp.sum(eqf, axis=0, keepdims=True))
            return tuple(out)
        lax.fori_loop(0, n_kc, body, tuple(jnp.zeros((1, TQ), F32) for _ in rows))

    def masked_logits(r, c):
        return lg_ref[r, c] + jnp.concatenate([am_ref[r, c]] * H_B, axis=1)

    def softmax_pv(shifts):
        acc_ref[...] = jnp.zeros_like(acc_ref)

        def body(cs, ls):
            out = list(ls)
            for r in rows:
                ps = [jnp.exp2(masked_logits(r, c) - shifts[r]) for c in cs]
                kvt = jnp.concatenate([kvt_ref[r, c] for c in cs], axis=1)
                acc_ref[r] += _dot(kvt, jnp.concatenate(ps, axis=0).astype(BF16))
                for p in ps:
                    out[r] = out[r] + _fold_keys(p, jnp.sum)
            return tuple(out)

        ls = quad_loop(body, tuple(jnp.zeros((8, hq), F32) for _ in rows))
        return tuple(jnp.sum(l, axis=0, keepdims=True) for l in ls)

    lsums = softmax_pv([jnp.max(m, axis=0, keepdims=True) for m in m0s])
    lmin = jnp.min(jnp.concatenate(lsums, axis=1))

    def exact_shift(_):
        def body(cs, ms):
            out = list(ms)
            for c in cs:
                for r in rows:
                    out[r] = jnp.maximum(out[r], _fold_keys(masked_logits(r, c), jnp.max))
            return tuple(out)
        ms = pair_loop(body, tuple(jnp.full((8, hq), -jnp.inf, F32) for _ in rows))
        return softmax_pv([jnp.max(m, axis=0, keepdims=True) for m in ms])

    lsums = lax.cond(lmin >= 2.0 ** -40, lambda _: lsums, exact_shift, 0)
    for r in rows:
        ot = (acc_ref[r] / lsums[r]).astype(BF16)
        for h in range(H_B):
            hs = slice(h * DV_B, (h + 1) * DV_B)
            y = _dot_tn(ot[:, h * TQ:(h + 1) * TQ], wuv_ref[h])
            o_ref[r, :, hs] = (y * _silu(zb_ref[r, :, hs])).astype(BF16)


def _dsa_attention(iq, ikw, qb, zb, ckv, g_kv, w_uv_bf16, bias_tiles, k_top, rb):
    bsz, seq, _ = iq.shape
    n_q = seq // TQ
    hq = H_B * TQ
    kern = functools.partial(_dsa_kernel, rb=rb, seq=seq, k_top=k_top)
    return pl.pallas_call(
        kern,
        grid=(bsz // rb, n_q),
        in_specs=[pl.BlockSpec((rb, TQ, H_IDX * D_IDX), lambda b, i: (b, i, 0)),
                  pl.BlockSpec((rb, TQ, LANES), lambda b, i: (b, i, 0)),
                  pl.BlockSpec((rb, TQ, H_B * R_KV), lambda b, i: (b, i, 0)),
                  pl.BlockSpec((rb, TQ, W_B), lambda b, i: (b, i, 0)),
                  pl.BlockSpec((rb, seq, LANES), lambda b, i: (b, 0, 0),
                               pipeline_mode=pl.Buffered(1)),
                  pl.BlockSpec((rb, seq, R_KV), lambda b, i: (b, 0, 0),
                               pipeline_mode=pl.Buffered(1)),
                  pl.BlockSpec((1, R_KV), lambda b, i: (0, 0)),
                  pl.BlockSpec((H_B, R_KV, DV_B), lambda b, i: (0, 0, 0)),
                  pl.BlockSpec((n_q, TQ, hq), lambda b, i: (0, 0, 0),
                               pipeline_mode=pl.Buffered(1))],
        out_specs=pl.BlockSpec((rb, TQ, W_B), lambda b, i: (b, i, 0)),
        out_shape=jax.ShapeDtypeStruct((bsz, seq, W_B), BF16),
        scratch_shapes=[pltpu.VMEM((rb, seq, R_KV), BF16),
                        pltpu.VMEM((rb, n_q, R_KV, TQ), BF16),
                        pltpu.VMEM((rb, seq, LANES), BF16),
                        pltpu.VMEM((rb, seq, LANES), BF16),
                        pltpu.VMEM((rb, n_q + 1, TQ, TQ), I32),
                        pltpu.VMEM((rb, n_q + 1, TQ // 2, TQ), I32),
                        pltpu.VMEM((rb, n_q, TQ, TQ), F32),
                        pltpu.VMEM((rb, n_q, TQ, hq), F32),
                        pltpu.VMEM((rb, R_KV, hq), F32)],
        compiler_params=pltpu.CompilerParams(dimension_semantics=("arbitrary", "arbitrary"),
                                             vmem_limit_bytes=VMEM_LIMIT),
        name="dsa",
    )(iq, ikw, qb, zb, ikw, ckv, g_kv.reshape(1, R_KV), w_uv_bf16, bias_tiles)


def _out_kernel(x_ref, oa_ref, ob_ref, mod_ref, g_ref, w_ref, o_ref):
    mix_in = jnp.concatenate([oa_ref[0], ob_ref[0]], axis=-1)
    mix = _dot(mix_in, w_ref[...])
    ms = jnp.mean(mix * mix, axis=-1, keepdims=True)
    normed = (mix * lax.rsqrt(ms + EPS)) * g_ref[...]
    o_ref[0] = x_ref[0] + mod_ref[0, 2:3, :] * normed


def _output(x, o_a, o_b, mod3, g_post, w_out_bf16, tm):
    bsz, seq, d = x.shape
    return pl.pallas_call(
        _out_kernel,
        grid=(bsz, seq // tm),
        in_specs=[pl.BlockSpec((1, tm, d), lambda b, i: (b, i, 0)),
                  pl.BlockSpec((1, tm, W_A), lambda b, i: (b, i, 0)),
                  pl.BlockSpec((1, tm, W_B), lambda b, i: (b, i, 0)),
                  pl.BlockSpec((1, 3, d), lambda b, i: (b, 0, 0)),
                  pl.BlockSpec((1, d), lambda b, i: (0, 0)),
                  pl.BlockSpec((W_A + W_B, d), lambda b, i: (0, 0))],
        out_specs=pl.BlockSpec((1, tm, d), lambda b, i: (b, i, 0)),
        out_shape=jax.ShapeDtypeStruct((bsz, seq, d), F32),
        compiler_params=pltpu.CompilerParams(dimension_semantics=("arbitrary", "arbitrary"),
                                             vmem_limit_bytes=VMEM_LIMIT),
        name="outproj",
    )(x, o_a, o_b, mod3, g_post.reshape(1, d), w_out_bf16)


def _pad_lanes(w):
    return jnp.pad(w, ((0, 0), (0, LANES - w.shape[1])))


def _pad_input_projection(w_in):
    o_ba = 3 * W_A + W_A
    o_qb = o_ba + 2 * H_A
    o_ik = o_qb + H_B * R_KV + R_KV + W_B + H_IDX * D_IDX
    return jnp.concatenate([w_in[:, :o_ba], _pad_lanes(w_in[:, o_ba:o_qb]),
                            w_in[:, o_qb:o_ik], _pad_lanes(w_in[:, o_ik:])], axis=1)


def _head_lanes(v):
    return jnp.zeros((1, LANES), F32).at[0, H_A:2 * H_A].set(v.astype(F32))


def kernel(x, c, w_ada, b_ada, g_pre, w_in, conv_w, a_log, dt_bias, g_gdn, g_kv, w_uv, rel_bias, w_out, g_post):
    bsz, seq, d = x.shape
    depth = w_ada.shape[0]
    assert seq % TQ == 0 and seq % CHUNK == 0
    k_top = min(TOPK_MAX, seq // 4)
    tm = min(1024, seq)
    tl = min(128, seq)
    rb = 8 if bsz % 8 == 0 else (2 if bsz % 2 == 0 else 1)
    bias_tiles = _bias_tiles(rel_bias, seq // TQ)
    for layer in range(depth):
        mod3 = _modulation(c, w_ada[layer], b_ada[layer]).reshape(bsz, 3, d)
        w_pad = _pad_input_projection(w_in[layer]).astype(BF16)
        qkv, za, ba, qb, ckv, zb, iq, ikw = _projection(x, mod3, g_pre[layer], w_pad, tm)
        o_a = _gated_deltanet(qkv, za, ba, conv_w[layer], _head_lanes(a_log[layer]),
                              _head_lanes(dt_bias[layer]), g_gdn[layer], rb, tl,
                              (tl // CHUNK) * math.gcd(max(8 // (tl // CHUNK), 1), rb))
        o_b = _dsa_attention(iq, ikw, qb, zb, ckv, g_kv[layer], w_uv[layer].astype(BF16),
                             bias_tiles, k_top, 4 if bsz % 4 == 0 else 1)
        x = _output(x, o_a, o_b, mod3, g_post[layer], w_out[layer].astype(BF16),
                    min(2048, seq))
    return x
```

```python
import functools
import math

import jax
import jax.numpy as jnp
from jax import lax
from jax.experimental import pallas as pl
from jax.experimental.pallas import tpu as pltpu

F32 = jnp.float32
BF16 = jnp.bfloat16
I32 = jnp.int32
HIGHEST = lax.Precision.HIGHEST

H_A, DK_A, DV_A = 4, 128, 128
W_A = H_A * DV_A
CONV_W = 4
CHUNK = 64
H_B, R_KV, DV_B = 4, 128, 128
W_B = H_B * DV_B
H_IDX, D_IDX = 8, 64
TOPK_MAX = 256
N_BUCKETS, MAX_EXACT, MAX_DIST = 32, 16, 128
EPS = 1e-6
LOG2E = math.log2(math.e)

LANES = 128
TQ = 128
FIELD = 15
V7X_VMEM_BYTES = 64 * 1024 * 1024
VMEM_LIMIT = V7X_VMEM_BYTES - 6 * 1024 * 1024

C_QKV = (0, 3 * W_A)
C_ZA = (C_QKV[1], C_QKV[1] + W_A)
C_BA = (C_ZA[1], C_ZA[1] + LANES)
C_QB = (C_BA[1], C_BA[1] + H_B * R_KV)
C_CKV = (C_QB[1], C_QB[1] + R_KV)
C_ZB = (C_CKV[1], C_CKV[1] + W_B)
C_IQ = (C_ZB[1], C_ZB[1] + H_IDX * D_IDX)
C_IKW = (C_IQ[1], C_IQ[1] + LANES)
D_IN_PAD = C_IKW[1]
IW_LANE = D_IDX

NEG_INF_KEY = -2139095041


def _sigmoid(x):
    return 1.0 / (1.0 + jnp.exp(-x))


def _silu(x):
    return x * _sigmoid(x)


def _softplus(x):
    return jnp.maximum(x, 0.0) + jnp.log1p(jnp.exp(-jnp.abs(x)))


def _dot(a, b, precision=None):
    return jnp.dot(a, b, precision=precision, preferred_element_type=F32)


def _dot_nt(a, b, precision=None):
    return lax.dot_general(a, b, (((1,), (1,)), ((), ())), precision=precision,
                           preferred_element_type=F32)


def _dot_tn(a, b, precision=None):
    return lax.dot_general(a, b, (((0,), (0,)), ((), ())), precision=precision,
                           preferred_element_type=F32)


def _bdot(a, b):
    return _dot(a.astype(BF16), b.astype(BF16))


def _bdot_nt(a, b):
    return _dot_nt(a.astype(BF16), b.astype(BF16))


def _mod_kernel(c_ref, w_ref, b_ref, o_ref):
    c = c_ref[...]
    o_ref[...] = _dot(_silu(c), w_ref[...], HIGHEST) + b_ref[...]


def _modulation(c, w_ada, b_ada):
    bsz, d = c.shape
    n = w_ada.shape[1]
    tn = 512
    return pl.pallas_call(
        _mod_kernel,
        grid=(n // tn,),
        in_specs=[pl.BlockSpec((bsz, d), lambda j: (0, 0)),
                  pl.BlockSpec((d, tn), lambda j: (0, j)),
                  pl.BlockSpec((1, tn), lambda j: (0, j))],
        out_specs=pl.BlockSpec((bsz, tn), lambda j: (0, j)),
        out_shape=jax.ShapeDtypeStruct((bsz, n), F32),
        compiler_params=pltpu.CompilerParams(dimension_semantics=("arbitrary",),
                                             vmem_limit_bytes=VMEM_LIMIT),
        name="mod",
    )(c, w_ada, b_ada.reshape(1, n))


def _proj_kernel(x_ref, mod_ref, g_ref, w_ref,
                 qkv_ref, za_ref, ba_ref, qb_ref, ckv_ref, zb_ref, iq_ref, ikw_ref, *, tm):
    shift = mod_ref[0, 0:1, :]
    scale = mod_ref[0, 1:2, :]
    half = max(tm // 2, 8)
    for r0 in range(0, tm, half):
        rs = slice(r0, r0 + half)
        x = x_ref[0, rs, :]
        ms = jnp.mean(x * x, axis=-1, keepdims=True)
        xn = x * lax.rsqrt(ms + EPS)
        h = (xn * g_ref[...]) * (1.0 + scale) + shift
        hb = h.astype(BF16)

        def mm(cols):
            return _dot(hb, w_ref[:, cols[0]:cols[1]])

        qkv_ref[0, rs, :] = mm(C_QKV)
        za_ref[0, rs, :] = mm(C_ZA)
        ba_ref[0, rs, :] = mm(C_BA)
        qb_ref[0, rs, :] = mm(C_QB).astype(BF16)
        ckv_ref[0, rs, :] = mm(C_CKV)
        zb_ref[0, rs, :] = mm(C_ZB)
        iq_ref[0, rs, :] = mm(C_IQ).astype(BF16)
        ikw_ref[0, rs, :] = mm(C_IKW)


def _projection(x, mod3, g_pre, w_pad, tm):
    bsz, seq, d = x.shape
    widths = [(C_QKV, F32), (C_ZA, F32), (C_BA, F32), (C_QB, BF16),
              (C_CKV, F32), (C_ZB, F32), (C_IQ, BF16), (C_IKW, F32)]
    out_shape = [jax.ShapeDtypeStruct((bsz, seq, c[1] - c[0]), dt) for c, dt in widths]
    out_specs = [pl.BlockSpec((1, tm, c[1] - c[0]), lambda b, i: (b, i, 0)) for c, _ in widths]
    return pl.pallas_call(
        functools.partial(_proj_kernel, tm=tm),
        grid=(bsz, seq // tm),
        in_specs=[pl.BlockSpec((1, tm, d), lambda b, i: (b, i, 0)),
                  pl.BlockSpec((1, 3, d), lambda b, i: (b, 0, 0)),
                  pl.BlockSpec((1, d), lambda b, i: (0, 0)),
                  pl.BlockSpec((d, D_IN_PAD), lambda b, i: (0, 0),
                               pipeline_mode=pl.Buffered(1))],
        out_specs=out_specs,
        out_shape=out_shape,
        compiler_params=pltpu.CompilerParams(dimension_semantics=("arbitrary", "arbitrary"),
                                             vmem_limit_bytes=VMEM_LIMIT),
        name="proj",
    )(x, mod3, g_pre.reshape(1, d), w_pad)


def _tri_inverse_many(lmats, level_masks, eye):
    ts = [eye - jnp.where(level_masks[0], lm, 0.0) for lm in lmats]
    for m in level_masks[1:]:
        xs = [_bdot(jnp.where(m, lm, 0.0), t) for lm, t in zip(lmats, ts)]
        ts = [t - _bdot(t, x) for t, x in zip(ts, xs)]
    return ts


def _gdn_kernel(qkv_ref, za_ref, ba_ref, cw_ref, alog_ref, dtb_ref, gn_ref, o_ref,
                halo_ref, s_ref, gc_ref, beta_ref, gct_ref, u_ref, wq_ref, kg_ref, at_ref,
                *, rb, tl, group_a):
    n_ch = tl // CHUNK
    assert group_a % n_ch == 0
    li = pl.program_id(1)

    @pl.when(li == 0)
    def _():
        halo_ref[...] = jnp.zeros_like(halo_ref)
        s_ref[...] = jnp.zeros_like(s_ref)

    r_i = lax.broadcasted_iota(I32, (tl, tl), 0)
    c_i = lax.broadcasted_iota(I32, (tl, tl), 1)
    shift = CHUNK.bit_length() - 1
    tri = jnp.where((c_i <= r_i) & ((r_i >> shift) == (c_i >> shift)), 1.0, 0.0)
    for r in range(rb):
        ba = ba_ref[r]
        beta_ref[r] = _sigmoid(ba)
        g = -jnp.exp(alog_ref[...]) * _softplus(ba + dtb_ref[...])
        gc = _dot(tri, g, HIGHEST)
        gc_ref[r] = gc
        gct = gc.T
        for c in range(n_ch):
            gct_ref[r * n_ch + c] = gct[:, c * CHUNK:(c + 1) * CHUNK]

    row = lax.broadcasted_iota(I32, (CHUNK, CHUNK), 0)
    col = lax.broadcasted_iota(I32, (CHUNK, CHUNK), 1)
    causal = row >= col
    strict = row > col
    eye = jnp.where(row == col, 1.0, 0.0)
    level_masks = []
    s = 1
    while s < CHUNK:
        ls = s.bit_length() - 1
        level_masks.append(((row >> (ls + 1)) == (col >> (ls + 1)))
                           & (((row >> ls) & 1) == 1) & (((col >> ls) & 1) == 0))
        s *= 2

    def phase_a(ig, carry):
        probs = []
        for j in range(group_a):
            c = j % n_ch
            r = ig * (group_a // n_ch) + j // n_ch
            it = r * n_ch + c
            base = c * CHUNK
            gc_c = gc_ref[r, pl.ds(base, CHUNK), :]
            beta_c = beta_ref[r, pl.ds(base, CHUNK), :]
            gct_c = gct_ref[it]
            for h in range(H_A):
                def conv_silu(sec):
                    c0 = sec * W_A + h * DK_A
                    if c == 0:
                        win = jnp.concatenate([halo_ref[r, :, c0:c0 + DK_A],
                                               qkv_ref[r, 0:CHUNK, c0:c0 + DK_A]], axis=0)
                    else:
                        win = qkv_ref[r, base - 8:base + CHUNK, c0:c0 + DK_A]
                    w0, w1, w2, w3 = (cw_ref[t:t + 1, c0:c0 + DK_A] for t in range(CONV_W))
                    prev = pltpu.roll(win, 1, axis=0)
                    near = win * w3 + prev * w2
                    far = pltpu.roll(win * w1 + prev * w0, 2, axis=0)
                    return _silu((near + far)[8:8 + CHUNK])

                q = conv_silu(0)
                k = conv_silu(1)
                v = conv_silu(2)
                q = q * lax.rsqrt(jnp.sum(q * q, axis=-1, keepdims=True) + EPS) * (DK_A ** -0.5)
                k = k * lax.rsqrt(jnp.sum(k * k, axis=-1, keepdims=True) + EPS)
                beta = beta_c[:, h:h + 1]
                gcol = gc_c[:, H_A + h:H_A + h + 1]
                grow = gct_c[H_A + h:H_A + h + 1, :]
                glast = gc_c[CHUNK - 1:CHUNK, H_A + h:H_A + h + 1]
                decay = jnp.exp(jnp.where(causal, gcol - grow, -jnp.inf))
                ecol = jnp.exp(gcol)
                kb = k * beta
                hc = slice(h * DV_A, (h + 1) * DV_A)
                wq_ref[it, h, CHUNK:2 * CHUNK, :] = (q * ecol).astype(BF16)
                kg_ref[r, pl.ds(base, CHUNK), hc] = (k * jnp.exp(glast - gcol)).astype(BF16)
                a2 = _bdot_nt(jnp.concatenate([kb, q], axis=0), k)
                at_ref[it, h] = (a2[CHUNK:] * decay).astype(BF16)
                probs.append(dict(
                    r=r, it=it, h=h, base=base, hc=hc,
                    lmat=jnp.where(strict, a2[:CHUNK] * decay, 0.0),
                    rhs=jnp.concatenate([v * beta, kb * ecol], axis=1).astype(BF16)))
        tmats = _tri_inverse_many([p["lmat"] for p in probs], level_masks, eye)
        for p, tmat in zip(probs, tmats):
            uw = _dot(tmat.astype(BF16), p["rhs"])
            u_ref[p["r"], pl.ds(p["base"], CHUNK), p["hc"]] = uw[:, :DV_A]
            wq_ref[p["it"], p["h"], 0:CHUNK, :] = uw[:, DV_A:].astype(BF16)
        return carry

    lax.fori_loop(0, rb * n_ch // group_a, phase_a, 0)
    halo_ref[...] = qkv_ref[:, tl - 8:tl, :]

    def phase_b(c, carry):
        base = pl.multiple_of(c * CHUNK, CHUNK)
        chains = [(r, h) for r in range(rb) for h in range(H_A)]
        hcs = [slice(h * DV_A, (h + 1) * DV_A) for _, h in chains]
        sts = [s_ref[r, h] for r, h in chains]
        wss = [_dot(wq_ref[r * n_ch + c, h], st.astype(BF16)) for (r, h), st in zip(chains, sts)]
        vnbs = [(u_ref[r, pl.ds(base, CHUNK), hc] - ws[:CHUNK]).astype(BF16)
                for (r, h), hc, ws in zip(chains, hcs, wss)]
        upds = [_dot_tn(kg_ref[r, pl.ds(base, CHUNK), hc], vnb)
                for (r, h), hc, vnb in zip(chains, hcs, vnbs)]
        for (r, h), st, upd in zip(chains, sts, upds):
            glast = gc_ref[r, pl.ds(base + CHUNK - 1, 1), :][:, H_A + h:H_A + h + 1]
            s_ref[r, h] = st * jnp.exp(glast) + upd
        for (r, h), hc, ws, vnb in zip(chains, hcs, wss, vnbs):
            o = ws[CHUNK:] + _dot(at_ref[r * n_ch + c, h], vnb)
            on = o * lax.rsqrt(jnp.mean(o * o, axis=-1, keepdims=True) + EPS) * gn_ref[...]
            z = za_ref[r, pl.ds(base, CHUNK), hc]
            o_ref[r, pl.ds(base, CHUNK), hc] = (on * _silu(z)).astype(BF16)
        return carry

    lax.fori_loop(0, n_ch, phase_b, 0, unroll=True)


def _gated_deltanet(qkv, za, ba, conv_w, alog_vec, dtb_vec, g_norm, rb, tl, group_a):
    bsz, seq, _ = qkv.shape
    n_ch = tl // CHUNK
    kern = functools.partial(_gdn_kernel, rb=rb, tl=tl, group_a=group_a)
    return pl.pallas_call(
        kern,
        grid=(bsz // rb, seq // tl),
        in_specs=[pl.BlockSpec((rb, tl, 3 * W_A), lambda b, i: (b, i, 0)),
                  pl.BlockSpec((rb, tl, W_A), lambda b, i: (b, i, 0)),
                  pl.BlockSpec((rb, tl, LANES), lambda b, i: (b, i, 0)),
                  pl.BlockSpec((CONV_W, 3 * W_A), lambda b, i: (0, 0)),
                  pl.BlockSpec((1, LANES), lambda b, i: (0, 0)),
                  pl.BlockSpec((1, LANES), lambda b, i: (0, 0)),
                  pl.BlockSpec((1, DV_A), lambda b, i: (0, 0))],
        out_specs=pl.BlockSpec((rb, tl, W_A), lambda b, i: (b, i, 0)),
        out_shape=jax.ShapeDtypeStruct((bsz, seq, W_A), BF16),
        scratch_shapes=[pltpu.VMEM((rb, 8, 3 * W_A), F32),
                        pltpu.VMEM((rb, H_A, DK_A, DV_A), F32),
                        pltpu.VMEM((rb, tl, LANES), F32),
                        pltpu.VMEM((rb, tl, LANES), F32),
                        pltpu.VMEM((rb * n_ch, LANES, CHUNK), F32),
                        pltpu.VMEM((rb, tl, W_A), F32),
                        pltpu.VMEM((rb * n_ch, H_A, 2 * CHUNK, DK_A), BF16),
                        pltpu.VMEM((rb, tl, W_A), BF16),
                        pltpu.VMEM((rb * n_ch, H_A, CHUNK, CHUNK), BF16)],
        compiler_params=pltpu.CompilerParams(dimension_semantics=("arbitrary", "arbitrary"),
                                             vmem_limit_bytes=VMEM_LIMIT),
        name="gdn",
    )(qkv, za, ba, conv_w, alog_vec, dtb_vec, g_norm.reshape(1, DV_A))


def _bias_kernel(rb_ref, o_ref):
    d = pl.program_id(0)
    kj = lax.broadcasted_iota(I32, (TQ, TQ), 0)
    qi = lax.broadcasted_iota(I32, (TQ, TQ), 1)
    dist = d * TQ + qi - kj
    n = jnp.maximum(dist, 0)
    nf = jnp.maximum(n, 1).astype(F32)
    large = MAX_EXACT + (jnp.log(nf / MAX_EXACT) / math.log(MAX_DIST / MAX_EXACT)
                         * (N_BUCKETS - MAX_EXACT)).astype(I32)
    large = jnp.minimum(large, N_BUCKETS - 1)
    bucket = jnp.where(n < MAX_EXACT, n, large)
    for h in range(H_B):
        acc = jnp.zeros((TQ, TQ), F32)
        for kb in range(N_BUCKETS):
            acc = jnp.where(bucket == kb, rb_ref[kb, h], acc)
        o_ref[0, :, h * TQ:(h + 1) * TQ] = acc * LOG2E


def _bias_tiles(rel_bias, n_diag):
    return pl.pallas_call(
        _bias_kernel,
        grid=(n_diag,),
        in_specs=[pl.BlockSpec(memory_space=pltpu.SMEM)],
        out_specs=pl.BlockSpec((1, TQ, H_B * TQ), lambda d: (d, 0, 0)),
        out_shape=jax.ShapeDtypeStruct((n_diag, TQ, H_B * TQ), F32),
        compiler_params=pltpu.CompilerParams(dimension_semantics=("arbitrary",)),
        name="bias",
    )(rel_bias)


def _fold_keys(x, op):
    x = op(x.reshape(4, x.shape[0] // 32, 8, x.shape[-1]), axis=1)
    return op(x, axis=0)


def _dsa_kernel(iq_ref, ikwq_ref, qb_ref, zb_ref, ikw_ref, ckv_ref, gkv_ref, wuv_ref, bias_ref,
                o_ref,
                kvn_ref, kvt_ref, iklo_ref, ikhi_ref, key_ref, w_ref, am_ref, lg_ref, acc_ref,
                *, rb, seq, k_top):
    qi = pl.program_id(1)
    n_kc = qi + 1
    hq = H_B * TQ
    rows = range(rb)

    @pl.when(qi == 0)
    def _():
        for r in rows:
            for c in range(seq // TQ):
                sl = slice(c * TQ, (c + 1) * TQ)
                ckv = ckv_ref[r, sl, :]
                ms = jnp.mean(ckv * ckv, axis=-1, keepdims=True)
                kvn = (ckv * lax.rsqrt(ms + EPS)) * gkv_ref[...]
                kvn_ref[r, sl, :] = kvn.astype(BF16)
                kvt_ref[r, c] = kvn.T.astype(BF16)
                ikw = ikw_ref[r, sl, :]
                lane = lax.broadcasted_iota(I32, ikw.shape, 1)
                lo = jnp.where(lane < D_IDX, ikw, 0.0)
                iklo_ref[r, sl, :] = lo.astype(BF16)
                ikhi_ref[r, sl, :] = pltpu.roll(lo, D_IDX, axis=1).astype(BF16)

    iq4s, iwts = [], []
    for r in rows:
        iq = iq_ref[r]
        iq4s.append(jnp.concatenate(
            [iq[:, p * LANES:(p + 1) * LANES] for p in range(H_IDX // 2)], axis=0))
        iwts.append((ikwq_ref[r] * (H_IDX ** -0.5 * D_IDX ** -0.5)).T)

    key_j = lax.broadcasted_iota(I32, (TQ, TQ), 0)
    qry_t = qi * TQ + lax.broadcasted_iota(I32, (TQ, TQ), 1)

    def causal_mask(c):
        return (c * TQ + key_j) <= qry_t

    q4s = []
    for r in rows:
        qb = qb_ref[r]
        q4s.append(jnp.concatenate([qb[:, h * R_KV:(h + 1) * R_KV] for h in range(H_B)], axis=0))
    scale = R_KV ** -0.5 * LOG2E

    def pair_loop(body, carry):
        carry = lax.fori_loop(0, n_kc >> 1, lambda i, cr: body((2 * i, 2 * i + 1), cr), carry)
        return lax.cond((n_kc & 1) == 1, lambda cr: body((n_kc - 1,), cr), lambda cr: cr, carry)

    def quad_loop(body, carry):
        n4 = n_kc >> 2
        carry = lax.fori_loop(0, n4, lambda i, cr: body(tuple(4 * i + t for t in range(4)), cr),
                              carry)
        b2 = n4 << 2
        carry = lax.cond((n_kc & 2) != 0, lambda cr: body((b2, b2 + 1), cr), lambda cr: cr, carry)
        b1 = b2 + (n_kc & 2)
        return lax.cond((n_kc & 1) != 0, lambda cr: body((b1,), cr), lambda cr: cr, carry)

    half = TQ // 2
    guards = jnp.int32(-(2 ** 31) + 2 ** 15)
    ones2 = jnp.int32(2 ** 16 + 1)
    fmax = 2 ** FIELD - 1

    def pack_fields(f):
        return (f[half:] << 16) | f[:half] | guards

    def field1(key):
        return lax.shift_right_logical(key, 32 - FIELD) ^ (1 << (FIELD - 1))

    def field2(key):
        return lax.shift_right_logical(key, 32 - 2 * FIELD) & fmax

    def score_chunks(cs, m0s):
        m0s = list(m0s)
        nc = len(cs)
        ks = pl.ds(pl.multiple_of(cs[0] * TQ, TQ), nc * TQ)
        rels = [_dot_nt(jnp.concatenate([iklo_ref[r, ks, :], ikhi_ref[r, ks, :]], axis=0),
                        iq4s[r]) for r in rows]
        sts = [_dot_nt(kvn_ref[r, ks, :], q4s[r]) for r in rows]
        for j, c in enumerate(cs):
            for r in rows:
                re = rels[r][j * TQ:(j + 1) * TQ]
                ro = rels[r][(nc + j) * TQ:(nc + j + 1) * TQ]
                s = None
                for p in range(H_IDX // 2):
                    ps = slice(p * TQ, (p + 1) * TQ)
                    we = iwts[r][IW_LANE + 2 * p:IW_LANE + 2 * p + 1, :]
                    wo = iwts[r][IW_LANE + 2 * p + 1:IW_LANE + 2 * p + 2, :]
                    t = jnp.maximum(re[:, ps], 0.0) * we + jnp.maximum(ro[:, ps], 0.0) * wo
                    s = t if s is None else s + t
                s = jnp.where(s == 0.0, 0.0, s)
                s = jnp.where(causal_mask(c), s, -jnp.inf)
                bits = pltpu.bitcast(s, I32)
                key = jnp.where(bits < 0, bits ^ 0x7FFFFFFF, bits)
                key_ref[r, c] = key
                w_ref[r, c] = pack_fields(field1(key))
                lg = sts[r][j * TQ:(j + 1) * TQ] * scale + bias_ref[qi - c]
                lg_ref[r, c] = lg
                m0s[r] = jnp.maximum(m0s[r], _fold_keys(lg, jnp.max))
        return tuple(m0s)

    m0s = quad_loop(score_chunks, tuple(jnp.full((8, hq), -jnp.inf, F32) for _ in rows))

    kf = float(k_top)
    n_pairs = (n_kc + 1) >> 1
    sign = jnp.int32(-2 ** 31)

    @pl.when((n_kc & 1) == 1)
    def _():
        for r in rows:
            key_ref[r, n_kc] = jnp.full((TQ, TQ), -2 ** 31, I32)

    def count_ge(cands):
        def body(i, accs):
            out = list(accs)
            for c in (2 * i, 2 * i + 1):
                for r in rows:
                    hit = jnp.where(key_ref[r, c] >= cands[r], 1.0, 0.0)
                    out[r] = out[r] + _fold_keys(hit, jnp.sum)
            return tuple(out)
        accs = lax.fori_loop(0, n_pairs, body, tuple(jnp.zeros((8, TQ), F32) for _ in rows))
        return [jnp.sum(a, axis=0, keepdims=True) for a in accs]

    def count_fields(cands):
        pairs = [(cu << 16) | cu for cu in cands]
        def body(i, accs):
            out = list(accs)
            for c in (2 * i, 2 * i + 1):
                for r in rows:
                    d = w_ref[r, c] - pairs[r]
                    hit = lax.shift_right_logical(d, 15) & ones2
                    out[r] = out[r] + _fold_keys(hit, jnp.sum)
            return tuple(out)
        accs = lax.fori_loop(0, n_pairs, body, tuple(jnp.zeros((8, TQ), I32) for _ in rows))
        both = [(a & 0xFFFF) + lax.shift_right_logical(a, 16) for a in accs]
        return [jnp.sum(b, axis=0, keepdims=True).astype(F32) for b in both]

    def field_level(bases, cges, want_above):
        def bit_body(i, carry):
            ts, cg = carry
            bit = lax.shift_left(jnp.int32(1), FIELD - 1 - i)
            cands = [t | bit for t in ts]
            cnts = [b + n for b, n in zip(bases, count_fields([cu[0:1, :] for cu in cands]))]
            take = [cnt >= kf for cnt in cnts]
            return (tuple(jnp.where(tk, cu, t) for tk, cu, t in zip(take, cands, ts)),
                    tuple(jnp.where(tk, cnt, g) for tk, cnt, g in zip(take, cnts, cg)))
        ts, cges = lax.fori_loop(0, FIELD, bit_body,
                                 (tuple(jnp.zeros((8, TQ), I32) for _ in rows), tuple(cges)))
        ps = [t[0:1, :] for t in ts]
        if not want_above:
            return ps, None, cges
        above = count_fields([jnp.minimum(p + 1, fmax) for p in ps])
        above = [b + jnp.where(p == fmax, 0.0, n) for b, p, n in zip(bases, ps, above)]
        return ps, above, cges

    @pl.when((n_kc & 1) == 1)
    def _():
        for r in rows:
            w_ref[r, n_kc] = jnp.full((half, TQ), guards, I32)

    cges = [jnp.full((1, TQ), kf, F32) for _ in rows]
    p1s, above1, cges = field_level([0.0 for _ in rows], cges, True)

    def build_level2(c, carry):
        for r in rows:
            key = key_ref[r, c]
            w_ref[r, c] = pack_fields(jnp.where(field1(key) == p1s[r], field2(key), 0))
        return carry

    lax.fori_loop(0, n_kc, build_level2, 0)
    p2s, _, cges = field_level(above1, cges, False)

    def bit_body(i, carry):
        tus, cges = carry
        bit = lax.shift_left(jnp.int32(1), 31 - i)
        cands = [tu | bit for tu in tus]
        cnts = count_ge([(cu ^ sign)[0:1, :] for cu in cands])
        take = [cnt >= kf for cnt in cnts]
        return (tuple(jnp.where(t, cu, tu) for t, cu, tu in zip(take, cands, tus)),
                tuple(jnp.where(t, cnt, cge) for t, cnt, cge in zip(take, cnts, cges)))

    tus = tuple(jnp.broadcast_to((p1 << (32 - FIELD)) | (p2 << (32 - 2 * FIELD)), (8, TQ))
                for p1, p2 in zip(p1s, p2s))
    tus, cges = lax.fori_loop(2 * FIELD, 32, bit_body, (tus, tuple(cges)))
    thrs = [(tu ^ sign)[0:1, :] for tu in tus]
    qrow = qi * TQ + lax.broadcasted_iota(I32, (1, TQ), 1)
    simple_all = None
    for r in rows:
        simple = (cges[r] == kf) | ((thrs[r] == NEG_INF_KEY) & (qrow < k_top))
        simple_all = simple if simple_all is None else (simple_all & simple)
    all_simple = jnp.min(jnp.where(simple_all, 1.0, 0.0)) > 0.5

    @pl.when(all_simple)
    def _():
        def body(c, carry):
            cm = causal_mask(c)
            for r in rows:
                sel = (key_ref[r, c] >= thrs[r]) & cm
                am_ref[r, c] = jnp.where(sel, 0.0, -jnp.inf)
            return carry
        lax.fori_loop(0, n_kc, body, 0)

    @pl.when(jnp.logical_not(all_simple))
    def _():
        lower = jnp.where(lax.broadcasted_iota(I32, (TQ, TQ), 1)
                          <= lax.broadcasted_iota(I32, (TQ, TQ), 0), 1.0, 0.0).astype(BF16)

        def count_gt(c, cgs):
            return tuple(cg + _fold_keys(jnp.where(key_ref[r, c] > thrs[r], 1.0, 0.0), jnp.sum)
                         for r, cg in zip(rows, cgs))

        cgs = lax.fori_loop(0, n_kc, count_gt, tuple(jnp.zeros((8, TQ), F32) for _ in rows))
        needs = [kf - jnp.sum(cg, axis=0, keepdims=True) for cg in cgs]

        def body(c, seens):
            cm = causal_mask(c)
            out = []
            for r in rows:
                key = key_ref[r, c]
                eq = key == thrs[r]
                eqf = jnp.where(eq, 1.0, 0.0)
                rank = seens[r] + _dot(lower, eqf.astype(BF16))
                sel = (key > thrs[r]) | (eq & (rank <= needs[r]))
                am_ref[r, c] = jnp.where(sel & cm, 0.0, -jnp.inf)
                out.append(seens[r] + jnp.sum(eqf, axis=0, keepdims=True))
            return tuple(out)
        lax.fori_loop(0, n_kc, body, tuple(jnp.zeros((1, TQ), F32) for _ in rows))

    def masked_logits(r, c):
        return lg_ref[r, c] + jnp.concatenate([am_ref[r, c]] * H_B, axis=1)

    def softmax_pv(shifts):
        acc_ref[...] = jnp.zeros_like(acc_ref)

        def body(cs, ls):
            out = list(ls)
            for r in rows:
                ps = [jnp.exp2(masked_logits(r, c) - shifts[r]) for c in cs]
                kvt = jnp.concatenate([kvt_ref[r, c] for c in cs], axis=1)
                acc_ref[r] += _dot(kvt, jnp.concatenate(ps, axis=0).astype(BF16))
                for p in ps:
                    out[r] = out[r] + _fold_keys(p, jnp.sum)
            return tuple(out)

        ls = quad_loop(body, tuple(jnp.zeros((8, hq), F32) for _ in rows))
        return tuple(jnp.sum(l, axis=0, keepdims=True) for l in ls)

    lsums = softmax_pv([jnp.max(m, axis=0, keepdims=True) for m in m0s])
    lmin = jnp.min(jnp.concatenate(lsums, axis=1))

    def exact_shift(_):
        def body(cs, ms):
            out = list(ms)
            for c in cs:
                for r in rows:
                    out[r] = jnp.maximum(out[r], _fold_keys(masked_logits(r, c), jnp.max))
            return tuple(out)
        ms = pair_loop(body, tuple(jnp.full((8, hq), -jnp.inf, F32) for _ in rows))
        return softmax_pv([jnp.max(m, axis=0, keepdims=True) for m in ms])

    lsums = lax.cond(lmin >= 2.0 ** -40, lambda _: lsums, exact_shift, 0)
    for r in rows:
        ot = (acc_ref[r] / lsums[r]).astype(BF16)
        for h in range(H_B):
            hs = slice(h * DV_B, (h + 1) * DV_B)
            y = _dot_tn(ot[:, h * TQ:(h + 1) * TQ], wuv_ref[h])
            o_ref[r, :, hs] = (y * _silu(zb_ref[r, :, hs])).astype(BF16)


def _dsa_attention(iq, ikw, qb, zb, ckv, g_kv, w_uv_bf16, bias_tiles, k_top, rb):
    bsz, seq, _ = iq.shape
    n_q = seq // TQ
    hq = H_B * TQ
    kern = functools.partial(_dsa_kernel, rb=rb, seq=seq, k_top=k_top)
    return pl.pallas_call(
        kern,
        grid=(bsz // rb, n_q),
        in_specs=[pl.BlockSpec((rb, TQ, H_IDX * D_IDX), lambda b, i: (b, i, 0)),
                  pl.BlockSpec((rb, TQ, LANES), lambda b, i: (b, i, 0)),
                  pl.BlockSpec((rb, TQ, H_B * R_KV), lambda b, i: (b, i, 0)),
                  pl.BlockSpec((rb, TQ, W_B), lambda b, i: (b, i, 0)),
                  pl.BlockSpec((rb, seq, LANES), lambda b, i: (b, 0, 0),
                               pipeline_mode=pl.Buffered(1)),
                  pl.BlockSpec((rb, seq, R_KV), lambda b, i: (b, 0, 0),
                               pipeline_mode=pl.Buffered(1)),
                  pl.BlockSpec((1, R_KV), lambda b, i: (0, 0)),
                  pl.BlockSpec((H_B, R_KV, DV_B), lambda b, i: (0, 0, 0)),
                  pl.BlockSpec((n_q, TQ, hq), lambda b, i: (0, 0, 0),
                               pipeline_mode=pl.Buffered(1))],
        out_specs=pl.BlockSpec((rb, TQ, W_B), lambda b, i: (b, i, 0)),
        out_shape=jax.ShapeDtypeStruct((bsz, seq, W_B), BF16),
        scratch_shapes=[pltpu.VMEM((rb, seq, R_KV), BF16),
                        pltpu.VMEM((rb, n_q, R_KV, TQ), BF16),
                        pltpu.VMEM((rb, seq, LANES), BF16),
                        pltpu.VMEM((rb, seq, LANES), BF16),
                        pltpu.VMEM((rb, n_q + 1, TQ, TQ), I32),
                        pltpu.VMEM((rb, n_q + 1, TQ // 2, TQ), I32),
                        pltpu.VMEM((rb, n_q, TQ, TQ), F32),
                        pltpu.VMEM((rb, n_q, TQ, hq), F32),
                        pltpu.VMEM((rb, R_KV, hq), F32)],
        compiler_params=pltpu.CompilerParams(dimension_semantics=("arbitrary", "arbitrary"),
                                             vmem_limit_bytes=VMEM_LIMIT),
        name="dsa",
    )(iq, ikw, qb, zb, ikw, ckv, g_kv.reshape(1, R_KV), w_uv_bf16, bias_tiles)


def _out_kernel(x_ref, oa_ref, ob_ref, mod_ref, g_ref, w_ref, o_ref):
    mix_in = jnp.concatenate([oa_ref[0], ob_ref[0]], axis=-1)
    mix = _dot(mix_in, w_ref[...])
    ms = jnp.mean(mix * mix, axis=-1, keepdims=True)
    normed = (mix * lax.rsqrt(ms + EPS)) * g_ref[...]
    o_ref[0] = x_ref[0] + mod_ref[0, 2:3, :] * normed


def _output(x, o_a, o_b, mod3, g_post, w_out_bf16, tm):
    bsz, seq, d = x.shape
    return pl.pallas_call(
        _out_kernel,
        grid=(bsz, seq // tm),
        in_specs=[pl.BlockSpec((1, tm, d), lambda b, i: (b, i, 0)),
                  pl.BlockSpec((1, tm, W_A), lambda b, i: (b, i, 0)),
                  pl.BlockSpec((1, tm, W_B), lambda b, i: (b, i, 0)),
                  pl.BlockSpec((1, 3, d), lambda b, i: (b, 0, 0)),
                  pl.BlockSpec((1, d), lambda b, i: (0, 0)),
                  pl.BlockSpec((W_A + W_B, d), lambda b, i: (0, 0))],
        out_specs=pl.BlockSpec((1, tm, d), lambda b, i: (b, i, 0)),
        out_shape=jax.ShapeDtypeStruct((bsz, seq, d), F32),
        compiler_params=pltpu.CompilerParams(dimension_semantics=("arbitrary", "arbitrary"),
                                             vmem_limit_bytes=VMEM_LIMIT),
        name="outproj",
    )(x, o_a, o_b, mod3, g_post.reshape(1, d), w_out_bf16)


def _pad_lanes(w):
    return jnp.pad(w, ((0, 0), (0, LANES - w.shape[1])))


def _pad_input_projection(w_in):
    o_ba = 3 * W_A + W_A
    o_qb = o_ba + 2 * H_A
    o_ik = o_qb + H_B * R_KV + R_KV + W_B + H_IDX * D_IDX
    return jnp.concatenate([w_in[:, :o_ba], _pad_lanes(w_in[:, o_ba:o_qb]),
                            w_in[:, o_qb:o_ik], _pad_lanes(w_in[:, o_ik:])], axis=1)


def _head_lanes(v):
    return jnp.zeros((1, LANES), F32).at[0, H_A:2 * H_A].set(v.astype(F32))


def kernel(x, c, w_ada, b_ada, g_pre, w_in, conv_w, a_log, dt_bias, g_gdn, g_kv, w_uv, rel_bias, w_out, g_post):
    bsz, seq, d = x.shape
    depth = w_ada.shape[0]
    assert seq % TQ == 0 and seq % CHUNK == 0
    k_top = min(TOPK_MAX, seq // 4)
    tm = min(1024, seq)
    tl = min(128, seq)
    rb = 8 if bsz % 8 == 0 else (2 if bsz % 2 == 0 else 1)
    bias_tiles = _bias_tiles(rel_bias, seq // TQ)
    for layer in range(depth):
        mod3 = _modulation(c, w_ada[layer], b_ada[layer]).reshape(bsz, 3, d)
        w_pad = _pad_input_projection(w_in[layer]).astype(BF16)
        qkv, za, ba, qb, ckv, zb, iq, ikw = _projection(x, mod3, g_pre[layer], w_pad, tm)
        o_a = _gated_deltanet(qkv, za, ba, conv_w[layer], _head_lanes(a_log[layer]),
                              _head_lanes(dt_bias[layer]), g_gdn[layer], rb, tl,
                              (tl // CHUNK) * math.gcd(max(8 // (tl // CHUNK), 1), rb))
        o_b = _dsa_attention(iq, ikw, qb, zb, ckv, g_kv[layer], w_uv[layer].astype(BF16),
                             bias_tiles, k_top, 4 if bsz % 4 == 0 else 1)
        x = _output(x, o_a, o_b, mod3, g_post[layer], w_out[layer].astype(BF16),
                    min(2048, seq))
    return x
```
